```python
import math
import jax
import jax.numpy as jnp
from jax import lax
import numpy as np

D_MODEL = 1024
BATCH = 16
SEQ = 256
DEPTH = 2
DEC_BATCH = 4
DEC_SEQ = 1024
PAST_LEN = 512

GRID_W = 64
HD_A = 64
N_HEADS_A = D_MODEL // (2 * HD_A)
Q_BLOCK = 128
ROPE_BASE = 10000.0
N_HEADS_B = 4
DK_B = D_MODEL // 2 // N_HEADS_B
DV_B = D_MODEL // N_HEADS_B
GK_RANK = 16
GATE_NORM = 16.0
GLA_CHUNK = 64
N_EXPERTS = 32
TOP_K = 4
MOE_FF = D_MODEL
SWIGLU_LIMIT = 7.0
SWIGLU_ALPHA = 1.702
MOE_BLOCK = 128
EPS = 1e-6
A_QK = N_HEADS_A * 2 * HD_A
A_V = N_HEADS_A * 2 * HD_A
B_QK = N_HEADS_B * DK_B
B_V = N_HEADS_B * DV_B
B_LR = 2 * GK_RANK
N_GATES = 2 * D_MODEL
N_IN = 2 * A_QK + A_V + 2 * B_QK + 2 * B_V + B_LR + N_GATES

kernel_name = 'diff_gla_moe_prefix_dit_step'


def rmsnorm(x, g):
    xf = x.astype(jnp.float32)
    y = xf * lax.rsqrt(jnp.mean(xf * xf, axis=-1, keepdims=True) + EPS)
    return y.astype(x.dtype) * g


def axial_rope(x, n_tok):
    rows = n_tok // GRID_W
    row = jnp.repeat(jnp.arange(rows), GRID_W)
    col = jnp.tile(jnp.arange(GRID_W), rows)
    half = HD_A // 2
    nf = half // 2
    inv = ROPE_BASE ** (-jnp.arange(nf, dtype=jnp.float32) / nf)

    def rot(xa, pos):
        ang = pos.astype(jnp.float32)[:, None] * inv
        cos = jnp.cos(ang)[None, :, None, None, :]
        sin = jnp.sin(ang)[None, :, None, None, :]
        x1 = xa[..., :nf].astype(jnp.float32)
        x2 = xa[..., nf:].astype(jnp.float32)
        return jnp.concatenate([x1 * cos - x2 * sin, x2 * cos + x1 * sin], axis=-1)

    out = jnp.concatenate([rot(x[..., :half], row), rot(x[..., half:], col)], axis=-1)
    return out.astype(x.dtype)


def diff_attention(q, k, v, lam):
    B, Tq = q.shape[0], q.shape[1]
    nb = Tq // Q_BLOCK
    qb = jnp.moveaxis(q.reshape(B, nb, Q_BLOCK, N_HEADS_A, 2, HD_A), 1, 0)

    def block(qblk):
        s = jnp.einsum('bqhmd,bkhmd->bhmqk', qblk, k).astype(jnp.float32) * (HD_A ** -0.5)
        pr = jax.nn.softmax(s, axis=-1)
        a = pr[:, :, 0] - lam * pr[:, :, 1]
        return jnp.einsum('bhqk,bkhe->bqhe', a.astype(v.dtype), v)

    o = lax.map(block, qb)
    return jnp.moveaxis(o, 0, 1).reshape(B, Tq, N_HEADS_A, 2 * HD_A)


def gla_chunk_scan(q, k, v, g, s0):
    B, T, H = q.shape[0], q.shape[1], q.shape[2]
    dv = v.shape[-1]
    n = T // GLA_CHUNK

    def chunks(a):
        return a.astype(jnp.float32).reshape(B, n, GLA_CHUNK, H, a.shape[-1]).transpose(1, 0, 3, 2, 4)

    causal = jnp.tril(jnp.ones((GLA_CHUNK, GLA_CHUNK), dtype=bool))[:, :, None]

    def step(S, inp):
        qc, kc, vc, gc = inp
        b = jnp.cumsum(gc, axis=2)
        inter = jnp.einsum('bhid,bhde->bhie', qc * jnp.exp(b), S)
        diff = b[:, :, :, None, :] - b[:, :, None, :, :]
        decay = jnp.where(causal, jnp.exp(jnp.where(causal, diff, 0.0)), 0.0)
        att = jnp.einsum('bhid,bhjd,bhijd->bhij', qc, kc, decay)
        intra = jnp.einsum('bhij,bhje->bhie', att, vc)
        b_last = b[:, :, -1:, :]
        S_new = S * jnp.exp(b_last[:, :, 0, :, None]) + jnp.einsum('bhjd,bhje->bhde', kc * jnp.exp(b_last - b), vc)
        return S_new, inter + intra

    S_fin, o = lax.scan(step, s0.astype(jnp.float32), (chunks(q), chunks(k), chunks(v), chunks(g)))
    o = o.transpose(1, 0, 3, 2, 4).reshape(B, T, H, dv)
    return o.astype(v.dtype), S_fin


def token_mixer(h, p, l, lam_init, ctx_k, ctx_v, s0_f, s0_b):
    B, T, _ = h.shape
    proj = h @ p['w_in'][l]
    offs = []
    acc = 0
    for size in (A_QK, A_QK, A_V, B_QK, B_QK, B_V, B_V, B_LR):
        acc += size
        offs.append(acc)
    a_q, a_k, a_v, b_q, b_k, b_v, b_g, b_lr, gates = jnp.split(proj, offs, axis=-1)

    q = a_q.reshape(B, T, N_HEADS_A, 2, HD_A)
    k = a_k.reshape(B, T, N_HEADS_A, 2, HD_A)
    v = a_v.reshape(B, T, N_HEADS_A, 2 * HD_A)
    if ctx_k is None:
        q_use, k_all, v_all = q, k, v
    else:
        q_use = axial_rope(q, T)
        k_all = jnp.concatenate([ctx_k.astype(k.dtype), axial_rope(k, T)], axis=1)
        v_all = jnp.concatenate([ctx_v.astype(v.dtype), v], axis=1)
    lq = p['diff_lambda'][l].astype(jnp.float32)
    lam = jnp.exp(jnp.sum(lq[0] * lq[1])) - jnp.exp(jnp.sum(lq[2] * lq[3])) + lam_init
    o_a = diff_attention(q_use, k_all, v_all, lam)
    o_a = rmsnorm(o_a, p['diff_subln_g'][l]) * (1.0 - lam_init)
    y_a = o_a.reshape(B, T, A_V) @ p['w_branch_a'][l]

    qg = b_q.reshape(B, T, N_HEADS_B, DK_B) * (DK_B ** -0.5)
    kg = b_k.reshape(B, T, N_HEADS_B, DK_B)
    vg = b_v.reshape(B, T, N_HEADS_B, DV_B)
    lr_f, lr_b = jnp.split(b_lr, 2, axis=-1)
    gk_f = (jax.nn.log_sigmoid(lr_f @ p['gla_gk_w'][l, 0] + p['gla_gk_b'][l, 0]) / GATE_NORM).reshape(B, T, N_HEADS_B, DK_B)
    gk_b = (jax.nn.log_sigmoid(lr_b @ p['gla_gk_w'][l, 1] + p['gla_gk_b'][l, 1]) / GATE_NORM).reshape(B, T, N_HEADS_B, DK_B)
    if s0_f is None:
        s0_f = jnp.zeros((B, N_HEADS_B, DK_B, DV_B), jnp.float32)
        s0_b = jnp.zeros((B, N_HEADS_B, DK_B, DV_B), jnp.float32)
    o_f, s_f = gla_chunk_scan(qg, kg, vg, gk_f, s0_f)
    o_b, s_b = gla_chunk_scan(qg[:, ::-1], kg[:, ::-1], vg[:, ::-1], gk_b[:, ::-1], s0_b)
    o_g = o_f + o_b[:, ::-1]
    o_g = rmsnorm(o_g, p['gla_norm_g'][l]) * jax.nn.silu(b_g.reshape(B, T, N_HEADS_B, DV_B))
    y_b = o_g.reshape(B, T, B_V) @ p['w_branch_b'][l]

    g_a, g_b = jnp.split(gates, 2, axis=-1)
    merged = jax.nn.sigmoid(g_a) * y_a + jax.nn.sigmoid(g_b) * y_b
    return merged @ p['w_out'][l], (k, v, s_f, s_b)


def moe(h, p, l):
    T, D = h.shape
    TK = T * TOP_K
    logits = (h @ p['router_w'][l] + p['router_b'][l]).astype(jnp.float32)
    top_val, top_idx = lax.top_k(logits, TOP_K)
    gates = jax.nn.softmax(top_val, axis=-1).astype(h.dtype)
    flat_e = top_idx.reshape(TK)
    flat_tok = jnp.repeat(jnp.arange(T, dtype=jnp.int32), TOP_K)
    flat_gate = gates.reshape(TK)
    order = jnp.argsort(flat_e)
    e_sorted = flat_e[order]
    counts = jnp.bincount(flat_e, length=N_EXPERTS)
    padded = (counts + MOE_BLOCK - 1) // MOE_BLOCK * MOE_BLOCK
    pad_end = jnp.cumsum(padded)
    pad_start = pad_end - padded
    start = jnp.cumsum(counts) - counts
    dest = pad_start[e_sorted] + jnp.arange(TK, dtype=jnp.int32) - start[e_sorted]
    n_blocks = (TK + N_EXPERTS * (MOE_BLOCK - 1)) // MOE_BLOCK + 1
    n_rows = n_blocks * MOE_BLOCK
    row_tok = jnp.full((n_rows,), T, jnp.int32).at[dest].set(flat_tok[order])
    row_gate = jnp.zeros((n_rows,), h.dtype).at[dest].set(flat_gate[order])
    blk_e = jnp.minimum(jnp.searchsorted(pad_end, jnp.arange(n_blocks) * MOE_BLOCK, side='right'), N_EXPERTS - 1)
    h_pad = jnp.concatenate([h, jnp.zeros((1, D), h.dtype)], axis=0)
    xb = h_pad[row_tok].reshape(n_blocks, MOE_BLOCK, D)
    w_gu = p['expert_w_gu'][l]
    b_gu = p['expert_b_gu'][l]
    w_dn = p['expert_w_down'][l]
    b_dn = p['expert_b_down'][l]

    def expert_block(args):
        xblk, e = args
        gu = xblk @ w_gu[e] + b_gu[e]
        gt = jnp.minimum(gu[:, :MOE_FF], SWIGLU_LIMIT)
        up = jnp.clip(gu[:, MOE_FF:], -SWIGLU_LIMIT, SWIGLU_LIMIT)
        act = (up + 1.0) * gt * jax.nn.sigmoid(SWIGLU_ALPHA * gt)
        return act @ w_dn[e] + b_dn[e]

    yb = lax.map(expert_block, (xb, blk_e)).reshape(n_rows, D)
    return jax.ops.segment_sum(yb * row_gate[:, None], row_tok, num_segments=T + 1)[:T]


def trunk_layer(x, mod, p, l, lam_init, ctx_k, ctx_v, s0_f, s0_b):
    shift_a, scale_a, gate_a, shift_m, scale_m, gate_m = jnp.split(mod, 6, axis=-1)
    h = rmsnorm(x, p['norm_mix_g'][l]) * (1.0 + scale_a) + shift_a
    mix, ctx_out = token_mixer(h, p, l, lam_init, ctx_k, ctx_v, s0_f, s0_b)
    x = x + gate_a * mix
    h = rmsnorm(x, p['norm_ffn_g'][l]) * (1.0 + scale_m) + shift_m
    B, T, D = x.shape
    x = x + gate_m * moe(h.reshape(B * T, D), p, l).reshape(B, T, D)
    return x, ctx_out


def setup_inputs(seed: int = 0) -> dict:
    key = jax.random.key(seed)
    ks = jax.random.split(key, 32)
    f32 = jnp.float32

    def nrm(k, shape, scale):
        return jax.random.normal(k, shape, f32) * scale

    return {
        'x_prompt': nrm(ks[0], (BATCH, SEQ, D_MODEL), 1.0),
        'x_sample': nrm(ks[1], (DEC_BATCH, DEC_SEQ, D_MODEL), 1.0),
        'cache_k': nrm(ks[2], (DEC_BATCH, DEPTH, PAST_LEN, N_HEADS_A, 2, HD_A), 1.0),
        'cache_v': nrm(ks[3], (DEC_BATCH, DEPTH, PAST_LEN, N_HEADS_A, 2 * HD_A), 1.0),
        'state_fwd': nrm(ks[4], (DEC_BATCH, DEPTH, N_HEADS_B, DK_B, DV_B), 2.0),
        'state_bwd': nrm(ks[5], (DEC_BATCH, DEPTH, N_HEADS_B, DK_B, DV_B), 2.0),
        'c': nrm(ks[6], (DEC_BATCH, D_MODEL), 1.0),
        'c_ctx': nrm(ks[7], (D_MODEL,), 1.0),
        'ada_w': nrm(ks[8], (DEPTH, D_MODEL, 6 * D_MODEL), 0.5 * D_MODEL ** -0.5),
        'ada_b': nrm(ks[9], (DEPTH, 6 * D_MODEL), 0.02),
        'norm_mix_g': 1.0 + nrm(ks[10], (DEPTH, D_MODEL), 0.02),
        'norm_ffn_g': 1.0 + nrm(ks[11], (DEPTH, D_MODEL), 0.02),
        'w_in': nrm(ks[12], (DEPTH, D_MODEL, N_IN), D_MODEL ** -0.5),
        'diff_lambda': nrm(ks[13], (DEPTH, 4, HD_A), 0.1),
        'diff_subln_g': 1.0 + nrm(ks[14], (DEPTH, 2 * HD_A), 0.02),
        'gla_gk_w': nrm(ks[15], (DEPTH, 2, GK_RANK, B_QK), GK_RANK ** -0.5),
        'gla_gk_b': nrm(ks[16], (DEPTH, 2, B_QK), 0.1),
        'gla_norm_g': 1.0 + nrm(ks[17], (DEPTH, DV_B), 0.02),
        'w_branch_a': nrm(ks[18], (DEPTH, A_V, D_MODEL), A_V ** -0.5),
        'w_branch_b': nrm(ks[19], (DEPTH, B_V, D_MODEL), B_V ** -0.5),
        'w_out': nrm(ks[20], (DEPTH, D_MODEL, D_MODEL), D_MODEL ** -0.5),
        'router_w': nrm(ks[21], (DEPTH, D_MODEL, N_EXPERTS), D_MODEL ** -0.5),
        'router_b': nrm(ks[22], (DEPTH, N_EXPERTS), 0.01),
        'expert_w_gu': nrm(ks[23], (DEPTH, N_EXPERTS, D_MODEL, 2 * MOE_FF), D_MODEL ** -0.5),
        'expert_b_gu': nrm(ks[24], (DEPTH, N_EXPERTS, 2 * MOE_FF), 0.02),
        'expert_w_down': nrm(ks[25], (DEPTH, N_EXPERTS, MOE_FF, D_MODEL), MOE_FF ** -0.5),
        'expert_b_down': nrm(ks[26], (DEPTH, N_EXPERTS, D_MODEL), 0.02),
        'final_norm_g': 1.0 + nrm(ks[27], (D_MODEL,), 0.02),
    }


def reference(x_prompt, x_sample, cache_k, cache_v, state_fwd, state_bwd, c, c_ctx,
              ada_w, ada_b, norm_mix_g, norm_ffn_g, w_in, diff_lambda, diff_subln_g,
              gla_gk_w, gla_gk_b, gla_norm_g, w_branch_a, w_branch_b, w_out,
              router_w, router_b, expert_w_gu, expert_b_gu, expert_w_down, expert_b_down,
              final_norm_g):
    p = {
        'norm_mix_g': norm_mix_g, 'norm_ffn_g': norm_ffn_g, 'w_in': w_in,
        'diff_lambda': diff_lambda, 'diff_subln_g': diff_subln_g,
        'gla_gk_w': gla_gk_w, 'gla_gk_b': gla_gk_b, 'gla_norm_g': gla_norm_g,
        'w_branch_a': w_branch_a, 'w_branch_b': w_branch_b, 'w_out': w_out,
        'router_w': router_w, 'router_b': router_b,
        'expert_w_gu': expert_w_gu, 'expert_b_gu': expert_b_gu,
        'expert_w_down': expert_w_down, 'expert_b_down': expert_b_down,
    }

    x = x_prompt
    ks, vs, sfs, sbs = [], [], [], []
    for l in range(DEPTH):
        lam_init = 0.8 - 0.6 * math.exp(-0.3 * l)
        mod = (jax.nn.silu(c_ctx) @ ada_w[l] + ada_b[l])[None, None, :]
        x, (k, v, s_f, s_b) = trunk_layer(x, mod, p, l, lam_init, None, None, None, None)
        ks.append(k)
        vs.append(v)
        sfs.append(s_f.astype(x_prompt.dtype))
        sbs.append(s_b.astype(x_prompt.dtype))
    y_prompt = rmsnorm(x, final_norm_g)

    x = x_sample
    for l in range(DEPTH):
        lam_init = 0.8 - 0.6 * math.exp(-0.3 * l)
        mod = (jax.nn.silu(c) @ ada_w[l] + ada_b[l])[:, None, :]
        x, _ = trunk_layer(x, mod, p, l, lam_init, cache_k[:, l], cache_v[:, l], state_fwd[:, l], state_bwd[:, l])
    y_sample = rmsnorm(x, final_norm_g)

    new_cache_k = jnp.stack(ks, axis=1)
    new_cache_v = jnp.stack(vs, axis=1)
    new_state_fwd = jnp.stack(sfs, axis=1)
    new_state_bwd = jnp.stack(sbs, axis=1)
    return (y_prompt, y_sample, new_cache_k, new_cache_v, new_state_fwd, new_state_bwd)
```

```python
import functools
import math

import jax
import jax.numpy as jnp
from jax import lax
from jax.experimental import pallas as pl
from jax.experimental.pallas import tpu as pltpu

f32 = jnp.float32
bf16 = jnp.bfloat16
i32 = jnp.int32

D = 1024
BATCH = 16
SEQ = 256
DEPTH = 2
DEC_BATCH = 4
DEC_SEQ = 1024
PAST_LEN = 512
GRID_W = 64
HD_A = 64
N_HEADS_A = 8
ROPE_BASE = 10000.0
N_HEADS_B = 4
DK_B = 128
DV_B = 256
GK_RANK = 16
GATE_NORM = 16.0
CHUNK = 64
N_EXPERTS = 32
TOP_K = 4
MOE_FF = 1024
SWIGLU_LIMIT = 7.0
SWIGLU_ALPHA = 1.702
EPS = 1e-6

T_P = BATCH * SEQ
T_S = DEC_BATCH * DEC_SEQ
T = T_P + T_S
TM = 256
N_TILES = T // TM
P_TILES = T_P // TM
S_TILES_PER_BATCH = DEC_SEQ // TM

MOE_ROWS = 128
N_BLOCKS = (T * TOP_K + N_EXPERTS * (MOE_ROWS - 1)) // MOE_ROWS + 1
N_ROWS = N_BLOCKS * MOE_ROWS
DUMP_SLOT = TOP_K * T
N_DUMP = 4 * MOE_ROWS
SUB = 8
LANES = 128
FAST_DECAY_LIMIT = -60.0

_VMEM_LIMIT = 56 * 1024 * 1024


def _cparams(sem, vmem=_VMEM_LIMIT):
    return pltpu.CompilerParams(dimension_semantics=sem, vmem_limit_bytes=vmem)


def _dot(a, b):
    return jnp.dot(a, b, preferred_element_type=f32)


def _dot_nt(a, b):
    return lax.dot_general(a, b, (((1,), (1,)), ((), ())), preferred_element_type=f32)


def _split(x):
    hi = x.astype(bf16)
    lo = (x - hi.astype(f32)).astype(bf16)
    return hi, lo


def _dot3(a, b):
    ah, al = _split(a)
    bh, bl = _split(b)
    return _dot(ah, bh) + _dot(ah, bl) + _dot(al, bh)


def _dot3_nt(a, b):
    ah, al = _split(a)
    bh, bl = _split(b)
    return _dot_nt(ah, bh) + _dot_nt(ah, bl) + _dot_nt(al, bh)


def _silu(x):
    return x * jax.nn.sigmoid(x)


def _mod_row(i):
    return jnp.where(i < P_TILES, 0, 1 + lax.shift_right_logical(jnp.maximum(i - P_TILES, 0), 2))


def _ada_kernel(c_ref, w_ref, b_ref, o_ref):
    o_ref[0] = _dot3(_silu(c_ref[...]), w_ref[0]) + b_ref[0]


def _ada_mod(cvec, ada_w, ada_b):
    tn = 1536
    return pl.pallas_call(
        _ada_kernel,
        grid=(DEPTH, 6 * D // tn),
        in_specs=[
            pl.BlockSpec((8, D), lambda l, j: (0, 0)),
            pl.BlockSpec((1, D, tn), lambda l, j: (l, 0, j)),
            pl.BlockSpec((1, 1, tn), lambda l, j: (l, 0, j)),
        ],
        out_specs=pl.BlockSpec((1, 8, tn), lambda l, j: (l, 0, j)),
        out_shape=jax.ShapeDtypeStruct((DEPTH, 8, 6 * D), f32),
        compiler_params=_cparams(("arbitrary", "arbitrary")),
        name="ada_mod",
    )(cvec, ada_w, ada_b.reshape(DEPTH, 1, 6 * D))


def _inproj_kernel(x_ref, mod_ref, g_ref, wm_ref, wg_ref, wl_ref, cos_ref, sin_ref,
                   q_ref, k_ref, v_ref, bq_ref, bk_ref, bv_ref, bg_ref, lr_ref, ga_ref, gb_ref):
    i = pl.program_id(0)
    m = mod_ref[pl.ds(_mod_row(i), 1), :]
    shift = m[:, 0:D]
    scale = m[:, D:2 * D]
    x = x_ref[...]
    h = x * lax.rsqrt(jnp.mean(x * x, axis=-1, keepdims=True) + EPS) * g_ref[...]
    hb = (h * (1.0 + scale) + shift).astype(bf16)

    cos = cos_ref[...]
    sin = sin_ref[...]
    lane = lax.broadcasted_iota(i32, (TM, LANES), 1)
    lo_mask = (lane & 31) < 16

    def rope(a):
        partner = jnp.where(lo_mask, pltpu.roll(a, LANES - 16, 1), pltpu.roll(a, 16, 1))
        return a * cos + partner * sin

    aq = _dot(hb, wm_ref[:, 0:D])
    for c in range(D // LANES):
        q_ref[:, c * LANES:(c + 1) * LANES] = rope(aq[:, c * LANES:(c + 1) * LANES]).astype(bf16)
    ak = _dot(hb, wm_ref[:, D:2 * D])
    for c in range(D // LANES):
        k_ref[:, c * LANES:(c + 1) * LANES] = rope(ak[:, c * LANES:(c + 1) * LANES])
    v_ref[...] = _dot(hb, wm_ref[:, 2 * D:3 * D])
    bq_ref[...] = _dot(hb, wm_ref[:, 3 * D:3 * D + 512])
    bk_ref[...] = _dot(hb, wm_ref[:, 3 * D + 512:4 * D])
    bv_ref[...] = _dot(hb, wm_ref[:, 4 * D:5 * D])
    bg_ref[...] = _dot(hb, wm_ref[:, 5 * D:6 * D])
    lr_ref[...] = _dot(hb, wl_ref[...])
    ga_ref[...] = _dot(hb, wg_ref[:, 0:D])
    gb_ref[...] = _dot(hb, wg_ref[:, D:2 * D])


def _inproj(x, mod, g, wm, wg, wl, cos, sin):
    tile = lambda w: pl.BlockSpec((TM, w), lambda i: (i, 0))
    res = lambda shape: pl.BlockSpec(shape, lambda i: (0, 0), pipeline_mode=pl.Buffered(1))
    widths = (D, D, D, 512, 512, D, D, LANES, D, D)
    dtypes = (bf16,) + (f32,) * 9
    return pl.pallas_call(
        _inproj_kernel,
        grid=(N_TILES,),
        in_specs=[tile(D), res((8, 6 * D)), res((1, D)), res((D, 6 * D)), res((D, 2 * D)),
                  res((D, LANES)), tile(LANES), tile(LANES)],
        out_specs=[tile(w) for w in widths],
        out_shape=[jax.ShapeDtypeStruct((T, w), dt) for w, dt in zip(widths, dtypes)],
        compiler_params=_cparams(("arbitrary",)),
        name="inproj",
    )(x, mod, g, wm, wg, wl, cos, sin)


def _lam_from(lq_ref, lam_init):
    lq = lq_ref[...]
    s01 = jnp.sum(lq[0:1] * lq[1:2], axis=-1, keepdims=True)
    s23 = jnp.sum(lq[2:3] * lq[3:4], axis=-1, keepdims=True)
    return jnp.exp(s01) - jnp.exp(s23) + lam_init


def _subln(o, g, lam_init):
    o = o * lax.rsqrt(jnp.mean(o * o, axis=-1, keepdims=True) + EPS)
    return o * g * (1.0 - lam_init)


def _attn_ctx_kernel(q_ref, k_ref, v_ref, lq_ref, g_ref, o_ref, *, lam_init):
    q = q_ref[...]
    kb = k_ref[...].astype(bf16)
    vb = v_ref[...].astype(bf16)
    lo = lax.broadcasted_iota(i32, q.shape, 1) < HD_A
    outs = []
    for qm in (jnp.where(lo, q, 0), jnp.where(lo, 0, q)):
        s = _dot_nt(qm, kb) * (HD_A ** -0.5)
        p = jnp.exp(s - jnp.max(s, axis=-1, keepdims=True))
        l = jnp.sum(p, axis=-1, keepdims=True)
        outs.append(_dot(p.astype(bf16), vb) / l)
    o = outs[0] - _lam_from(lq_ref, lam_init) * outs[1]
    o_ref[...] = _subln(o, g_ref[...], lam_init).astype(bf16)


def _attn_lat_kernel(q_ref, k_ref, v_ref, ck_ref, cv_ref, lq_ref, g_ref, o_ref, *, lam_init):
    q = q_ref[...]
    kb = k_ref[...].astype(bf16)
    vb = v_ref[...].astype(bf16)
    ckb = ck_ref[...].astype(bf16)
    cvb = cv_ref[...].astype(bf16)
    lo = lax.broadcasted_iota(i32, q.shape, 1) < HD_A
    outs = []
    for qm in (jnp.where(lo, q, 0), jnp.where(lo, 0, q)):
        sc = _dot_nt(qm, ckb) * (HD_A ** -0.5)
        ss = _dot_nt(qm, kb) * (HD_A ** -0.5)
        mx = jnp.maximum(jnp.max(sc, axis=-1, keepdims=True), jnp.max(ss, axis=-1, keepdims=True))
        pc = jnp.exp(sc - mx)
        ps = jnp.exp(ss - mx)
        l = jnp.sum(pc, axis=-1, keepdims=True) + jnp.sum(ps, axis=-1, keepdims=True)
        outs.append((_dot(pc.astype(bf16), cvb) + _dot(ps.astype(bf16), vb)) / l)
    o = outs[0] - _lam_from(lq_ref, lam_init) * outs[1]
    o_ref[...] = _subln(o, g_ref[...], lam_init).astype(bf16)


def _attn_ctx(q, k, v, lq, g, lam_init):
    blk = pl.BlockSpec((SEQ, LANES), lambda b, h: (b, h))
    small = lambda s: pl.BlockSpec(s, lambda b, h: (0, 0))
    return pl.pallas_call(
        functools.partial(_attn_ctx_kernel, lam_init=lam_init),
        grid=(BATCH, N_HEADS_A),
        in_specs=[blk, blk, blk, small((8, LANES)), small((1, LANES))],
        out_specs=blk,
        out_shape=jax.ShapeDtypeStruct((T_P, D), bf16),
        compiler_params=_cparams(("arbitrary", "arbitrary")),
        name="attn_ctx",
    )(q, k, v, lq, g)


def _attn_lat(q, k, v, ck, cv, lq, g, lam_init, layer):
    nq = DEC_SEQ // TM
    qblk = pl.BlockSpec((TM, LANES), lambda b, h, t: (P_TILES + b * nq + t, h))
    kblk = pl.BlockSpec((DEC_SEQ, LANES), lambda b, h, t: (T_P // DEC_SEQ + b, h))
    cblk = pl.BlockSpec((None, None, PAST_LEN, LANES), lambda b, h, t: (b, layer, 0, h))
    small = lambda s: pl.BlockSpec(s, lambda b, h, t: (0, 0))
    return pl.pallas_call(
        functools.partial(_attn_lat_kernel, lam_init=lam_init),
        grid=(DEC_BATCH, N_HEADS_A, nq),
        in_specs=[qblk, kblk, kblk, cblk, cblk, small((8, LANES)), small((1, LANES))],
        out_specs=pl.BlockSpec((TM, LANES), lambda b, h, t: (b * nq + t, h)),
        out_shape=jax.ShapeDtypeStruct((T_S, D), bf16),
        compiler_params=_cparams(("arbitrary", "arbitrary", "arbitrary")),
        name="attn_lat",
    )(q, k, v, ck, cv, lq, g)


def _log_decay(lr, w_ref, b_ref):
    z = _dot3(lr, w_ref[...]) + b_ref[...]
    return -(jnp.maximum(-z, 0.0) + jnp.log(1.0 + jnp.exp(-jnp.abs(z)))) / GATE_NORM


def _gla_direction(q_ref, k_ref, v3b, g, st0, reverse, b_ref, att_ref, sall_ref):
    tb = g.shape[0]
    n = tb // CHUNK
    scale = DK_B ** -0.5
    ri = lax.broadcasted_iota(i32, (CHUNK, CHUNK), 0)
    ci = lax.broadcasted_iota(i32, (CHUNK, CHUNK), 1)
    keep = (ci >= ri) if reverse else (ci <= ri)
    tri = jnp.broadcast_to(jnp.where(keep, 1.0, 0.0).astype(bf16)[None], (n, CHUNK, CHUNK))

    g3 = g.reshape(n, CHUNK, DK_B)
    ghi, glo = _split(g3)
    bdot = lambda a, b: lax.dot_general(a, b, (((2,), (1,)), ((0,), (0,))), preferred_element_type=f32)
    b = bdot(tri, ghi) + bdot(tri, glo)
    tot = b[:, 0:1, :] if reverse else b[:, CHUNK - 1:CHUNK, :]
    q3 = q_ref[...].reshape(n, CHUNK, DK_B) * scale
    k3 = k_ref[...].reshape(n, CHUNK, DK_B)

    qs = (q3 * jnp.exp(b)).astype(bf16)
    kl = (k3 * jnp.exp(tot - b)).astype(bf16)
    kn = (k3 * jnp.exp(jnp.minimum(-b, 80.0))).astype(bf16)
    att = lax.dot_general(qs, kn, (((2,), (2,)), ((0,), (0,))), preferred_element_type=f32)
    att_ref[...] = jnp.where(keep[None], att, 0.0)

    b_ref[...] = b
    need_fix = jnp.min(tot) < FAST_DECAY_LIMIT

    @pl.when(need_fix)
    def _():
        lane = lax.broadcasted_iota(i32, (CHUNK, CHUNK), 1)

        def chunk_body(c, carry):
            bc = b_ref[c]
            row0 = pl.multiple_of(c * CHUNK, CHUNK)
            tot_c = bc[0:1] if reverse else bc[CHUNK - 1:CHUNK]

            @pl.when(jnp.min(tot_c) < FAST_DECAY_LIMIT)
            def _():
                qc = q_ref[pl.ds(row0, CHUNK), :] * scale

                def jbody(j, acc):
                    bj = b_ref[c, pl.ds(j, 1), :]
                    kj = k_ref[pl.ds(row0 + j, 1), :]
                    e = jnp.exp(jnp.minimum(bc - bj, 0.0))
                    col = jnp.sum(qc * e * kj, axis=-1, keepdims=True)
                    return acc + jnp.where(lane == j, col, 0.0)

                acc = lax.fori_loop(0, CHUNK, jbody, jnp.zeros((CHUNK, CHUNK), f32))
                att_ref[c] = jnp.where(keep, acc, 0.0)

            return carry

        lax.fori_loop(0, n, chunk_body, 0)

    u = lax.dot_general(v3b, kl, (((1,), (1,)), ((0,), (0,))), preferred_element_type=f32)
    dec = jnp.exp(tot)
    st = st0
    order = range(n - 1, -1, -1) if reverse else range(n)
    for c in order:
        sall_ref[c] = st
        st = st * dec[c] + u[c]
    inter = lax.dot_general(qs, sall_ref[...].astype(bf16), (((2,), (2,)), ((0,), (0,))),
                            preferred_element_type=f32)
    intra = lax.dot_general(att_ref[...].astype(bf16), v3b, (((2,), (1,)), ((0,), (0,))),
                            preferred_element_type=f32)
    return (inter + intra).reshape(tb, DV_B), st


def _gla_kernel(*refs, has_init):
    if has_init:
        (q_ref, k_ref, v_ref, bg_ref, lr_ref, wf_ref, wb_ref, bf_ref, bb_ref, gn_ref, sf0_ref, sb0_ref,
         o_ref, b_ref, att_ref, sall_ref) = refs
        stf0 = sf0_ref[...].T
        stb0 = sb0_ref[...].T
    else:
        (q_ref, k_ref, v_ref, bg_ref, lr_ref, wf_ref, wb_ref, bf_ref, bb_ref, gn_ref,
         o_ref, sf_ref, sb_ref, b_ref, att_ref, sall_ref) = refs
        stf0 = jnp.zeros((DV_B, DK_B), f32)
        stb0 = stf0
    tb = q_ref.shape[0]
    n = tb // CHUNK
    lr = lr_ref[...]
    v3b = v_ref[...].reshape(n, CHUNK, DV_B).astype(bf16)
    gf = _log_decay(lr, wf_ref, bf_ref)
    of, stf = _gla_direction(q_ref, k_ref, v3b, gf, stf0, False, b_ref, att_ref, sall_ref)
    gb = _log_decay(lr, wb_ref, bb_ref)
    ob, stb = _gla_direction(q_ref, k_ref, v3b, gb, stb0, True, b_ref, att_ref, sall_ref)
    o = of + ob
    o = o * lax.rsqrt(jnp.mean(o * o, axis=-1, keepdims=True) + EPS) * gn_ref[...]
    o_ref[...] = (o * _silu(bg_ref[...])).astype(bf16)
    if not has_init:
        sf_ref[...] = stf.T
        sb_ref[...] = stb.T


def _gla(bq, bk, bv, bg, lr, wf, wb, bfv, bbv, gn, tb, row_blk0, nb, states=None, layer=0):
    has_init = states is not None
    n = tb // CHUNK
    rows = lambda w: pl.BlockSpec((tb, w), lambda b, h: (row_blk0 + b, h))
    lrs = pl.BlockSpec((tb, LANES), lambda b, h: (row_blk0 + b, 0))
    wsp = pl.BlockSpec((LANES, DK_B), lambda b, h: (0, h))
    bsp = pl.BlockSpec((1, DK_B), lambda b, h: (0, h))
    gsp = pl.BlockSpec((1, DV_B), lambda b, h: (0, 0))
    in_specs = [rows(DK_B), rows(DK_B), rows(DV_B), rows(DV_B), lrs, wsp, wsp, bsp, bsp, gsp]
    args = [bq, bk, bv, bg, lr, wf, wb, bfv, bbv, gn]
    osp = pl.BlockSpec((tb, DV_B), lambda b, h: (b, h))
    oshape = jax.ShapeDtypeStruct((nb * tb, D), bf16)
    if has_init:
        ssp = pl.BlockSpec((None, None, None, DK_B, DV_B), lambda b, h: (b, layer, h, 0, 0))
        in_specs += [ssp, ssp]
        args += list(states)
        out_specs, out_shape = osp, oshape
    else:
        ssp = pl.BlockSpec((None, None, DK_B, DV_B), lambda b, h: (b, h, 0, 0))
        sshape = jax.ShapeDtypeStruct((nb, N_HEADS_B, DK_B, DV_B), f32)
        out_specs, out_shape = [osp, ssp, ssp], [oshape, sshape, sshape]
    return pl.pallas_call(
        functools.partial(_gla_kernel, has_init=has_init),
        grid=(nb, N_HEADS_B),
        in_specs=in_specs,
        out_specs=out_specs,
        out_shape=out_shape,
        scratch_shapes=[pltpu.VMEM((n, CHUNK, DK_B), f32), pltpu.VMEM((n, CHUNK, CHUNK), f32),
                        pltpu.VMEM((n, DV_B, DK_B), f32)],
        compiler_params=_cparams(("arbitrary", "arbitrary")),
        name="gla_lat" if has_init else "gla_ctx",
    )(*args)


def _merge_kernel(x_ref, oap_ref, oas_ref, ogp_ref, ogs_ref, ga_ref, gb_ref, mod_ref, wa_ref, wb_ref,
                  wo_ref, g_ref, rw_ref, rb_ref,
                  x1_ref, ht_ref, idx_ref, gate_ref, rank_ref, cnt_ref, base_ref):
    i = pl.program_id(0)

    @pl.when(i == 0)
    def _():
        base_ref[...] = jnp.zeros_like(base_ref)

    is_ctx = i < P_TILES
    oa = jnp.where(is_ctx, oap_ref[...], oas_ref[...])
    og = jnp.where(is_ctx, ogp_ref[...], ogs_ref[...])
    ya = _dot(oa, wa_ref[...])
    yb = _dot(og, wb_ref[...])
    merged = jax.nn.sigmoid(ga_ref[...]) * ya + jax.nn.sigmoid(gb_ref[...]) * yb
    mix = _dot(merged.astype(bf16), wo_ref[...])
    m = mod_ref[pl.ds(_mod_row(i), 1), :]
    x1 = x_ref[...] + m[:, 2 * D:3 * D] * mix
    x1_ref[...] = x1
    h = x1 * lax.rsqrt(jnp.mean(x1 * x1, axis=-1, keepdims=True) + EPS) * g_ref[...]
    h = h * (1.0 + m[:, 4 * D:5 * D]) + m[:, 3 * D:4 * D]
    for s in range(SUB):
        ht_ref[pl.ds(s, TM, stride=SUB), :] = h[:, s * LANES:(s + 1) * LANES]

    lg = _dot3_nt(rw_ref[...], h) + rb_ref[...][:, 0:1]
    eid = lax.broadcasted_iota(i32, (N_EXPERTS, TM), 0)
    vals, sels, idxs = [], [], []
    for _ in range(TOP_K):
        mx = jnp.max(lg, axis=0, keepdims=True)
        ik = jnp.min(jnp.where(lg == mx, eid, N_EXPERTS), axis=0, keepdims=True)
        sel = eid == ik
        vals.append(mx)
        idxs.append(ik)
        sels.append(sel)
        lg = jnp.where(sel, -jnp.inf, lg)
    es = [jnp.exp(v - vals[0]) for v in vals]
    den = es[0] + es[1] + es[2] + es[3]
    onehot = jnp.where(sels[0] | sels[1] | sels[2] | sels[3], 1.0, 0.0)
    tr = lax.broadcasted_iota(i32, (TM, TM), 0)
    tc = lax.broadcasted_iota(i32, (TM, TM), 1)
    before = jnp.where(tr < tc, 1.0, 0.0).astype(bf16)
    pos = base_ref[:, 0:1] + _dot(onehot.astype(bf16), before)
    zrow_i = jnp.zeros((8 - TOP_K, TM), i32)
    ranks = [jnp.sum(jnp.where(s, pos, 0.0), axis=0, keepdims=True).astype(i32) for s in sels]
    idx_ref[...] = jnp.concatenate(idxs + [zrow_i], axis=0)
    rank_ref[...] = jnp.concatenate(ranks + [zrow_i], axis=0)
    gate_ref[...] = jnp.concatenate([e / den for e in es] + [jnp.zeros((8 - TOP_K, TM), f32)], axis=0)
    base = base_ref[...] + jnp.sum(onehot, axis=1, keepdims=True)
    base_ref[...] = base
    cnt_ref[...] = base.astype(i32)


def _merge(x, oap, oas, ogp, ogs, ga, gb, mod, wa, wb, wo, g, rw, rb):
    tile = lambda w: pl.BlockSpec((TM, w), lambda i: (i, 0))
    ctx = pl.BlockSpec((TM, D), lambda i: (jnp.minimum(i, P_TILES - 1), 0))
    lat = pl.BlockSpec((TM, D), lambda i: (jnp.maximum(i - P_TILES, 0), 0))
    res = lambda shape: pl.BlockSpec(shape, lambda i: (0, 0), pipeline_mode=pl.Buffered(1))
    meta = pl.BlockSpec((8, TM), lambda i: (0, i))
    return pl.pallas_call(
        _merge_kernel,
        grid=(N_TILES,),
        in_specs=[tile(D), ctx, lat, ctx, lat, tile(D), tile(D), res((8, 6 * D)), res((D, D)), res((D, D)),
                  res((D, D)), res((1, D)), res((N_EXPERTS, D)), res((N_EXPERTS, LANES))],
        out_specs=[tile(D), pl.BlockSpec((TM * SUB, LANES), lambda i: (i, 0)), meta, meta, meta,
                   pl.BlockSpec((N_EXPERTS, LANES), lambda i: (0, 0))],
        out_shape=[jax.ShapeDtypeStruct((T, D), f32), jax.ShapeDtypeStruct((T * SUB, LANES), f32),
                   jax.ShapeDtypeStruct((8, T), i32), jax.ShapeDtypeStruct((8, T), f32),
                   jax.ShapeDtypeStruct((8, T), i32), jax.ShapeDtypeStruct((N_EXPERTS, LANES), i32)],
        scratch_shapes=[pltpu.VMEM((N_EXPERTS, LANES), f32)],
        compiler_params=_cparams(("arbitrary",)),
        name="merge_route",
    )(x, oap, oas, ogp, ogs, ga, gb, mod, wa, wb, wo, g, rw, rb)


def _experts_kernel(blk_e_ref, src_ref, srcn_ref, dst_ref, gate_ref, h_hbm, wgu_ref, bgu_ref, wdn_ref,
                    bdn_ref, yk_hbm, xg, yb, wgu_bf, wdn_bf, gsem, ssem):
    i = pl.program_id(0)
    n = pl.num_programs(0)
    slot = i % 2
    nslot = 1 - slot
    buf_rows = MOE_ROWS * SUB

    def gather(idx_ref, s):
        for r in range(MOE_ROWS):
            t = idx_ref[0, 0, r]
            pltpu.make_async_copy(h_hbm.at[pl.ds(pl.multiple_of(t * SUB, SUB), SUB)],
                                  xg.at[s, pl.ds(r * SUB, SUB)],
                                  gsem.at[s]).start()

    @pl.when(i == 0)
    def _():
        yb[1] = jnp.zeros((buf_rows, LANES), f32)
        fills = [pltpu.make_async_copy(yb.at[1], yk_hbm.at[pl.ds((DUMP_SLOT + p * MOE_ROWS) * SUB, buf_rows)],
                                       ssem.at[1]) for p in range(N_DUMP // MOE_ROWS)]
        for cp in fills:
            cp.start()
        for cp in fills:
            cp.wait()
        gather(src_ref, 0)

    @pl.when(i + 1 < n)
    def _():
        gather(srcn_ref, nslot)

    e = blk_e_ref[i]
    prev = blk_e_ref[jnp.maximum(i - 1, 0)]

    @pl.when((i == 0) | (e != prev))
    def _():
        wgu_bf[...] = wgu_ref[0].astype(bf16)
        wdn_bf[...] = wdn_ref[0].astype(bf16)

    pltpu.make_async_copy(h_hbm.at[pl.ds(0, buf_rows)], xg.at[slot], gsem.at[slot]).wait()

    @pl.when(i >= 2)
    def _():
        pltpu.make_async_copy(yb.at[slot], yk_hbm.at[pl.ds(0, buf_rows)], ssem.at[slot]).wait()

    xb = jnp.concatenate([xg[slot, pl.ds(s, MOE_ROWS, stride=SUB), :] for s in range(SUB)],
                         axis=1).astype(bf16)
    gu = _dot(xb, wgu_bf[...]) + bgu_ref[0]
    gt = jnp.minimum(gu[:, :MOE_FF], SWIGLU_LIMIT)
    up = jnp.clip(gu[:, MOE_FF:], -SWIGLU_LIMIT, SWIGLU_LIMIT)
    act = (up + 1.0) * gt * jax.nn.sigmoid(SWIGLU_ALPHA * gt)
    y = _dot(act.astype(bf16), wdn_bf[...]) + bdn_ref[0]
    ri = lax.broadcasted_iota(i32, (MOE_ROWS, MOE_ROWS), 0)
    ci = lax.broadcasted_iota(i32, (MOE_ROWS, MOE_ROWS), 1)
    gcol = jnp.sum(jnp.where(ri == ci, jnp.broadcast_to(gate_ref[0], (MOE_ROWS, MOE_ROWS)), 0.0),
                   axis=1, keepdims=True)
    y = y * gcol
    for s in range(SUB):
        yb[slot, pl.ds(s, MOE_ROWS, stride=SUB), :] = y[:, s * LANES:(s + 1) * LANES]
    for r in range(MOE_ROWS):
        d = dst_ref[0, 0, r]
        pltpu.make_async_copy(yb.at[slot, pl.ds(r * SUB, SUB)],
                              yk_hbm.at[pl.ds(pl.multiple_of(d * SUB, SUB), SUB)],
                              ssem.at[slot]).start()

    @pl.when(i == n - 1)
    def _():
        pltpu.make_async_copy(yb.at[slot], yk_hbm.at[pl.ds(0, buf_rows)], ssem.at[slot]).wait()
        pltpu.make_async_copy(yb.at[nslot], yk_hbm.at[pl.ds(0, buf_rows)], ssem.at[nslot]).wait()


def _experts(blk_e, row_src, row_dst, row_gate, ht, wgu, bgu, wdn, bdn):
    src3 = row_src.reshape(N_BLOCKS, 1, MOE_ROWS)
    dst3 = row_dst.reshape(N_BLOCKS, 1, MOE_ROWS)
    gate3 = row_gate.reshape(N_BLOCKS, 1, MOE_ROWS)
    smem = lambda f: pl.BlockSpec((1, 1, MOE_ROWS), f, memory_space=pltpu.SMEM)
    grid_spec = pltpu.PrefetchScalarGridSpec(
        num_scalar_prefetch=1,
        grid=(N_BLOCKS,),
        in_specs=[
            smem(lambda i, be: (i, 0, 0)),
            smem(lambda i, be: (jnp.minimum(i + 1, N_BLOCKS - 1), 0, 0)),
            smem(lambda i, be: (i, 0, 0)),
            pl.BlockSpec((1, 1, MOE_ROWS), lambda i, be: (i, 0, 0)),
            pl.BlockSpec(memory_space=pl.ANY),
            pl.BlockSpec((1, D, 2 * MOE_FF), lambda i, be: (be[i], 0, 0)),
            pl.BlockSpec((1, 1, 2 * MOE_FF), lambda i, be: (be[i], 0, 0)),
            pl.BlockSpec((1, MOE_FF, D), lambda i, be: (be[i], 0, 0)),
            pl.BlockSpec((1, 1, D), lambda i, be: (be[i], 0, 0)),
        ],
        out_specs=pl.BlockSpec(memory_space=pl.ANY),
        scratch_shapes=[
            pltpu.VMEM((2, MOE_ROWS * SUB, LANES), f32),
            pltpu.VMEM((2, MOE_ROWS * SUB, LANES), f32),
            pltpu.VMEM((D, 2 * MOE_FF), bf16),
            pltpu.VMEM((MOE_FF, D), bf16),
            pltpu.SemaphoreType.DMA((2,)),
            pltpu.SemaphoreType.DMA((2,)),
        ],
    )
    return pl.pallas_call(
        _experts_kernel,
        grid_spec=grid_spec,
        out_shape=jax.ShapeDtypeStruct(((TOP_K * T + N_DUMP) * SUB, LANES), f32),
        compiler_params=_cparams(("arbitrary",)),
        name="experts",
    )(blk_e, src3, src3, dst3, gate3, ht, wgu, bgu.reshape(N_EXPERTS, 1, 2 * MOE_FF), wdn,
      bdn.reshape(N_EXPERTS, 1, D))


def _combine_kernel(x1_ref, y0_ref, y1_ref, y2_ref, y3_ref, mod_ref, fg_ref, x2_ref, yn_ref, acc_ref):
    i = pl.program_id(0)
    acc_ref[...] = (y0_ref[...] + y1_ref[...]) + (y2_ref[...] + y3_ref[...])
    moe = jnp.concatenate([acc_ref[pl.ds(s, TM, stride=SUB), :] for s in range(SUB)], axis=1)
    m = mod_ref[pl.ds(_mod_row(i), 1), :]
    x2 = x1_ref[...] + m[:, 5 * D:6 * D] * moe
    x2_ref[...] = x2
    yn_ref[...] = x2 * lax.rsqrt(jnp.mean(x2 * x2, axis=-1, keepdims=True) + EPS) * fg_ref[...]


def _combine(x1, yk, mod, fg):
    tile = pl.BlockSpec((TM, D), lambda i: (i, 0))
    slot = lambda k: pl.BlockSpec((TM * SUB, LANES), lambda i: (k * N_TILES + i, 0))
    res = lambda shape: pl.BlockSpec(shape, lambda i: (0, 0), pipeline_mode=pl.Buffered(1))
    return pl.pallas_call(
        _combine_kernel,
        grid=(N_TILES,),
        in_specs=[tile, slot(0), slot(1), slot(2), slot(3), res((8, 6 * D)), res((1, D))],
        out_specs=[tile, tile],
        out_shape=[jax.ShapeDtypeStruct((T, D), f32), jax.ShapeDtypeStruct((T, D), f32)],
        scratch_shapes=[pltpu.VMEM((TM * SUB, LANES), f32)],
        compiler_params=_cparams(("arbitrary",)),
        name="combine",
    )(x1, yk, yk, yk, yk, mod, fg)


def _rope_tables():
    nf = HD_A // 4
    inv = ROPE_BASE ** (-jnp.arange(nf, dtype=f32) / nf)
    n = jnp.arange(DEC_SEQ)
    row = (n // GRID_W).astype(f32)[:, None] * inv
    col = (n % GRID_W).astype(f32)[:, None] * inv
    cos64 = jnp.concatenate([jnp.cos(row), jnp.cos(row), jnp.cos(col), jnp.cos(col)], axis=-1)
    sin64 = jnp.concatenate([-jnp.sin(row), jnp.sin(row), -jnp.sin(col), jnp.sin(col)], axis=-1)
    cos_s = jnp.tile(jnp.tile(cos64, (1, 2)), (DEC_BATCH, 1))
    sin_s = jnp.tile(jnp.tile(sin64, (1, 2)), (DEC_BATCH, 1))
    cos = jnp.concatenate([jnp.ones((T_P, LANES), f32), cos_s], axis=0)
    sin = jnp.concatenate([jnp.zeros((T_P, LANES), f32), sin_s], axis=0)
    return cos, sin


def _routing_tables(idx, rank, gate, counts):
    padded = (counts + MOE_ROWS - 1) // MOE_ROWS * MOE_ROWS
    pad_end = jnp.cumsum(padded)
    pad_start = pad_end - padded
    dest = (pad_start[idx] + rank).reshape(-1)
    inv = jnp.full((N_ROWS,), -1, i32).at[dest].set(jnp.arange(TOP_K * T, dtype=i32))
    valid = inv >= 0
    safe = jnp.maximum(inv, 0)
    row_src = jnp.where(valid, safe % T, 0).astype(i32)
    row_dst = jnp.where(valid, inv, DUMP_SLOT + jnp.arange(N_ROWS, dtype=i32) % N_DUMP).astype(i32)
    row_gate = jnp.where(valid, gate.reshape(-1)[safe], 0.0)
    blk_e = jnp.minimum(jnp.searchsorted(pad_end, jnp.arange(N_BLOCKS, dtype=i32) * MOE_ROWS, side='right'),
                        N_EXPERTS - 1).astype(i32)
    return blk_e, row_src, row_dst, row_gate


def kernel(x_prompt, x_sample, cache_k, cache_v, state_fwd, state_bwd, c, c_ctx, ada_w, ada_b, norm_mix_g,
           norm_ffn_g, w_in, diff_lambda, diff_subln_g, gla_gk_w, gla_gk_b, gla_norm_g, w_branch_a,
           w_branch_b, w_out, router_w, router_b, expert_w_gu, expert_b_gu, expert_w_down, expert_b_down,
           final_norm_g):
    x = jnp.concatenate([x_prompt.reshape(T_P, D), x_sample.reshape(T_S, D)], axis=0)
    cvec = jnp.concatenate([c_ctx[None], c, jnp.zeros((3, D), f32)], axis=0)
    mods = _ada_mod(cvec, ada_w, ada_b)
    cos, sin = _rope_tables()
    ck = cache_k.reshape(DEC_BATCH, DEPTH, PAST_LEN, D)
    cv = cache_v.reshape(DEC_BATCH, DEPTH, PAST_LEN, D)
    main_w = 6 * D
    lr0 = main_w
    g0 = main_w + 2 * GK_RANK

    ks, vs, sfs, sbs = [], [], [], []
    yn = None
    for l in range(DEPTH):
        lam_init = 0.8 - 0.6 * math.exp(-0.3 * l)
        mod = mods[l]
        wm = w_in[l, :, :main_w].astype(bf16)
        wl = jnp.pad(w_in[l, :, lr0:g0], ((0, 0), (0, LANES - 2 * GK_RANK))).astype(bf16)
        wg = w_in[l, :, g0:].astype(bf16)
        q, k, v, bq, bk, bv, bg, lr, ga, gb = _inproj(x, mod, norm_mix_g[l][None], wm, wg, wl, cos, sin)
        ks.append(k[:T_P].reshape(BATCH, SEQ, D))
        vs.append(v[:T_P].reshape(BATCH, SEQ, D))

        lq = jnp.pad(diff_lambda[l], ((0, 4), (0, LANES - HD_A)))
        sg = diff_subln_g[l][None]
        oap = _attn_ctx(q, k, v, lq, sg, lam_init)
        oas = _attn_lat(q, k, v, ck, cv, lq, sg, lam_init, l)

        wf = jnp.pad(gla_gk_w[l, 0], ((0, LANES - GK_RANK), (0, 0)))
        wb = jnp.pad(gla_gk_w[l, 1], ((GK_RANK, LANES - 2 * GK_RANK), (0, 0)))
        bfv = gla_gk_b[l, 0][None]
        bbv = gla_gk_b[l, 1][None]
        gn = gla_norm_g[l][None]
        ogp, sf, sb = _gla(bq, bk, bv, bg, lr, wf, wb, bfv, bbv, gn, SEQ, 0, BATCH)
        ogs = _gla(bq, bk, bv, bg, lr, wf, wb, bfv, bbv, gn, DEC_SEQ, T_P // DEC_SEQ, DEC_BATCH,
                   states=(state_fwd, state_bwd), layer=l)
        sfs.append(sf)
        sbs.append(sb)

        rb = jnp.broadcast_to(router_b[l][:, None], (N_EXPERTS, LANES))
        x1, ht, idx8, gate8, rank8, cnt = _merge(
            x, oap, oas, ogp, ogs, ga, gb, mod, w_branch_a[l].astype(bf16), w_branch_b[l].astype(bf16),
            w_out[l].astype(bf16), norm_ffn_g[l][None], router_w[l].T, rb)
        blk_e, row_src, row_dst, row_gate = _routing_tables(idx8[:TOP_K], rank8[:TOP_K], gate8[:TOP_K],
                                                            cnt[:, 0])
        yk = _experts(blk_e, row_src, row_dst, row_gate, ht, expert_w_gu[l], expert_b_gu[l],
                      expert_w_down[l], expert_b_down[l])
        x, yn = _combine(x1, yk, mod, final_norm_g[None])

    y_prompt = yn[:T_P].reshape(BATCH, SEQ, D)
    y_sample = yn[T_P:].reshape(DEC_BATCH, DEC_SEQ, D)
    new_k = jnp.stack(ks, axis=1).reshape(BATCH, DEPTH, SEQ, N_HEADS_A, 2, HD_A)
    new_v = jnp.stack(vs, axis=1).reshape(BATCH, DEPTH, SEQ, N_HEADS_A, 2 * HD_A)
    new_sf = jnp.stack(sfs, axis=1)
    new_sb = jnp.stack(sbs, axis=1)
    return (y_prompt, y_sample, new_k, new_v, new_sf, new_sb)
```

```python
import functools
import math

import jax
import jax.numpy as jnp
from jax import lax
from jax.experimental import pallas as pl
from jax.experimental.pallas import tpu as pltpu

f32 = jnp.float32
bf16 = jnp.bfloat16
i32 = jnp.int32

D = 1024
BATCH = 16
SEQ = 256
DEPTH = 2
DEC_BATCH = 4
DEC_SEQ = 1024
PAST_LEN = 512
GRID_W = 64
HD_A = 64
N_HEADS_A = 8
ROPE_BASE = 10000.0
N_HEADS_B = 4
DK_B = 128
DV_B = 256
GK_RANK = 16
GATE_NORM = 16.0
CHUNK = 64
N_EXPERTS = 32
TOP_K = 4
MOE_FF = 1024
SWIGLU_LIMIT = 7.0
SWIGLU_ALPHA = 1.702
EPS = 1e-6

T_P = BATCH * SEQ
T_S = DEC_BATCH * DEC_SEQ
T = T_P + T_S
TM = 256
N_TILES = T // TM
P_TILES = T_P // TM
S_TILES_PER_BATCH = DEC_SEQ // TM

MOE_ROWS = 128
N_BLOCKS = (T * TOP_K + N_EXPERTS * (MOE_ROWS - 1)) // MOE_ROWS + 1
N_ROWS = N_BLOCKS * MOE_ROWS
SUB = 8
LANES = 128
FAST_DECAY_LIMIT = -60.0

_VMEM_LIMIT = 56 * 1024 * 1024
_EXPERTS_VMEM_LIMIT = 60 * 1024 * 1024


def _cparams(sem, vmem=_VMEM_LIMIT):
    return pltpu.CompilerParams(dimension_semantics=sem, vmem_limit_bytes=vmem)


def _dot(a, b):
    return jnp.dot(a, b, preferred_element_type=f32)


def _dot_nt(a, b):
    return lax.dot_general(a, b, (((1,), (1,)), ((), ())), preferred_element_type=f32)


def _split(x):
    hi = x.astype(bf16)
    lo = (x - hi.astype(f32)).astype(bf16)
    return hi, lo


def _dot3(a, b):
    ah, al = _split(a)
    bh, bl = _split(b)
    return _dot(ah, bh) + _dot(ah, bl) + _dot(al, bh)


def _dot3_nt(a, b):
    ah, al = _split(a)
    bh, bl = _split(b)
    return _dot_nt(ah, bh) + _dot_nt(ah, bl) + _dot_nt(al, bh)


def _silu(x):
    return x * jax.nn.sigmoid(x)


def _mod_row(i):
    return jnp.where(i < P_TILES, 0, 1 + lax.shift_right_logical(jnp.maximum(i - P_TILES, 0), 2))


def _ada_kernel(c_ref, w_ref, b_ref, o_ref):
    o_ref[0] = _dot3(_silu(c_ref[...]), w_ref[0]) + b_ref[0]


def _ada_mod(cvec, ada_w, ada_b):
    tn = 1536
    return pl.pallas_call(
        _ada_kernel,
        grid=(DEPTH, 6 * D // tn),
        in_specs=[
            pl.BlockSpec((8, D), lambda l, j: (0, 0)),
            pl.BlockSpec((1, D, tn), lambda l, j: (l, 0, j)),
            pl.BlockSpec((1, 1, tn), lambda l, j: (l, 0, j)),
        ],
        out_specs=pl.BlockSpec((1, 8, tn), lambda l, j: (l, 0, j)),
        out_shape=jax.ShapeDtypeStruct((DEPTH, 8, 6 * D), f32),
        compiler_params=_cparams(("arbitrary", "arbitrary")),
        name="ada_mod",
    )(cvec, ada_w, ada_b.reshape(DEPTH, 1, 6 * D))


def _inproj_kernel(x_ref, mod_ref, g_ref, wm_ref, wg_ref, wl_ref, cos_ref, sin_ref,
                   q_ref, k_ref, v_ref, bq_ref, bk_ref, bv_ref, bg_ref, lr_ref, ga_ref, gb_ref):
    i = pl.program_id(0)
    m = mod_ref[pl.ds(_mod_row(i), 1), :]
    shift = m[:, 0:D]
    scale = m[:, D:2 * D]
    x = x_ref[...]
    h = x * lax.rsqrt(jnp.mean(x * x, axis=-1, keepdims=True) + EPS) * g_ref[...]
    hb = (h * (1.0 + scale) + shift).astype(bf16)

    cos = cos_ref[...]
    sin = sin_ref[...]
    lane = lax.broadcasted_iota(i32, (TM, LANES), 1)
    lo_mask = (lane & 31) < 16

    def rope(a):
        partner = jnp.where(lo_mask, pltpu.roll(a, LANES - 16, 1), pltpu.roll(a, 16, 1))
        return a * cos + partner * sin

    aq = _dot(hb, wm_ref[:, 0:D])
    for c in range(D // LANES):
        q_ref[:, c * LANES:(c + 1) * LANES] = rope(aq[:, c * LANES:(c + 1) * LANES]).astype(bf16)
    ak = _dot(hb, wm_ref[:, D:2 * D])
    for c in range(D // LANES):
        k_ref[:, c * LANES:(c + 1) * LANES] = rope(ak[:, c * LANES:(c + 1) * LANES])
    v_ref[...] = _dot(hb, wm_ref[:, 2 * D:3 * D])
    bq_ref[...] = _dot(hb, wm_ref[:, 3 * D:3 * D + 512])
    bk_ref[...] = _dot(hb, wm_ref[:, 3 * D + 512:4 * D])
    bv_ref[...] = _dot(hb, wm_ref[:, 4 * D:5 * D])
    bg_ref[...] = _dot(hb, wm_ref[:, 5 * D:6 * D])
    lr_ref[...] = _dot(hb, wl_ref[...])
    ga_ref[...] = _dot(hb, wg_ref[:, 0:D])
    gb_ref[...] = _dot(hb, wg_ref[:, D:2 * D])


def _inproj(x, mod, g, wm, wg, wl, cos, sin):
    tile = lambda w: pl.BlockSpec((TM, w), lambda i: (i, 0))
    res = lambda shape: pl.BlockSpec(shape, lambda i: (0, 0), pipeline_mode=pl.Buffered(1))
    widths = (D, D, D, 512, 512, D, D, LANES, D, D)
    dtypes = (bf16,) + (f32,) * 9
    return pl.pallas_call(
        _inproj_kernel,
        grid=(N_TILES,),
        in_specs=[tile(D), res((8, 6 * D)), res((1, D)), res((D, 6 * D)), res((D, 2 * D)),
                  res((D, LANES)), tile(LANES), tile(LANES)],
        out_specs=[tile(w) for w in widths],
        out_shape=[jax.ShapeDtypeStruct((T, w), dt) for w, dt in zip(widths, dtypes)],
        compiler_params=_cparams(("arbitrary",)),
        name="inproj",
    )(x, mod, g, wm, wg, wl, cos, sin)


def _lam_from(lq_ref, lam_init):
    lq = lq_ref[...]
    s01 = jnp.sum(lq[0:1] * lq[1:2], axis=-1, keepdims=True)
    s23 = jnp.sum(lq[2:3] * lq[3:4], axis=-1, keepdims=True)
    return jnp.exp(s01) - jnp.exp(s23) + lam_init


def _subln(o, g, lam_init):
    o = o * lax.rsqrt(jnp.mean(o * o, axis=-1, keepdims=True) + EPS)
    return o * g * (1.0 - lam_init)


def _attn_ctx_kernel(q_ref, k_ref, v_ref, lq_ref, g_ref, o_ref, *, lam_init):
    q = q_ref[...]
    kb = k_ref[...].astype(bf16)
    vb = v_ref[...].astype(bf16)
    lo = lax.broadcasted_iota(i32, q.shape, 1) < HD_A
    outs = []
    for qm in (jnp.where(lo, q, 0), jnp.where(lo, 0, q)):
        s = _dot_nt(qm, kb) * (HD_A ** -0.5)
        p = jnp.exp(s - jnp.max(s, axis=-1, keepdims=True))
        l = jnp.sum(p, axis=-1, keepdims=True)
        outs.append(_dot(p.astype(bf16), vb) / l)
    o = outs[0] - _lam_from(lq_ref, lam_init) * outs[1]
    o_ref[...] = _subln(o, g_ref[...], lam_init).astype(bf16)


def _attn_lat_kernel(q_ref, k_ref, v_ref, ck_ref, cv_ref, lq_ref, g_ref, o_ref, *, lam_init):
    q = q_ref[...]
    kb = k_ref[...].astype(bf16)
    vb = v_ref[...].astype(bf16)
    ckb = ck_ref[...].astype(bf16)
    cvb = cv_ref[...].astype(bf16)
    lo = lax.broadcasted_iota(i32, q.shape, 1) < HD_A
    outs = []
    for qm in (jnp.where(lo, q, 0), jnp.where(lo, 0, q)):
        sc = _dot_nt(qm, ckb) * (HD_A ** -0.5)
        ss = _dot_nt(qm, kb) * (HD_A ** -0.5)
        mx = jnp.maximum(jnp.max(sc, axis=-1, keepdims=True), jnp.max(ss, axis=-1, keepdims=True))
        pc = jnp.exp(sc - mx)
        ps = jnp.exp(ss - mx)
        l = jnp.sum(pc, axis=-1, keepdims=True) + jnp.sum(ps, axis=-1, keepdims=True)
        outs.append((_dot(pc.astype(bf16), cvb) + _dot(ps.astype(bf16), vb)) / l)
    o = outs[0] - _lam_from(lq_ref, lam_init) * outs[1]
    o_ref[...] = _subln(o, g_ref[...], lam_init).astype(bf16)


def _attn_ctx(q, k, v, lq, g, lam_init):
    blk = pl.BlockSpec((SEQ, LANES), lambda b, h: (b, h))
    small = lambda s: pl.BlockSpec(s, lambda b, h: (0, 0))
    return pl.pallas_call(
        functools.partial(_attn_ctx_kernel, lam_init=lam_init),
        grid=(BATCH, N_HEADS_A),
        in_specs=[blk, blk, blk, small((8, LANES)), small((1, LANES))],
        out_specs=blk,
        out_shape=jax.ShapeDtypeStruct((T_P, D), bf16),
        compiler_params=_cparams(("arbitrary", "arbitrary")),
        name="attn_ctx",
    )(q, k, v, lq, g)


def _attn_lat(q, k, v, ck, cv, lq, g, lam_init, layer):
    nq = DEC_SEQ // TM
    qblk = pl.BlockSpec((TM, LANES), lambda b, h, t: (P_TILES + b * nq + t, h))
    kblk = pl.BlockSpec((DEC_SEQ, LANES), lambda b, h, t: (T_P // DEC_SEQ + b, h))
    cblk = pl.BlockSpec((None, None, PAST_LEN, LANES), lambda b, h, t: (b, layer, 0, h))
    small = lambda s: pl.BlockSpec(s, lambda b, h, t: (0, 0))
    return pl.pallas_call(
        functools.partial(_attn_lat_kernel, lam_init=lam_init),
        grid=(DEC_BATCH, N_HEADS_A, nq),
        in_specs=[qblk, kblk, kblk, cblk, cblk, small((8, LANES)), small((1, LANES))],
        out_specs=pl.BlockSpec((TM, LANES), lambda b, h, t: (b * nq + t, h)),
        out_shape=jax.ShapeDtypeStruct((T_S, D), bf16),
        compiler_params=_cparams(("arbitrary", "arbitrary", "arbitrary")),
        name="attn_lat",
    )(q, k, v, ck, cv, lq, g)


def _log_decay(lr, w_ref, b_ref):
    z = _dot3(lr, w_ref[...]) + b_ref[...]
    return -(jnp.maximum(-z, 0.0) + jnp.log(1.0 + jnp.exp(-jnp.abs(z)))) / GATE_NORM


def _gla_direction(q_ref, k_ref, v3b, g, st0, reverse, b_ref, att_ref, sall_ref):
    tb = g.shape[0]
    n = tb // CHUNK
    scale = DK_B ** -0.5
    ri = lax.broadcasted_iota(i32, (CHUNK, CHUNK), 0)
    ci = lax.broadcasted_iota(i32, (CHUNK, CHUNK), 1)
    keep = (ci >= ri) if reverse else (ci <= ri)
    tri = jnp.broadcast_to(jnp.where(keep, 1.0, 0.0).astype(bf16)[None], (n, CHUNK, CHUNK))

    g3 = g.reshape(n, CHUNK, DK_B)
    ghi, glo = _split(g3)
    bdot = lambda a, b: lax.dot_general(a, b, (((2,), (1,)), ((0,), (0,))), preferred_element_type=f32)
    b = bdot(tri, ghi) + bdot(tri, glo)
    tot = b[:, 0:1, :] if reverse else b[:, CHUNK - 1:CHUNK, :]
    q3 = q_ref[...].reshape(n, CHUNK, DK_B) * scale
    k3 = k_ref[...].reshape(n, CHUNK, DK_B)

    qs = (q3 * jnp.exp(b)).astype(bf16)
    kl = (k3 * jnp.exp(tot - b)).astype(bf16)
    kn = (k3 * jnp.exp(jnp.minimum(-b, 80.0))).astype(bf16)
    att = lax.dot_general(qs, kn, (((2,), (2,)), ((0,), (0,))), preferred_element_type=f32)
    att_ref[...] = jnp.where(keep[None], att, 0.0)

    b_ref[...] = b
    need_fix = jnp.min(tot) < FAST_DECAY_LIMIT

    @pl.when(need_fix)
    def _():
        lane = lax.broadcasted_iota(i32, (CHUNK, CHUNK), 1)

        def chunk_body(c, carry):
            bc = b_ref[c]
            row0 = pl.multiple_of(c * CHUNK, CHUNK)
            tot_c = bc[0:1] if reverse else bc[CHUNK - 1:CHUNK]

            @pl.when(jnp.min(tot_c) < FAST_DECAY_LIMIT)
            def _():
                qc = q_ref[pl.ds(row0, CHUNK), :] * scale

                def jbody(j, acc):
                    bj = b_ref[c, pl.ds(j, 1), :]
                    kj = k_ref[pl.ds(row0 + j, 1), :]
                    e = jnp.exp(jnp.minimum(bc - bj, 0.0))
                    col = jnp.sum(qc * e * kj, axis=-1, keepdims=True)
                    return acc + jnp.where(lane == j, col, 0.0)

                acc = lax.fori_loop(0, CHUNK, jbody, jnp.zeros((CHUNK, CHUNK), f32))
                att_ref[c] = jnp.where(keep, acc, 0.0)

            return carry

        lax.fori_loop(0, n, chunk_body, 0)

    u = lax.dot_general(v3b, kl, (((1,), (1,)), ((0,), (0,))), preferred_element_type=f32)
    dec = jnp.exp(tot)
    st = st0
    order = range(n - 1, -1, -1) if reverse else range(n)
    for c in order:
        sall_ref[c] = st
        st = st * dec[c] + u[c]
    inter = lax.dot_general(qs, sall_ref[...].astype(bf16), (((2,), (2,)), ((0,), (0,))),
                            preferred_element_type=f32)
    intra = lax.dot_general(att_ref[...].astype(bf16), v3b, (((2,), (1,)), ((0,), (0,))),
                            preferred_element_type=f32)
    return (inter + intra).reshape(tb, DV_B), st


def _gla_kernel(*refs, has_init):
    if has_init:
        (q_ref, k_ref, v_ref, bg_ref, lr_ref, wf_ref, wb_ref, bf_ref, bb_ref, gn_ref, sf0_ref, sb0_ref,
         o_ref, b_ref, att_ref, sall_ref) = refs
        stf0 = sf0_ref[...].T
        stb0 = sb0_ref[...].T
    else:
        (q_ref, k_ref, v_ref, bg_ref, lr_ref, wf_ref, wb_ref, bf_ref, bb_ref, gn_ref,
         o_ref, sf_ref, sb_ref, b_ref, att_ref, sall_ref) = refs
        stf0 = jnp.zeros((DV_B, DK_B), f32)
        stb0 = stf0
    tb = q_ref.shape[0]
    n = tb // CHUNK
    lr = lr_ref[...]
    v3b = v_ref[...].reshape(n, CHUNK, DV_B).astype(bf16)
    gf = _log_decay(lr, wf_ref, bf_ref)
    of, stf = _gla_direction(q_ref, k_ref, v3b, gf, stf0, False, b_ref, att_ref, sall_ref)
    gb = _log_decay(lr, wb_ref, bb_ref)
    ob, stb = _gla_direction(q_ref, k_ref, v3b, gb, stb0, True, b_ref, att_ref, sall_ref)
    o = of + ob
    o = o * lax.rsqrt(jnp.mean(o * o, axis=-1, keepdims=True) + EPS) * gn_ref[...]
    o_ref[...] = (o * _silu(bg_ref[...])).astype(bf16)
    if not has_init:
        sf_ref[...] = stf.T
        sb_ref[...] = stb.T


def _gla(bq, bk, bv, bg, lr, wf, wb, bfv, bbv, gn, tb, row_blk0, nb, states=None, layer=0):
    has_init = states is not None
    n = tb // CHUNK
    rows = lambda w: pl.BlockSpec((tb, w), lambda b, h: (row_blk0 + b, h))
    lrs = pl.BlockSpec((tb, LANES), lambda b, h: (row_blk0 + b, 0))
    wsp = pl.BlockSpec((LANES, DK_B), lambda b, h: (0, h))
    bsp = pl.BlockSpec((1, DK_B), lambda b, h: (0, h))
    gsp = pl.BlockSpec((1, DV_B), lambda b, h: (0, 0))
    in_specs = [rows(DK_B), rows(DK_B), rows(DV_B), rows(DV_B), lrs, wsp, wsp, bsp, bsp, gsp]
    args = [bq, bk, bv, bg, lr, wf, wb, bfv, bbv, gn]
    osp = pl.BlockSpec((tb, DV_B), lambda b, h: (b, h))
    oshape = jax.ShapeDtypeStruct((nb * tb, D), bf16)
    if has_init:
        ssp = pl.BlockSpec((None, None, None, DK_B, DV_B), lambda b, h: (b, layer, h, 0, 0))
        in_specs += [ssp, ssp]
        args += list(states)
        out_specs, out_shape = osp, oshape
    else:
        ssp = pl.BlockSpec((None, None, DK_B, DV_B), lambda b, h: (b, h, 0, 0))
        sshape = jax.ShapeDtypeStruct((nb, N_HEADS_B, DK_B, DV_B), f32)
        out_specs, out_shape = [osp, ssp, ssp], [oshape, sshape, sshape]
    return pl.pallas_call(
        functools.partial(_gla_kernel, has_init=has_init),
        grid=(nb, N_HEADS_B),
        in_specs=in_specs,
        out_specs=out_specs,
        out_shape=out_shape,
        scratch_shapes=[pltpu.VMEM((n, CHUNK, DK_B), f32), pltpu.VMEM((n, CHUNK, CHUNK), f32),
                        pltpu.VMEM((n, DV_B, DK_B), f32)],
        compiler_params=_cparams(("arbitrary", "arbitrary")),
        name="gla_lat" if has_init else "gla_ctx",
    )(*args)


def _merge_kernel(x_ref, oap_ref, oas_ref, ogp_ref, ogs_ref, ga_ref, gb_ref, mod_ref, wa_ref, wb_ref,
                  wo_ref, g_ref, rw_ref, rb_ref,
                  x1_ref, ht_ref, idx_ref, gate_ref, rank_ref, cnt_ref, base_ref):
    i = pl.program_id(0)

    @pl.when(i == 0)
    def _():
        base_ref[...] = jnp.zeros_like(base_ref)

    is_ctx = i < P_TILES
    oa = jnp.where(is_ctx, oap_ref[...], oas_ref[...])
    og = jnp.where(is_ctx, ogp_ref[...], ogs_ref[...])
    ya = _dot(oa, wa_ref[...])
    yb = _dot(og, wb_ref[...])
    merged = jax.nn.sigmoid(ga_ref[...]) * ya + jax.nn.sigmoid(gb_ref[...]) * yb
    mix = _dot(merged.astype(bf16), wo_ref[...])
    m = mod_ref[pl.ds(_mod_row(i), 1), :]
    x1 = x_ref[...] + m[:, 2 * D:3 * D] * mix
    x1_ref[...] = x1
    h = x1 * lax.rsqrt(jnp.mean(x1 * x1, axis=-1, keepdims=True) + EPS) * g_ref[...]
    h = h * (1.0 + m[:, 4 * D:5 * D]) + m[:, 3 * D:4 * D]
    for s in range(SUB):
        ht_ref[pl.ds(s, TM, stride=SUB), :] = h[:, s * LANES:(s + 1) * LANES]

    lg = _dot3_nt(rw_ref[...], h) + rb_ref[...][:, 0:1]
    eid = lax.broadcasted_iota(i32, (N_EXPERTS, TM), 0)
    vals, sels, idxs = [], [], []
    for _ in range(TOP_K):
        mx = jnp.max(lg, axis=0, keepdims=True)
        ik = jnp.min(jnp.where(lg == mx, eid, N_EXPERTS), axis=0, keepdims=True)
        sel = eid == ik
        vals.append(mx)
        idxs.append(ik)
        sels.append(sel)
        lg = jnp.where(sel, -jnp.inf, lg)
    es = [jnp.exp(v - vals[0]) for v in vals]
    den = es[0] + es[1] + es[2] + es[3]
    onehot = jnp.where(sels[0] | sels[1] | sels[2] | sels[3], 1.0, 0.0)
    tr = lax.broadcasted_iota(i32, (TM, TM), 0)
    tc = lax.broadcasted_iota(i32, (TM, TM), 1)
    before = jnp.where(tr < tc, 1.0, 0.0).astype(bf16)
    pos = base_ref[:, 0:1] + _dot(onehot.astype(bf16), before)
    zrow_i = jnp.zeros((8 - TOP_K, TM), i32)
    ranks = [jnp.sum(jnp.where(s, pos, 0.0), axis=0, keepdims=True).astype(i32) for s in sels]
    idx_ref[...] = jnp.concatenate(idxs + [zrow_i], axis=0)
    rank_ref[...] = jnp.concatenate(ranks + [zrow_i], axis=0)
    gate_ref[...] = jnp.concatenate([e / den for e in es] + [jnp.zeros((8 - TOP_K, TM), f32)], axis=0)
    base = base_ref[...] + jnp.sum(onehot, axis=1, keepdims=True)
    base_ref[...] = base
    cnt_ref[...] = base.astype(i32)


def _merge(x, oap, oas, ogp, ogs, ga, gb, mod, wa, wb, wo, g, rw, rb):
    tile = lambda w: pl.BlockSpec((TM, w), lambda i: (i, 0))
    ctx = pl.BlockSpec((TM, D), lambda i: (jnp.minimum(i, P_TILES - 1), 0))
    lat = pl.BlockSpec((TM, D), lambda i: (jnp.maximum(i - P_TILES, 0), 0))
    res = lambda shape: pl.BlockSpec(shape, lambda i: (0, 0), pipeline_mode=pl.Buffered(1))
    meta = pl.BlockSpec((8, TM), lambda i: (0, i))
    return pl.pallas_call(
        _merge_kernel,
        grid=(N_TILES,),
        in_specs=[tile(D), ctx, lat, ctx, lat, tile(D), tile(D), res((8, 6 * D)), res((D, D)), res((D, D)),
                  res((D, D)), res((1, D)), res((N_EXPERTS, D)), res((N_EXPERTS, LANES))],
        out_specs=[tile(D), pl.BlockSpec((TM * SUB, LANES), lambda i: (i, 0)), meta, meta, meta,
                   pl.BlockSpec((N_EXPERTS, LANES), lambda i: (0, 0))],
        out_shape=[jax.ShapeDtypeStruct((T, D), f32), jax.ShapeDtypeStruct((T * SUB, LANES), f32),
                   jax.ShapeDtypeStruct((8, T), i32), jax.ShapeDtypeStruct((8, T), f32),
                   jax.ShapeDtypeStruct((8, T), i32), jax.ShapeDtypeStruct((N_EXPERTS, LANES), i32)],
        scratch_shapes=[pltpu.VMEM((N_EXPERTS, LANES), f32)],
        compiler_params=_cparams(("arbitrary",)),
        name="merge_route",
    )(x, oap, oas, ogp, ogs, ga, gb, mod, wa, wb, wo, g, rw, rb)


def _experts_kernel(blk_e_ref, nxt_e_ref, src_ref, srcn_ref, meta_ref, h_hbm, wgu_hbm, bgu_ref, wdn_hbm,
                    bdn_ref, out_hbm, acc, wgu_st, wdn_st, wgu_bf, wdn_bf, xg, yb, gsem, wsem, osem, *,
                    layer):
    i = pl.program_id(0)
    n = pl.num_programs(0)
    slot = i % 2
    buf_rows = MOE_ROWS * SUB

    def gather(idx_ref, s):
        for r in range(MOE_ROWS):
            t = jnp.minimum(idx_ref[0, 0, r], T - 1)
            pltpu.make_async_copy(h_hbm.at[pl.ds(pl.multiple_of(t * SUB, SUB), SUB)],
                                  xg.at[s, pl.ds(r * SUB, SUB)], gsem.at[s]).start()

    def weight_copies(e):
        return (pltpu.make_async_copy(wgu_hbm.at[layer, e], wgu_st, wsem.at[0]),
                pltpu.make_async_copy(wdn_hbm.at[layer, e], wdn_st, wsem.at[1]))

    e = blk_e_ref[i]

    @pl.when(i == 0)
    def _():
        for cp in weight_copies(e):
            cp.start()
        gather(src_ref, 0)
        acc[...] = jnp.zeros_like(acc)

    @pl.when(i + 1 < n)
    def _():
        gather(srcn_ref, 1 - slot)

    @pl.when((i == 0) | (e != blk_e_ref[jnp.maximum(i - 1, 0)]))
    def _():
        for cp in weight_copies(e):
            cp.wait()
        wgu_bf[...] = wgu_st[...].astype(bf16)
        wdn_bf[...] = wdn_st[...].astype(bf16)
        en = nxt_e_ref[e]

        @pl.when(en >= 0)
        def _():
            for cp in weight_copies(en):
                cp.start()

    pltpu.make_async_copy(h_hbm.at[pl.ds(0, buf_rows)], xg.at[slot], gsem.at[slot]).wait()

    xb = jnp.concatenate([xg[slot, pl.ds(s, MOE_ROWS, stride=SUB), :] for s in range(SUB)],
                         axis=1).astype(bf16)
    gu = _dot(xb, wgu_bf[...]) + bgu_ref[...]
    gt = jnp.minimum(gu[:, :MOE_FF], SWIGLU_LIMIT)
    up = jnp.clip(gu[:, MOE_FF:], -SWIGLU_LIMIT, SWIGLU_LIMIT)
    act = (up + 1.0) * gt * jax.nn.sigmoid(SWIGLU_ALPHA * gt)
    y = _dot(act.astype(bf16), wdn_bf[...]) + bdn_ref[...]
    ri = lax.broadcasted_iota(i32, (MOE_ROWS, MOE_ROWS), 0)
    ci = lax.broadcasted_iota(i32, (MOE_ROWS, MOE_ROWS), 1)
    diag = ri == ci
    col = lambda v: jnp.sum(jnp.where(diag, jnp.broadcast_to(v, (MOE_ROWS, MOE_ROWS)), 0.0),
                            axis=1, keepdims=True)
    y = jnp.where(col(meta_ref[0, 1:2, :]) > 0.5, y * col(meta_ref[0, 0:1, :]), 0.0)
    for s in range(SUB):
        yb[pl.ds(s, MOE_ROWS, stride=SUB), :] = y[:, s * LANES:(s + 1) * LANES]
    group = 16
    for r0 in range(0, MOE_ROWS, group):
        offs = [pl.multiple_of(src_ref[0, 0, r] * SUB, SUB) for r in range(r0, r0 + group)]
        olds = [acc[pl.ds(o, SUB), :] for o in offs]
        for j, o in enumerate(offs):
            acc[pl.ds(o, SUB), :] = olds[j] + yb[(r0 + j) * SUB:(r0 + j + 1) * SUB, :]

    @pl.when(i == n - 1)
    def _():
        cp = pltpu.make_async_copy(acc.at[pl.ds(0, T * SUB)], out_hbm, osem.at[0])
        cp.start()
        cp.wait()


def _experts(blk_e, nxt_e, row_src, meta, ht, wgu, bgu, wdn, bdn, layer):
    src3 = row_src.reshape(N_BLOCKS, 1, MOE_ROWS)
    smem = lambda f: pl.BlockSpec((1, 1, MOE_ROWS), f, memory_space=pltpu.SMEM)
    grid_spec = pltpu.PrefetchScalarGridSpec(
        num_scalar_prefetch=2,
        grid=(N_BLOCKS,),
        in_specs=[
            smem(lambda i, be, ne: (i, 0, 0)),
            smem(lambda i, be, ne: (jnp.minimum(i + 1, N_BLOCKS - 1), 0, 0)),
            pl.BlockSpec((1, 8, MOE_ROWS), lambda i, be, ne: (i, 0, 0)),
            pl.BlockSpec(memory_space=pl.ANY),
            pl.BlockSpec(memory_space=pl.ANY),
            pl.BlockSpec((None, None, 1, 2 * MOE_FF), lambda i, be, ne: (layer, be[i], 0, 0)),
            pl.BlockSpec(memory_space=pl.ANY),
            pl.BlockSpec((None, None, 1, D), lambda i, be, ne: (layer, be[i], 0, 0)),
        ],
        out_specs=pl.BlockSpec(memory_space=pl.ANY),
        scratch_shapes=[
            pltpu.VMEM(((T + 8) * SUB, LANES), f32),
            pltpu.VMEM((D, 2 * MOE_FF), f32),
            pltpu.VMEM((MOE_FF, D), f32),
            pltpu.VMEM((D, 2 * MOE_FF), bf16),
            pltpu.VMEM((MOE_FF, D), bf16),
            pltpu.VMEM((2, MOE_ROWS * SUB, LANES), f32),
            pltpu.VMEM((MOE_ROWS * SUB, LANES), f32),
            pltpu.SemaphoreType.DMA((2,)),
            pltpu.SemaphoreType.DMA((2,)),
            pltpu.SemaphoreType.DMA((1,)),
        ],
    )
    return pl.pallas_call(
        functools.partial(_experts_kernel, layer=layer),
        grid_spec=grid_spec,
        out_shape=jax.ShapeDtypeStruct((T * SUB, LANES), f32),
        compiler_params=_cparams(("arbitrary",), _EXPERTS_VMEM_LIMIT),
        name="experts",
    )(blk_e, nxt_e, src3, src3, meta, ht, wgu, bgu.reshape(DEPTH, N_EXPERTS, 1, 2 * MOE_FF), wdn,
      bdn.reshape(DEPTH, N_EXPERTS, 1, D))


def _combine_kernel(x1_ref, moe_ref, mod_ref, fg_ref, x2_ref, yn_ref):
    i = pl.program_id(0)
    moe = jnp.concatenate([moe_ref[pl.ds(s, TM, stride=SUB), :] for s in range(SUB)], axis=1)
    m = mod_ref[pl.ds(_mod_row(i), 1), :]
    x2 = x1_ref[...] + m[:, 5 * D:6 * D] * moe
    x2_ref[...] = x2
    yn_ref[...] = x2 * lax.rsqrt(jnp.mean(x2 * x2, axis=-1, keepdims=True) + EPS) * fg_ref[...]


def _combine(x1, moe, mod, fg):
    tile = pl.BlockSpec((TM, D), lambda i: (i, 0))
    res = lambda shape: pl.BlockSpec(shape, lambda i: (0, 0), pipeline_mode=pl.Buffered(1))
    return pl.pallas_call(
        _combine_kernel,
        grid=(N_TILES,),
        in_specs=[tile, pl.BlockSpec((TM * SUB, LANES), lambda i: (i, 0)), res((8, 6 * D)), res((1, D))],
        out_specs=[tile, tile],
        out_shape=[jax.ShapeDtypeStruct((T, D), f32), jax.ShapeDtypeStruct((T, D), f32)],
        compiler_params=_cparams(("arbitrary",)),
        name="combine",
    )(x1, moe, mod, fg)


def _rope_tables():
    nf = HD_A // 4
    inv = ROPE_BASE ** (-jnp.arange(nf, dtype=f32) / nf)
    n = jnp.arange(DEC_SEQ)
    row = (n // GRID_W).astype(f32)[:, None] * inv
    col = (n % GRID_W).astype(f32)[:, None] * inv
    cos64 = jnp.concatenate([jnp.cos(row), jnp.cos(row), jnp.cos(col), jnp.cos(col)], axis=-1)
    sin64 = jnp.concatenate([-jnp.sin(row), jnp.sin(row), -jnp.sin(col), jnp.sin(col)], axis=-1)
    cos_s = jnp.tile(jnp.tile(cos64, (1, 2)), (DEC_BATCH, 1))
    sin_s = jnp.tile(jnp.tile(sin64, (1, 2)), (DEC_BATCH, 1))
    cos = jnp.concatenate([jnp.ones((T_P, LANES), f32), cos_s], axis=0)
    sin = jnp.concatenate([jnp.zeros((T_P, LANES), f32), sin_s], axis=0)
    return cos, sin


def _routing_tables(idx, rank, gate, counts):
    padded = (counts + MOE_ROWS - 1) // MOE_ROWS * MOE_ROWS
    pad_end = jnp.cumsum(padded)
    pad_start = pad_end - padded
    eids = jnp.arange(N_EXPERTS, dtype=i32)
    start_of = jnp.sum(jnp.where(idx[..., None] == eids, pad_start, 0), axis=-1)
    dest = (start_of + rank).reshape(-1)
    inv = jnp.full((N_ROWS,), -1, i32).at[dest].set(jnp.arange(TOP_K * T, dtype=i32))
    valid = inv >= 0
    safe = jnp.maximum(inv, 0)
    row_src = jnp.where(valid, safe % T, T).astype(i32)
    row_gate = jnp.where(valid, gate.reshape(-1)[safe], 0.0)
    meta = jnp.zeros((N_BLOCKS, 8, MOE_ROWS), f32)
    meta = meta.at[:, 0].set(row_gate.reshape(N_BLOCKS, MOE_ROWS))
    meta = meta.at[:, 1].set(valid.astype(f32).reshape(N_BLOCKS, MOE_ROWS))
    starts = jnp.arange(N_BLOCKS, dtype=i32) * MOE_ROWS
    blk_e = jnp.sum((pad_end[None, :] <= starts[:, None]).astype(i32), axis=1)
    present = counts > 0
    blk_e = jnp.minimum(blk_e, jnp.max(jnp.where(present, eids, 0))).astype(i32)
    later = (eids[None, :] > eids[:, None]) & present[None, :]
    nxt = jnp.min(jnp.where(later, eids[None, :], N_EXPERTS), axis=1)
    nxt_e = jnp.where(nxt < N_EXPERTS, nxt, -1).astype(i32)
    return blk_e, nxt_e, row_src, meta


def kernel(x_prompt, x_sample, cache_k, cache_v, state_fwd, state_bwd, c, c_ctx, ada_w, ada_b, norm_mix_g,
           norm_ffn_g, w_in, diff_lambda, diff_subln_g, gla_gk_w, gla_gk_b, gla_norm_g, w_branch_a,
           w_branch_b, w_out, router_w, router_b, expert_w_gu, expert_b_gu, expert_w_down, expert_b_down,
           final_norm_g):
    x = jnp.concatenate([x_prompt.reshape(T_P, D), x_sample.reshape(T_S, D)], axis=0)
    cvec = jnp.concatenate([c_ctx[None], c, jnp.zeros((3, D), f32)], axis=0)
    mods = _ada_mod(cvec, ada_w, ada_b)
    cos, sin = _rope_tables()
    ck = cache_k.reshape(DEC_BATCH, DEPTH, PAST_LEN, D)
    cv = cache_v.reshape(DEC_BATCH, DEPTH, PAST_LEN, D)
    main_w = 6 * D
    lr0 = main_w
    g0 = main_w + 2 * GK_RANK

    ks, vs, sfs, sbs = [], [], [], []
    yn = None
    for l in range(DEPTH):
        lam_init = 0.8 - 0.6 * math.exp(-0.3 * l)
        mod = mods[l]
        wm = w_in[l, :, :main_w].astype(bf16)
        wl = jnp.pad(w_in[l, :, lr0:g0], ((0, 0), (0, LANES - 2 * GK_RANK))).astype(bf16)
        wg = w_in[l, :, g0:].astype(bf16)
        q, k, v, bq, bk, bv, bg, lr, ga, gb = _inproj(x, mod, norm_mix_g[l][None], wm, wg, wl, cos, sin)
        ks.append(k[:T_P].reshape(BATCH, SEQ, D))
        vs.append(v[:T_P].reshape(BATCH, SEQ, D))

        lq = jnp.pad(diff_lambda[l], ((0, 4), (0, LANES - HD_A)))
        sg = diff_subln_g[l][None]
        oap = _attn_ctx(q, k, v, lq, sg, lam_init)
        oas = _attn_lat(q, k, v, ck, cv, lq, sg, lam_init, l)

        wf = jnp.pad(gla_gk_w[l, 0], ((0, LANES - GK_RANK), (0, 0)))
        wb = jnp.pad(gla_gk_w[l, 1], ((GK_RANK, LANES - 2 * GK_RANK), (0, 0)))
        bfv = gla_gk_b[l, 0][None]
        bbv = gla_gk_b[l, 1][None]
        gn = gla_norm_g[l][None]
        ogp, sf, sb = _gla(bq, bk, bv, bg, lr, wf, wb, bfv, bbv, gn, SEQ, 0, BATCH)
        ogs = _gla(bq, bk, bv, bg, lr, wf, wb, bfv, bbv, gn, DEC_SEQ, T_P // DEC_SEQ, DEC_BATCH,
                   states=(state_fwd, state_bwd), layer=l)
        sfs.append(sf)
        sbs.append(sb)

        rb = jnp.broadcast_to(router_b[l][:, None], (N_EXPERTS, LANES))
        x1, ht, idx8, gate8, rank8, cnt = _merge(
            x, oap, oas, ogp, ogs, ga, gb, mod, w_branch_a[l].astype(bf16), w_branch_b[l].astype(bf16),
            w_out[l].astype(bf16), norm_ffn_g[l][None], router_w[l].T, rb)
        blk_e, nxt_e, row_src, meta = _routing_tables(idx8[:TOP_K], rank8[:TOP_K], gate8[:TOP_K], cnt[:, 0])
        moe = _experts(blk_e, nxt_e, row_src, meta, ht, expert_w_gu, expert_b_gu, expert_w_down,
                       expert_b_down, l)
        x, yn = _combine(x1, moe, mod, final_norm_g[None])

    y_prompt = yn[:T_P].reshape(BATCH, SEQ, D)
    y_sample = yn[T_P:].reshape(DEC_BATCH, DEC_SEQ, D)
    new_k = jnp.stack(ks, axis=1).reshape(BATCH, DEPTH, SEQ, N_HEADS_A, 2, HD_A)
    new_v = jnp.stack(vs, axis=1).reshape(BATCH, DEPTH, SEQ, N_HEADS_A, 2 * HD_A)
    new_sf = jnp.stack(sfs, axis=1)
    new_sb = jnp.stack(sbs, axis=1)
    return (y_prompt, y_sample, new_k, new_v, new_sf, new_sb)
```

```python
import functools
import math

import jax
import jax.numpy as jnp
from jax import lax
from jax.experimental import pallas as pl
from jax.experimental.pallas import tpu as pltpu

f32 = jnp.float32
bf16 = jnp.bfloat16
i32 = jnp.int32

D = 1024
BATCH = 16
SEQ = 256
DEPTH = 2
DEC_BATCH = 4
DEC_SEQ = 1024
PAST_LEN = 512
GRID_W = 64
HD_A = 64
N_HEADS_A = 8
ROPE_BASE = 10000.0
N_HEADS_B = 4
DK_B = 128
DV_B = 256
GK_RANK = 16
GATE_NORM = 16.0
CHUNK = 64
N_EXPERTS = 32
TOP_K = 4
MOE_FF = 1024
SWIGLU_LIMIT = 7.0
SWIGLU_ALPHA = 1.702
EPS = 1e-6

T_P = BATCH * SEQ
T_S = DEC_BATCH * DEC_SEQ
T = T_P + T_S
TM = 256
N_TILES = T // TM
P_TILES = T_P // TM
S_TILES_PER_BATCH = DEC_SEQ // TM

MOE_ROWS = 128
N_BLOCKS = (T * TOP_K + N_EXPERTS * (MOE_ROWS - 1)) // MOE_ROWS + 1
N_ROWS = N_BLOCKS * MOE_ROWS
GATHER_DEPTH = 3
SUB = 8
LANES = 128
FAST_DECAY_LIMIT = -60.0

_VMEM_LIMIT = 56 * 1024 * 1024
_EXPERTS_VMEM_LIMIT = 60 * 1024 * 1024


def _cparams(sem, vmem=_VMEM_LIMIT):
    return pltpu.CompilerParams(dimension_semantics=sem, vmem_limit_bytes=vmem)


def _dot(a, b):
    return jnp.dot(a, b, preferred_element_type=f32)


def _dot_nt(a, b):
    return lax.dot_general(a, b, (((1,), (1,)), ((), ())), preferred_element_type=f32)


def _split(x):
    hi = x.astype(bf16)
    lo = (x - hi.astype(f32)).astype(bf16)
    return hi, lo


def _dot3(a, b):
    ah, al = _split(a)
    bh, bl = _split(b)
    return _dot(ah, bh) + _dot(ah, bl) + _dot(al, bh)


def _dot3_nt(a, b):
    ah, al = _split(a)
    bh, bl = _split(b)
    return _dot_nt(ah, bh) + _dot_nt(ah, bl) + _dot_nt(al, bh)


def _silu(x):
    return x * jax.nn.sigmoid(x)


def _mod_row(i):
    return jnp.where(i < P_TILES, 0, 1 + lax.shift_right_logical(jnp.maximum(i - P_TILES, 0), 2))


def _ada_kernel(c_ref, w_ref, b_ref, o_ref):
    o_ref[0] = _dot3(_silu(c_ref[...]), w_ref[0]) + b_ref[0]


def _ada_mod(cvec, ada_w, ada_b):
    tn = 1536
    return pl.pallas_call(
        _ada_kernel,
        grid=(DEPTH, 6 * D // tn),
        in_specs=[
            pl.BlockSpec((8, D), lambda l, j: (0, 0)),
            pl.BlockSpec((1, D, tn), lambda l, j: (l, 0, j)),
            pl.BlockSpec((1, 1, tn), lambda l, j: (l, 0, j)),
        ],
        out_specs=pl.BlockSpec((1, 8, tn), lambda l, j: (l, 0, j)),
        out_shape=jax.ShapeDtypeStruct((DEPTH, 8, 6 * D), f32),
        compiler_params=_cparams(("arbitrary", "arbitrary")),
        name="ada_mod",
    )(cvec, ada_w, ada_b.reshape(DEPTH, 1, 6 * D))


def _inproj_kernel(x_ref, mod_ref, g_ref, wm_ref, wg_ref, wl_ref, cos_ref, sin_ref,
                   q_ref, k_ref, v_ref, bq_ref, bk_ref, bv_ref, bg_ref, lr_ref, ga_ref, gb_ref):
    i = pl.program_id(0)
    m = mod_ref[pl.ds(_mod_row(i), 1), :]
    shift = m[:, 0:D]
    scale = m[:, D:2 * D]
    x = x_ref[...]
    h = x * lax.rsqrt(jnp.mean(x * x, axis=-1, keepdims=True) + EPS) * g_ref[...]
    hb = (h * (1.0 + scale) + shift).astype(bf16)

    cos = cos_ref[...]
    sin = sin_ref[...]
    lane = lax.broadcasted_iota(i32, (TM, LANES), 1)
    lo_mask = (lane & 31) < 16

    def rope(a):
        partner = jnp.where(lo_mask, pltpu.roll(a, LANES - 16, 1), pltpu.roll(a, 16, 1))
        return a * cos + partner * sin

    aq = _dot(hb, wm_ref[:, 0:D])
    for c in range(D // LANES):
        q_ref[:, c * LANES:(c + 1) * LANES] = rope(aq[:, c * LANES:(c + 1) * LANES]).astype(bf16)
    ak = _dot(hb, wm_ref[:, D:2 * D])
    for c in range(D // LANES):
        k_ref[:, c * LANES:(c + 1) * LANES] = rope(ak[:, c * LANES:(c + 1) * LANES])
    v_ref[...] = _dot(hb, wm_ref[:, 2 * D:3 * D])
    bq_ref[...] = _dot(hb, wm_ref[:, 3 * D:3 * D + 512])
    bk_ref[...] = _dot(hb, wm_ref[:, 3 * D + 512:4 * D])
    bv_ref[...] = _dot(hb, wm_ref[:, 4 * D:5 * D])
    bg_ref[...] = _dot(hb, wm_ref[:, 5 * D:6 * D])
    lr_ref[...] = _dot(hb, wl_ref[...])
    ga_ref[...] = _dot(hb, wg_ref[:, 0:D])
    gb_ref[...] = _dot(hb, wg_ref[:, D:2 * D])


def _inproj(x, mod, g, wm, wg, wl, cos, sin):
    tile = lambda w: pl.BlockSpec((TM, w), lambda i: (i, 0))
    res = lambda shape: pl.BlockSpec(shape, lambda i: (0, 0), pipeline_mode=pl.Buffered(1))
    widths = (D, D, D, 512, 512, D, D, LANES, D, D)
    dtypes = (bf16,) + (f32,) * 9
    return pl.pallas_call(
        _inproj_kernel,
        grid=(N_TILES,),
        in_specs=[tile(D), res((8, 6 * D)), res((1, D)), res((D, 6 * D)), res((D, 2 * D)),
                  res((D, LANES)), tile(LANES), tile(LANES)],
        out_specs=[tile(w) for w in widths],
        out_shape=[jax.ShapeDtypeStruct((T, w), dt) for w, dt in zip(widths, dtypes)],
        compiler_params=_cparams(("arbitrary",)),
        name="inproj",
    )(x, mod, g, wm, wg, wl, cos, sin)


def _lam_from(lq_ref, lam_init):
    lq = lq_ref[...]
    s01 = jnp.sum(lq[0:1] * lq[1:2], axis=-1, keepdims=True)
    s23 = jnp.sum(lq[2:3] * lq[3:4], axis=-1, keepdims=True)
    return jnp.exp(s01) - jnp.exp(s23) + lam_init


def _subln(o, g, lam_init):
    o = o * lax.rsqrt(jnp.mean(o * o, axis=-1, keepdims=True) + EPS)
    return o * g * (1.0 - lam_init)


def _attn_ctx_kernel(q_ref, k_ref, v_ref, lq_ref, g_ref, o_ref, *, lam_init):
    q = q_ref[...]
    kb = k_ref[...].astype(bf16)
    vb = v_ref[...].astype(bf16)
    lo = lax.broadcasted_iota(i32, q.shape, 1) < HD_A
    outs = []
    for qm in (jnp.where(lo, q, 0), jnp.where(lo, 0, q)):
        s = _dot_nt(qm, kb) * (HD_A ** -0.5)
        p = jnp.exp(s - jnp.max(s, axis=-1, keepdims=True))
        l = jnp.sum(p, axis=-1, keepdims=True)
        outs.append(_dot(p.astype(bf16), vb) / l)
    o = outs[0] - _lam_from(lq_ref, lam_init) * outs[1]
    o_ref[...] = _subln(o, g_ref[...], lam_init).astype(bf16)


def _attn_lat_kernel(q_ref, k_ref, v_ref, ck_ref, cv_ref, lq_ref, g_ref, o_ref, *, lam_init):
    q = q_ref[...]
    kb = k_ref[...].astype(bf16)
    vb = v_ref[...].astype(bf16)
    ckb = ck_ref[...].astype(bf16)
    cvb = cv_ref[...].astype(bf16)
    lo = lax.broadcasted_iota(i32, q.shape, 1) < HD_A
    outs = []
    for qm in (jnp.where(lo, q, 0), jnp.where(lo, 0, q)):
        sc = _dot_nt(qm, ckb) * (HD_A ** -0.5)
        ss = _dot_nt(qm, kb) * (HD_A ** -0.5)
        mx = jnp.maximum(jnp.max(sc, axis=-1, keepdims=True), jnp.max(ss, axis=-1, keepdims=True))
        pc = jnp.exp(sc - mx)
        ps = jnp.exp(ss - mx)
        l = jnp.sum(pc, axis=-1, keepdims=True) + jnp.sum(ps, axis=-1, keepdims=True)
        outs.append((_dot(pc.astype(bf16), cvb) + _dot(ps.astype(bf16), vb)) / l)
    o = outs[0] - _lam_from(lq_ref, lam_init) * outs[1]
    o_ref[...] = _subln(o, g_ref[...], lam_init).astype(bf16)


def _attn_ctx(q, k, v, lq, g, lam_init):
    blk = pl.BlockSpec((SEQ, LANES), lambda b, h: (b, h))
    small = lambda s: pl.BlockSpec(s, lambda b, h: (0, 0))
    return pl.pallas_call(
        functools.partial(_attn_ctx_kernel, lam_init=lam_init),
        grid=(BATCH, N_HEADS_A),
        in_specs=[blk, blk, blk, small((8, LANES)), small((1, LANES))],
        out_specs=blk,
        out_shape=jax.ShapeDtypeStruct((T_P, D), bf16),
        compiler_params=_cparams(("arbitrary", "arbitrary")),
        name="attn_ctx",
    )(q, k, v, lq, g)


def _attn_lat(q, k, v, ck, cv, lq, g, lam_init, layer):
    nq = DEC_SEQ // TM
    qblk = pl.BlockSpec((TM, LANES), lambda b, h, t: (P_TILES + b * nq + t, h))
    kblk = pl.BlockSpec((DEC_SEQ, LANES), lambda b, h, t: (T_P // DEC_SEQ + b, h))
    cblk = pl.BlockSpec((None, None, PAST_LEN, LANES), lambda b, h, t: (b, layer, 0, h))
    small = lambda s: pl.BlockSpec(s, lambda b, h, t: (0, 0))
    return pl.pallas_call(
        functools.partial(_attn_lat_kernel, lam_init=lam_init),
        grid=(DEC_BATCH, N_HEADS_A, nq),
        in_specs=[qblk, kblk, kblk, cblk, cblk, small((8, LANES)), small((1, LANES))],
        out_specs=pl.BlockSpec((TM, LANES), lambda b, h, t: (b * nq + t, h)),
        out_shape=jax.ShapeDtypeStruct((T_S, D), bf16),
        compiler_params=_cparams(("arbitrary", "arbitrary", "arbitrary")),
        name="attn_lat",
    )(q, k, v, ck, cv, lq, g)


def _log_decay(lr, w_ref, b_ref):
    z = _dot3(lr, w_ref[...]) + b_ref[...]
    return -(jnp.maximum(-z, 0.0) + jnp.log(1.0 + jnp.exp(-jnp.abs(z)))) / GATE_NORM


def _gla_direction(q_ref, k_ref, v3b, g, st0, reverse, b_ref, att_ref, sall_ref):
    tb = g.shape[0]
    n = tb // CHUNK
    scale = DK_B ** -0.5
    ri = lax.broadcasted_iota(i32, (CHUNK, CHUNK), 0)
    ci = lax.broadcasted_iota(i32, (CHUNK, CHUNK), 1)
    keep = (ci >= ri) if reverse else (ci <= ri)
    tri = jnp.broadcast_to(jnp.where(keep, 1.0, 0.0).astype(bf16)[None], (n, CHUNK, CHUNK))

    g3 = g.reshape(n, CHUNK, DK_B)
    ghi, glo = _split(g3)
    bdot = lambda a, b: lax.dot_general(a, b, (((2,), (1,)), ((0,), (0,))), preferred_element_type=f32)
    b = bdot(tri, ghi) + bdot(tri, glo)
    tot = b[:, 0:1, :] if reverse else b[:, CHUNK - 1:CHUNK, :]
    q3 = q_ref[...].reshape(n, CHUNK, DK_B) * scale
    k3 = k_ref[...].reshape(n, CHUNK, DK_B)

    qs = (q3 * jnp.exp(b)).astype(bf16)
    kl = (k3 * jnp.exp(tot - b)).astype(bf16)
    kn = (k3 * jnp.exp(jnp.minimum(-b, 80.0))).astype(bf16)
    att = lax.dot_general(qs, kn, (((2,), (2,)), ((0,), (0,))), preferred_element_type=f32)
    att_ref[...] = jnp.where(keep[None], att, 0.0)

    b_ref[...] = b
    need_fix = jnp.min(tot) < FAST_DECAY_LIMIT

    @pl.when(need_fix)
    def _():
        lane = lax.broadcasted_iota(i32, (CHUNK, CHUNK), 1)

        def chunk_body(c, carry):
            bc = b_ref[c]
            row0 = pl.multiple_of(c * CHUNK, CHUNK)
            tot_c = bc[0:1] if reverse else bc[CHUNK - 1:CHUNK]

            @pl.when(jnp.min(tot_c) < FAST_DECAY_LIMIT)
            def _():
                qc = q_ref[pl.ds(row0, CHUNK), :] * scale

                def jbody(j, acc):
                    bj = b_ref[c, pl.ds(j, 1), :]
                    kj = k_ref[pl.ds(row0 + j, 1), :]
                    e = jnp.exp(jnp.minimum(bc - bj, 0.0))
                    col = jnp.sum(qc * e * kj, axis=-1, keepdims=True)
                    return acc + jnp.where(lane == j, col, 0.0)

                acc = lax.fori_loop(0, CHUNK, jbody, jnp.zeros((CHUNK, CHUNK), f32))
                att_ref[c] = jnp.where(keep, acc, 0.0)

            return carry

        lax.fori_loop(0, n, chunk_body, 0)

    u = lax.dot_general(v3b, kl, (((1,), (1,)), ((0,), (0,))), preferred_element_type=f32)
    dec = jnp.exp(tot)
    st = st0
    order = range(n - 1, -1, -1) if reverse else range(n)
    for c in order:
        sall_ref[c] = st
        st = st * dec[c] + u[c]
    inter = lax.dot_general(qs, sall_ref[...].astype(bf16), (((2,), (2,)), ((0,), (0,))),
                            preferred_element_type=f32)
    intra = lax.dot_general(att_ref[...].astype(bf16), v3b, (((2,), (1,)), ((0,), (0,))),
                            preferred_element_type=f32)
    return (inter + intra).reshape(tb, DV_B), st


def _gla_kernel(*refs, has_init):
    if has_init:
        (q_ref, k_ref, v_ref, bg_ref, lr_ref, wf_ref, wb_ref, bf_ref, bb_ref, gn_ref, sf0_ref, sb0_ref,
         o_ref, b_ref, att_ref, sall_ref) = refs
        stf0 = sf0_ref[...].T
        stb0 = sb0_ref[...].T
    else:
        (q_ref, k_ref, v_ref, bg_ref, lr_ref, wf_ref, wb_ref, bf_ref, bb_ref, gn_ref,
         o_ref, sf_ref, sb_ref, b_ref, att_ref, sall_ref) = refs
        stf0 = jnp.zeros((DV_B, DK_B), f32)
        stb0 = stf0
    tb = q_ref.shape[0]
    n = tb // CHUNK
    lr = lr_ref[...]
    v3b = v_ref[...].reshape(n, CHUNK, DV_B).astype(bf16)
    gf = _log_decay(lr, wf_ref, bf_ref)
    of, stf = _gla_direction(q_ref, k_ref, v3b, gf, stf0, False, b_ref, att_ref, sall_ref)
    gb = _log_decay(lr, wb_ref, bb_ref)
    ob, stb = _gla_direction(q_ref, k_ref, v3b, gb, stb0, True, b_ref, att_ref, sall_ref)
    o = of + ob
    o = o * lax.rsqrt(jnp.mean(o * o, axis=-1, keepdims=True) + EPS) * gn_ref[...]
    o_ref[...] = (o * _silu(bg_ref[...])).astype(bf16)
    if not has_init:
        sf_ref[...] = stf.T
        sb_ref[...] = stb.T


def _gla(bq, bk, bv, bg, lr, wf, wb, bfv, bbv, gn, tb, row_blk0, nb, states=None, layer=0):
    has_init = states is not None
    n = tb // CHUNK
    rows = lambda w: pl.BlockSpec((tb, w), lambda b, h: (row_blk0 + b, h))
    lrs = pl.BlockSpec((tb, LANES), lambda b, h: (row_blk0 + b, 0))
    wsp = pl.BlockSpec((LANES, DK_B), lambda b, h: (0, h))
    bsp = pl.BlockSpec((1, DK_B), lambda b, h: (0, h))
    gsp = pl.BlockSpec((1, DV_B), lambda b, h: (0, 0))
    in_specs = [rows(DK_B), rows(DK_B), rows(DV_B), rows(DV_B), lrs, wsp, wsp, bsp, bsp, gsp]
    args = [bq, bk, bv, bg, lr, wf, wb, bfv, bbv, gn]
    osp = pl.BlockSpec((tb, DV_B), lambda b, h: (b, h))
    oshape = jax.ShapeDtypeStruct((nb * tb, D), bf16)
    if has_init:
        ssp = pl.BlockSpec((None, None, None, DK_B, DV_B), lambda b, h: (b, layer, h, 0, 0))
        in_specs += [ssp, ssp]
        args += list(states)
        out_specs, out_shape = osp, oshape
    else:
        ssp = pl.BlockSpec((None, None, DK_B, DV_B), lambda b, h: (b, h, 0, 0))
        sshape = jax.ShapeDtypeStruct((nb, N_HEADS_B, DK_B, DV_B), f32)
        out_specs, out_shape = [osp, ssp, ssp], [oshape, sshape, sshape]
    return pl.pallas_call(
        functools.partial(_gla_kernel, has_init=has_init),
        grid=(nb, N_HEADS_B),
        in_specs=in_specs,
        out_specs=out_specs,
        out_shape=out_shape,
        scratch_shapes=[pltpu.VMEM((n, CHUNK, DK_B), f32), pltpu.VMEM((n, CHUNK, CHUNK), f32),
                        pltpu.VMEM((n, DV_B, DK_B), f32)],
        compiler_params=_cparams(("arbitrary", "arbitrary")),
        name="gla_lat" if has_init else "gla_ctx",
    )(*args)


def _merge_kernel(x_ref, oap_ref, oas_ref, ogp_ref, ogs_ref, ga_ref, gb_ref, mod_ref, wa_ref, wb_ref,
                  wo_ref, g_ref, rw_ref, rb_ref,
                  x1_ref, ht_ref, idx_ref, gate_ref, rank_ref, cnt_ref, base_ref):
    i = pl.program_id(0)

    @pl.when(i == 0)
    def _():
        base_ref[...] = jnp.zeros_like(base_ref)

    is_ctx = i < P_TILES
    oa = jnp.where(is_ctx, oap_ref[...], oas_ref[...])
    og = jnp.where(is_ctx, ogp_ref[...], ogs_ref[...])
    ya = _dot(oa, wa_ref[...])
    yb = _dot(og, wb_ref[...])
    merged = jax.nn.sigmoid(ga_ref[...]) * ya + jax.nn.sigmoid(gb_ref[...]) * yb
    mix = _dot(merged.astype(bf16), wo_ref[...])
    m = mod_ref[pl.ds(_mod_row(i), 1), :]
    x1 = x_ref[...] + m[:, 2 * D:3 * D] * mix
    x1_ref[...] = x1
    h = x1 * lax.rsqrt(jnp.mean(x1 * x1, axis=-1, keepdims=True) + EPS) * g_ref[...]
    h = h * (1.0 + m[:, 4 * D:5 * D]) + m[:, 3 * D:4 * D]
    for s in range(SUB):
        ht_ref[pl.ds(s, TM, stride=SUB), :] = h[:, s * LANES:(s + 1) * LANES]

    lg = _dot3_nt(rw_ref[...], h) + rb_ref[...][:, 0:1]
    eid = lax.broadcasted_iota(i32, (N_EXPERTS, TM), 0)
    vals, sels, idxs = [], [], []
    for _ in range(TOP_K):
        mx = jnp.max(lg, axis=0, keepdims=True)
        ik = jnp.min(jnp.where(lg == mx, eid, N_EXPERTS), axis=0, keepdims=True)
        sel = eid == ik
        vals.append(mx)
        idxs.append(ik)
        sels.append(sel)
        lg = jnp.where(sel, -jnp.inf, lg)
    es = [jnp.exp(v - vals[0]) for v in vals]
    den = es[0] + es[1] + es[2] + es[3]
    onehot = jnp.where(sels[0] | sels[1] | sels[2] | sels[3], 1.0, 0.0)
    tr = lax.broadcasted_iota(i32, (TM, TM), 0)
    tc = lax.broadcasted_iota(i32, (TM, TM), 1)
    before = jnp.where(tr < tc, 1.0, 0.0).astype(bf16)
    pos = base_ref[:, 0:1] + _dot(onehot.astype(bf16), before)
    zrow_i = jnp.zeros((8 - TOP_K, TM), i32)
    ranks = [jnp.sum(jnp.where(s, pos, 0.0), axis=0, keepdims=True).astype(i32) for s in sels]
    idx_ref[...] = jnp.concatenate(idxs + [zrow_i], axis=0)
    rank_ref[...] = jnp.concatenate(ranks + [zrow_i], axis=0)
    gate_ref[...] = jnp.concatenate([e / den for e in es] + [jnp.zeros((8 - TOP_K, TM), f32)], axis=0)
    base = base_ref[...] + jnp.sum(onehot, axis=1, keepdims=True)
    base_ref[...] = base
    cnt_ref[...] = base.astype(i32)


def _merge(x, oap, oas, ogp, ogs, ga, gb, mod, wa, wb, wo, g, rw, rb):
    tile = lambda w: pl.BlockSpec((TM, w), lambda i: (i, 0))
    ctx = pl.BlockSpec((TM, D), lambda i: (jnp.minimum(i, P_TILES - 1), 0))
    lat = pl.BlockSpec((TM, D), lambda i: (jnp.maximum(i - P_TILES, 0), 0))
    res = lambda shape: pl.BlockSpec(shape, lambda i: (0, 0), pipeline_mode=pl.Buffered(1))
    meta = pl.BlockSpec((8, TM), lambda i: (0, i))
    return pl.pallas_call(
        _merge_kernel,
        grid=(N_TILES,),
        in_specs=[tile(D), ctx, lat, ctx, lat, tile(D), tile(D), res((8, 6 * D)), res((D, D)), res((D, D)),
                  res((D, D)), res((1, D)), res((N_EXPERTS, D)), res((N_EXPERTS, LANES))],
        out_specs=[tile(D), pl.BlockSpec((TM * SUB, LANES), lambda i: (i, 0)), meta, meta, meta,
                   pl.BlockSpec((N_EXPERTS, LANES), lambda i: (0, 0))],
        out_shape=[jax.ShapeDtypeStruct((T, D), f32), jax.ShapeDtypeStruct((T * SUB, LANES), f32),
                   jax.ShapeDtypeStruct((8, T), i32), jax.ShapeDtypeStruct((8, T), f32),
                   jax.ShapeDtypeStruct((8, T), i32), jax.ShapeDtypeStruct((N_EXPERTS, LANES), i32)],
        scratch_shapes=[pltpu.VMEM((N_EXPERTS, LANES), f32)],
        compiler_params=_cparams(("arbitrary",)),
        name="merge_route",
    )(x, oap, oas, ogp, ogs, ga, gb, mod, wa, wb, wo, g, rw, rb)


def _experts_kernel(blk_e_ref, nxt_e_ref, nused_ref, src_ref, src1_ref, src2_ref, meta_ref, h_hbm, wgu_hbm,
                    bgu_ref, wdn_hbm, bdn_ref, out_hbm, acc, wgu_st, wdn_st, wgu_bf, wdn_bf, xg, yb, gsem,
                    wsem, osem, *, layer):
    i = pl.program_id(0)
    n_used = nused_ref[0]

    @pl.when(i < n_used)
    def _():
        _experts_step(i, n_used, blk_e_ref, nxt_e_ref, src_ref, src1_ref, src2_ref, meta_ref, h_hbm, wgu_hbm,
                      bgu_ref, wdn_hbm, bdn_ref, out_hbm, acc, wgu_st, wdn_st, wgu_bf, wdn_bf, xg, yb, gsem,
                      wsem, osem, layer)


def _experts_step(i, n_used, blk_e_ref, nxt_e_ref, src_ref, src1_ref, src2_ref, meta_ref, h_hbm, wgu_hbm,
                  bgu_ref, wdn_hbm, bdn_ref, out_hbm, acc, wgu_st, wdn_st, wgu_bf, wdn_bf, xg, yb, gsem, wsem,
                  osem, layer):
    slot = i % GATHER_DEPTH
    buf_rows = MOE_ROWS * SUB

    def gather(idx_ref, s):
        for r in range(MOE_ROWS):
            t = jnp.minimum(idx_ref[0, 0, r], T - 1)
            pltpu.make_async_copy(h_hbm.at[pl.ds(pl.multiple_of(t * SUB, SUB), SUB)],
                                  xg.at[s, pl.ds(r * SUB, SUB)], gsem.at[s]).start()

    def gather_wait(s):
        pltpu.make_async_copy(h_hbm.at[pl.ds(0, buf_rows)], xg.at[s], gsem.at[s]).wait()

    def weight_copies(e):
        return (pltpu.make_async_copy(wgu_hbm.at[layer, e], wgu_st, wsem.at[0]),
                pltpu.make_async_copy(wdn_hbm.at[layer, e], wdn_st, wsem.at[1]))

    e = blk_e_ref[i]

    @pl.when(i == 0)
    def _():
        for cp in weight_copies(e):
            cp.start()
        gather(src_ref, 0)
        gather(src1_ref, 1)
        acc[...] = jnp.zeros_like(acc)

    @pl.when((i == 0) | (e != blk_e_ref[jnp.maximum(i - 1, 0)]))
    def _():
        for cp in weight_copies(e):
            cp.wait()
        wgu_bf[...] = wgu_st[...].astype(bf16)
        wdn_bf[...] = wdn_st[...].astype(bf16)
        en = nxt_e_ref[e]

        @pl.when(en >= 0)
        def _():
            for cp in weight_copies(en):
                cp.start()

    gather(src2_ref, (i + GATHER_DEPTH - 1) % GATHER_DEPTH)
    gather_wait(slot)

    xb = jnp.concatenate([xg[slot, pl.ds(s, MOE_ROWS, stride=SUB), :] for s in range(SUB)],
                         axis=1).astype(bf16)
    gu = _dot(xb, wgu_bf[...]) + bgu_ref[...]
    gt = jnp.minimum(gu[:, :MOE_FF], SWIGLU_LIMIT)
    up = jnp.clip(gu[:, MOE_FF:], -SWIGLU_LIMIT, SWIGLU_LIMIT)
    act = (up + 1.0) * gt * jax.nn.sigmoid(SWIGLU_ALPHA * gt)
    y = _dot(act.astype(bf16), wdn_bf[...]) + bdn_ref[...]
    ri = lax.broadcasted_iota(i32, (MOE_ROWS, MOE_ROWS), 0)
    ci = lax.broadcasted_iota(i32, (MOE_ROWS, MOE_ROWS), 1)
    diag = ri == ci
    col = lambda v: jnp.sum(jnp.where(diag, jnp.broadcast_to(v, (MOE_ROWS, MOE_ROWS)), 0.0),
                            axis=1, keepdims=True)
    y = jnp.where(col(meta_ref[0, 1:2, :]) > 0.5, y * col(meta_ref[0, 0:1, :]), 0.0)
    for s in range(SUB):
        yb[pl.ds(s, MOE_ROWS, stride=SUB), :] = y[:, s * LANES:(s + 1) * LANES]
    group = 16
    for r0 in range(0, MOE_ROWS, group):
        offs = [pl.multiple_of(src_ref[0, 0, r] * SUB, SUB) for r in range(r0, r0 + group)]
        olds = [acc[pl.ds(o, SUB), :] for o in offs]
        for j, o in enumerate(offs):
            acc[pl.ds(o, SUB), :] = olds[j] + yb[(r0 + j) * SUB:(r0 + j + 1) * SUB, :]

    @pl.when(i == n_used - 1)
    def _():
        gather_wait((i + 1) % GATHER_DEPTH)
        gather_wait((i + 2) % GATHER_DEPTH)
        cp = pltpu.make_async_copy(acc.at[pl.ds(0, T * SUB)], out_hbm, osem.at[0])
        cp.start()
        cp.wait()


def _experts(blk_e, nxt_e, n_used, row_src, meta, ht, wgu, bgu, wdn, bdn, layer):
    src3 = row_src.reshape(N_BLOCKS, 1, MOE_ROWS)
    ahead = lambda k: pl.BlockSpec((1, 1, MOE_ROWS),
                                   lambda i, be, ne, nu: (jnp.minimum(i + k, N_BLOCKS - 1), 0, 0),
                                   memory_space=pltpu.SMEM)
    grid_spec = pltpu.PrefetchScalarGridSpec(
        num_scalar_prefetch=3,
        grid=(N_BLOCKS,),
        in_specs=[
            ahead(0), ahead(1), ahead(2),
            pl.BlockSpec((1, 8, MOE_ROWS), lambda i, be, ne, nu: (i, 0, 0)),
            pl.BlockSpec(memory_space=pl.ANY),
            pl.BlockSpec(memory_space=pl.ANY),
            pl.BlockSpec((None, None, 1, 2 * MOE_FF), lambda i, be, ne, nu: (layer, be[i], 0, 0)),
            pl.BlockSpec(memory_space=pl.ANY),
            pl.BlockSpec((None, None, 1, D), lambda i, be, ne, nu: (layer, be[i], 0, 0)),
        ],
        out_specs=pl.BlockSpec(memory_space=pl.ANY),
        scratch_shapes=[
            pltpu.VMEM(((T + 8) * SUB, LANES), f32),
            pltpu.VMEM((D, 2 * MOE_FF), f32),
            pltpu.VMEM((MOE_FF, D), f32),
            pltpu.VMEM((D, 2 * MOE_FF), bf16),
            pltpu.VMEM((MOE_FF, D), bf16),
            pltpu.VMEM((GATHER_DEPTH, MOE_ROWS * SUB, LANES), f32),
            pltpu.VMEM((MOE_ROWS * SUB, LANES), f32),
            pltpu.SemaphoreType.DMA((GATHER_DEPTH,)),
            pltpu.SemaphoreType.DMA((2,)),
            pltpu.SemaphoreType.DMA((1,)),
        ],
    )
    return pl.pallas_call(
        functools.partial(_experts_kernel, layer=layer),
        grid_spec=grid_spec,
        out_shape=jax.ShapeDtypeStruct((T * SUB, LANES), f32),
        compiler_params=_cparams(("arbitrary",), _EXPERTS_VMEM_LIMIT),
        name="experts",
    )(blk_e, nxt_e, n_used, src3, src3, src3, meta, ht, wgu, bgu.reshape(DEPTH, N_EXPERTS, 1, 2 * MOE_FF),
      wdn, bdn.reshape(DEPTH, N_EXPERTS, 1, D))


def _combine_kernel(x1_ref, moe_ref, mod_ref, fg_ref, x2_ref, yn_ref):
    i = pl.program_id(0)
    moe = jnp.concatenate([moe_ref[pl.ds(s, TM, stride=SUB), :] for s in range(SUB)], axis=1)
    m = mod_ref[pl.ds(_mod_row(i), 1), :]
    x2 = x1_ref[...] + m[:, 5 * D:6 * D] * moe
    x2_ref[...] = x2
    yn_ref[...] = x2 * lax.rsqrt(jnp.mean(x2 * x2, axis=-1, keepdims=True) + EPS) * fg_ref[...]


def _combine(x1, moe, mod, fg):
    tile = pl.BlockSpec((TM, D), lambda i: (i, 0))
    res = lambda shape: pl.BlockSpec(shape, lambda i: (0, 0), pipeline_mode=pl.Buffered(1))
    return pl.pallas_call(
        _combine_kernel,
        grid=(N_TILES,),
        in_specs=[tile, pl.BlockSpec((TM * SUB, LANES), lambda i: (i, 0)), res((8, 6 * D)), res((1, D))],
        out_specs=[tile, tile],
        out_shape=[jax.ShapeDtypeStruct((T, D), f32), jax.ShapeDtypeStruct((T, D), f32)],
        compiler_params=_cparams(("arbitrary",)),
        name="combine",
    )(x1, moe, mod, fg)


def _rope_tables():
    nf = HD_A // 4
    inv = ROPE_BASE ** (-jnp.arange(nf, dtype=f32) / nf)
    n = jnp.arange(DEC_SEQ)
    row = (n // GRID_W).astype(f32)[:, None] * inv
    col = (n % GRID_W).astype(f32)[:, None] * inv
    cos64 = jnp.concatenate([jnp.cos(row), jnp.cos(row), jnp.cos(col), jnp.cos(col)], axis=-1)
    sin64 = jnp.concatenate([-jnp.sin(row), jnp.sin(row), -jnp.sin(col), jnp.sin(col)], axis=-1)
    cos_s = jnp.tile(jnp.tile(cos64, (1, 2)), (DEC_BATCH, 1))
    sin_s = jnp.tile(jnp.tile(sin64, (1, 2)), (DEC_BATCH, 1))
    cos = jnp.concatenate([jnp.ones((T_P, LANES), f32), cos_s], axis=0)
    sin = jnp.concatenate([jnp.zeros((T_P, LANES), f32), sin_s], axis=0)
    return cos, sin


def _routing_tables(idx, rank, gate, counts):
    padded = (counts + MOE_ROWS - 1) // MOE_ROWS * MOE_ROWS
    pad_end = jnp.cumsum(padded)
    pad_start = pad_end - padded
    eids = jnp.arange(N_EXPERTS, dtype=i32)
    start_of = jnp.sum(jnp.where(idx[..., None] == eids, pad_start, 0), axis=-1)
    dest = (start_of + rank).reshape(-1)
    inv = jnp.full((N_ROWS,), -1, i32).at[dest].set(jnp.arange(TOP_K * T, dtype=i32))
    valid = inv >= 0
    safe = jnp.maximum(inv, 0)
    row_src = jnp.where(valid, safe % T, T).astype(i32)
    row_gate = jnp.where(valid, gate.reshape(-1)[safe], 0.0)
    meta = jnp.zeros((N_BLOCKS, 8, MOE_ROWS), f32)
    meta = meta.at[:, 0].set(row_gate.reshape(N_BLOCKS, MOE_ROWS))
    meta = meta.at[:, 1].set(valid.astype(f32).reshape(N_BLOCKS, MOE_ROWS))
    starts = jnp.arange(N_BLOCKS, dtype=i32) * MOE_ROWS
    blk_e = jnp.sum((pad_end[None, :] <= starts[:, None]).astype(i32), axis=1)
    present = counts > 0
    blk_e = jnp.minimum(blk_e, jnp.max(jnp.where(present, eids, 0))).astype(i32)
    later = (eids[None, :] > eids[:, None]) & present[None, :]
    nxt = jnp.min(jnp.where(later, eids[None, :], N_EXPERTS), axis=1)
    nxt_e = jnp.where(nxt < N_EXPERTS, nxt, -1).astype(i32)
    n_used = (pad_end[-1:] // MOE_ROWS).astype(i32)
    return blk_e, nxt_e, n_used, row_src, meta


def kernel(x_prompt, x_sample, cache_k, cache_v, state_fwd, state_bwd, c, c_ctx, ada_w, ada_b, norm_mix_g,
           norm_ffn_g, w_in, diff_lambda, diff_subln_g, gla_gk_w, gla_gk_b, gla_norm_g, w_branch_a,
           w_branch_b, w_out, router_w, router_b, expert_w_gu, expert_b_gu, expert_w_down, expert_b_down,
           final_norm_g):
    x = jnp.concatenate([x_prompt.reshape(T_P, D), x_sample.reshape(T_S, D)], axis=0)
    cvec = jnp.concatenate([c_ctx[None], c, jnp.zeros((3, D), f32)], axis=0)
    mods = _ada_mod(cvec, ada_w, ada_b)
    cos, sin = _rope_tables()
    ck = cache_k.reshape(DEC_BATCH, DEPTH, PAST_LEN, D)
    cv = cache_v.reshape(DEC_BATCH, DEPTH, PAST_LEN, D)
    main_w = 6 * D
    lr0 = main_w
    g0 = main_w + 2 * GK_RANK

    ks, vs, sfs, sbs = [], [], [], []
    yn = None
    for l in range(DEPTH):
        lam_init = 0.8 - 0.6 * math.exp(-0.3 * l)
        mod = mods[l]
        wm = w_in[l, :, :main_w].astype(bf16)
        wl = jnp.pad(w_in[l, :, lr0:g0], ((0, 0), (0, LANES - 2 * GK_RANK))).astype(bf16)
        wg = w_in[l, :, g0:].astype(bf16)
        q, k, v, bq, bk, bv, bg, lr, ga, gb = _inproj(x, mod, norm_mix_g[l][None], wm, wg, wl, cos, sin)
        ks.append(k[:T_P].reshape(BATCH, SEQ, D))
        vs.append(v[:T_P].reshape(BATCH, SEQ, D))

        lq = jnp.pad(diff_lambda[l], ((0, 4), (0, LANES - HD_A)))
        sg = diff_subln_g[l][None]
        oap = _attn_ctx(q, k, v, lq, sg, lam_init)
        oas = _attn_lat(q, k, v, ck, cv, lq, sg, lam_init, l)

        wf = jnp.pad(gla_gk_w[l, 0], ((0, LANES - GK_RANK), (0, 0)))
        wb = jnp.pad(gla_gk_w[l, 1], ((GK_RANK, LANES - 2 * GK_RANK), (0, 0)))
        bfv = gla_gk_b[l, 0][None]
        bbv = gla_gk_b[l, 1][None]
        gn = gla_norm_g[l][None]
        ogp, sf, sb = _gla(bq, bk, bv, bg, lr, wf, wb, bfv, bbv, gn, SEQ, 0, BATCH)
        ogs = _gla(bq, bk, bv, bg, lr, wf, wb, bfv, bbv, gn, DEC_SEQ, T_P // DEC_SEQ, DEC_BATCH,
                   states=(state_fwd, state_bwd), layer=l)
        sfs.append(sf)
        sbs.append(sb)

        rb = jnp.broadcast_to(router_b[l][:, None], (N_EXPERTS, LANES))
        x1, ht, idx8, gate8, rank8, cnt = _merge(
            x, oap, oas, ogp, ogs, ga, gb, mod, w_branch_a[l].astype(bf16), w_branch_b[l].astype(bf16),
            w_out[l].astype(bf16), norm_ffn_g[l][None], router_w[l].T, rb)
        blk_e, nxt_e, n_used, row_src, meta = _routing_tables(idx8[:TOP_K], rank8[:TOP_K], gate8[:TOP_K],
                                                              cnt[:, 0])
        moe = _experts(blk_e, nxt_e, n_used, row_src, meta, ht, expert_w_gu, expert_b_gu, expert_w_down,
                       expert_b_down, l)
        x, yn = _combine(x1, moe, mod, final_norm_g[None])

    y_prompt = yn[:T_P].reshape(BATCH, SEQ, D)
    y_sample = yn[T_P:].reshape(DEC_BATCH, DEC_SEQ, D)
    new_k = jnp.stack(ks, axis=1).reshape(BATCH, DEPTH, SEQ, N_HEADS_A, 2, HD_A)
    new_v = jnp.stack(vs, axis=1).reshape(BATCH, DEPTH, SEQ, N_HEADS_A, 2 * HD_A)
    new_sf = jnp.stack(sfs, axis=1)
    new_sb = jnp.stack(sbs, axis=1)
    return (y_prompt, y_sample, new_k, new_v, new_sf, new_sb)
```

```python
import functools
import math

import jax
import jax.numpy as jnp
from jax import lax
from jax.experimental import pallas as pl
from jax.experimental.pallas import tpu as pltpu

f32 = jnp.float32
bf16 = jnp.bfloat16
i32 = jnp.int32

D = 1024
BATCH = 16
SEQ = 256
DEPTH = 2
DEC_BATCH = 4
DEC_SEQ = 1024
PAST_LEN = 512
GRID_W = 64
HD_A = 64
N_HEADS_A = 8
ROPE_BASE = 10000.0
N_HEADS_B = 4
DK_B = 128
DV_B = 256
GK_RANK = 16
GATE_NORM = 16.0
CHUNK = 64
N_EXPERTS = 32
TOP_K = 4
MOE_FF = 1024
SWIGLU_LIMIT = 7.0
SWIGLU_ALPHA = 1.702
EPS = 1e-6

T_P = BATCH * SEQ
T_S = DEC_BATCH * DEC_SEQ
T = T_P + T_S
TM = 256
N_TILES = T // TM
P_TILES = T_P // TM
S_TILES_PER_BATCH = DEC_SEQ // TM

MOE_ROWS = 128
SUB = 8
LANES = 128
FAST_DECAY_LIMIT = -60.0

_VMEM_LIMIT = 56 * 1024 * 1024
_EXPERTS_VMEM_LIMIT = 60 * 1024 * 1024


def _cparams(sem, vmem=_VMEM_LIMIT):
    return pltpu.CompilerParams(dimension_semantics=sem, vmem_limit_bytes=vmem)


def _dot(a, b):
    return jnp.dot(a, b, preferred_element_type=f32)


def _dot_nt(a, b):
    return lax.dot_general(a, b, (((1,), (1,)), ((), ())), preferred_element_type=f32)


def _split(x):
    hi = x.astype(bf16)
    lo = (x - hi.astype(f32)).astype(bf16)
    return hi, lo


def _dot3(a, b):
    ah, al = _split(a)
    bh, bl = _split(b)
    return _dot(ah, bh) + _dot(ah, bl) + _dot(al, bh)


def _dot3_nt(a, b):
    ah, al = _split(a)
    bh, bl = _split(b)
    return _dot_nt(ah, bh) + _dot_nt(ah, bl) + _dot_nt(al, bh)


def _silu(x):
    return x * jax.nn.sigmoid(x)


def _mod_row(i):
    return jnp.where(i < P_TILES, 0, 1 + lax.shift_right_logical(jnp.maximum(i - P_TILES, 0), 2))


def _ada_kernel(c_ref, w_ref, b_ref, o_ref):
    o_ref[0] = _dot3(_silu(c_ref[...]), w_ref[0]) + b_ref[0]


def _ada_mod(cvec, ada_w, ada_b):
    tn = 1536
    return pl.pallas_call(
        _ada_kernel,
        grid=(DEPTH, 6 * D // tn),
        in_specs=[
            pl.BlockSpec((8, D), lambda l, j: (0, 0)),
            pl.BlockSpec((1, D, tn), lambda l, j: (l, 0, j)),
            pl.BlockSpec((1, 1, tn), lambda l, j: (l, 0, j)),
        ],
        out_specs=pl.BlockSpec((1, 8, tn), lambda l, j: (l, 0, j)),
        out_shape=jax.ShapeDtypeStruct((DEPTH, 8, 6 * D), f32),
        compiler_params=_cparams(("arbitrary", "arbitrary")),
        name="ada_mod",
    )(cvec, ada_w, ada_b.reshape(DEPTH, 1, 6 * D))


def _inproj_kernel(x_ref, mod_ref, g_ref, wm_ref, wg_ref, wl_ref, cos_ref, sin_ref,
                   q_ref, k_ref, v_ref, bq_ref, bk_ref, bv_ref, bg_ref, lr_ref, ga_ref, gb_ref):
    i = pl.program_id(0)
    m = mod_ref[pl.ds(_mod_row(i), 1), :]
    shift = m[:, 0:D]
    scale = m[:, D:2 * D]
    x = x_ref[...]
    h = x * lax.rsqrt(jnp.mean(x * x, axis=-1, keepdims=True) + EPS) * g_ref[...]
    hb = (h * (1.0 + scale) + shift).astype(bf16)

    cos = cos_ref[...]
    sin = sin_ref[...]
    lane = lax.broadcasted_iota(i32, (TM, LANES), 1)
    lo_mask = (lane & 31) < 16

    def rope(a):
        partner = jnp.where(lo_mask, pltpu.roll(a, LANES - 16, 1), pltpu.roll(a, 16, 1))
        return a * cos + partner * sin

    aq = _dot(hb, wm_ref[:, 0:D])
    for c in range(D // LANES):
        q_ref[:, c * LANES:(c + 1) * LANES] = rope(aq[:, c * LANES:(c + 1) * LANES]).astype(bf16)
    ak = _dot(hb, wm_ref[:, D:2 * D])
    for c in range(D // LANES):
        k_ref[:, c * LANES:(c + 1) * LANES] = rope(ak[:, c * LANES:(c + 1) * LANES])
    v_ref[...] = _dot(hb, wm_ref[:, 2 * D:3 * D])
    bq_ref[...] = _dot(hb, wm_ref[:, 3 * D:3 * D + 512])
    bk_ref[...] = _dot(hb, wm_ref[:, 3 * D + 512:4 * D])
    bv_ref[...] = _dot(hb, wm_ref[:, 4 * D:5 * D])
    bg_ref[...] = _dot(hb, wm_ref[:, 5 * D:6 * D])
    lr_ref[...] = _dot(hb, wl_ref[...])
    ga_ref[...] = _dot(hb, wg_ref[:, 0:D])
    gb_ref[...] = _dot(hb, wg_ref[:, D:2 * D])


def _inproj(x, mod, g, wm, wg, wl, cos, sin):
    tile = lambda w: pl.BlockSpec((TM, w), lambda i: (i, 0))
    res = lambda shape: pl.BlockSpec(shape, lambda i: (0, 0), pipeline_mode=pl.Buffered(1))
    widths = (D, D, D, 512, 512, D, D, LANES, D, D)
    dtypes = (bf16,) + (f32,) * 9
    return pl.pallas_call(
        _inproj_kernel,
        grid=(N_TILES,),
        in_specs=[tile(D), res((8, 6 * D)), res((1, D)), res((D, 6 * D)), res((D, 2 * D)),
                  res((D, LANES)), tile(LANES), tile(LANES)],
        out_specs=[tile(w) for w in widths],
        out_shape=[jax.ShapeDtypeStruct((T, w), dt) for w, dt in zip(widths, dtypes)],
        compiler_params=_cparams(("arbitrary",)),
        name="inproj",
    )(x, mod, g, wm, wg, wl, cos, sin)


def _lam_from(lq_ref, lam_init):
    lq = lq_ref[...]
    s01 = jnp.sum(lq[0:1] * lq[1:2], axis=-1, keepdims=True)
    s23 = jnp.sum(lq[2:3] * lq[3:4], axis=-1, keepdims=True)
    return jnp.exp(s01) - jnp.exp(s23) + lam_init


def _subln(o, g, lam_init):
    o = o * lax.rsqrt(jnp.mean(o * o, axis=-1, keepdims=True) + EPS)
    return o * g * (1.0 - lam_init)


def _attn_ctx_kernel(q_ref, k_ref, v_ref, lq_ref, g_ref, o_ref, *, lam_init):
    q = q_ref[...]
    kb = k_ref[...].astype(bf16)
    vb = v_ref[...].astype(bf16)
    lo = lax.broadcasted_iota(i32, q.shape, 1) < HD_A
    outs = []
    for qm in (jnp.where(lo, q, 0), jnp.where(lo, 0, q)):
        s = _dot_nt(qm, kb) * (HD_A ** -0.5)
        p = jnp.exp(s - jnp.max(s, axis=-1, keepdims=True))
        l = jnp.sum(p, axis=-1, keepdims=True)
        outs.append(_dot(p.astype(bf16), vb) / l)
    o = outs[0] - _lam_from(lq_ref, lam_init) * outs[1]
    o_ref[...] = _subln(o, g_ref[...], lam_init).astype(bf16)


def _attn_lat_kernel(q_ref, k_ref, v_ref, ck_ref, cv_ref, lq_ref, g_ref, o_ref, *, lam_init):
    q = q_ref[...]
    kb = k_ref[...].astype(bf16)
    vb = v_ref[...].astype(bf16)
    ckb = ck_ref[...].astype(bf16)
    cvb = cv_ref[...].astype(bf16)
    lo = lax.broadcasted_iota(i32, q.shape, 1) < HD_A
    outs = []
    for qm in (jnp.where(lo, q, 0), jnp.where(lo, 0, q)):
        sc = _dot_nt(qm, ckb) * (HD_A ** -0.5)
        ss = _dot_nt(qm, kb) * (HD_A ** -0.5)
        mx = jnp.maximum(jnp.max(sc, axis=-1, keepdims=True), jnp.max(ss, axis=-1, keepdims=True))
        pc = jnp.exp(sc - mx)
        ps = jnp.exp(ss - mx)
        l = jnp.sum(pc, axis=-1, keepdims=True) + jnp.sum(ps, axis=-1, keepdims=True)
        outs.append((_dot(pc.astype(bf16), cvb) + _dot(ps.astype(bf16), vb)) / l)
    o = outs[0] - _lam_from(lq_ref, lam_init) * outs[1]
    o_ref[...] = _subln(o, g_ref[...], lam_init).astype(bf16)


def _attn_ctx(q, k, v, lq, g, lam_init):
    blk = pl.BlockSpec((SEQ, LANES), lambda b, h: (b, h))
    small = lambda s: pl.BlockSpec(s, lambda b, h: (0, 0))
    return pl.pallas_call(
        functools.partial(_attn_ctx_kernel, lam_init=lam_init),
        grid=(BATCH, N_HEADS_A),
        in_specs=[blk, blk, blk, small((8, LANES)), small((1, LANES))],
        out_specs=blk,
        out_shape=jax.ShapeDtypeStruct((T_P, D), bf16),
        compiler_params=_cparams(("arbitrary", "arbitrary")),
        name="attn_ctx",
    )(q, k, v, lq, g)


def _attn_lat(q, k, v, ck, cv, lq, g, lam_init, layer):
    nq = DEC_SEQ // TM
    qblk = pl.BlockSpec((TM, LANES), lambda b, h, t: (P_TILES + b * nq + t, h))
    kblk = pl.BlockSpec((DEC_SEQ, LANES), lambda b, h, t: (T_P // DEC_SEQ + b, h))
    cblk = pl.BlockSpec((None, None, PAST_LEN, LANES), lambda b, h, t: (b, layer, 0, h))
    small = lambda s: pl.BlockSpec(s, lambda b, h, t: (0, 0))
    return pl.pallas_call(
        functools.partial(_attn_lat_kernel, lam_init=lam_init),
        grid=(DEC_BATCH, N_HEADS_A, nq),
        in_specs=[qblk, kblk, kblk, cblk, cblk, small((8, LANES)), small((1, LANES))],
        out_specs=pl.BlockSpec((TM, LANES), lambda b, h, t: (b * nq + t, h)),
        out_shape=jax.ShapeDtypeStruct((T_S, D), bf16),
        compiler_params=_cparams(("arbitrary", "arbitrary", "arbitrary")),
        name="attn_lat",
    )(q, k, v, ck, cv, lq, g)


def _log_decay(lr, w_ref, b_ref):
    z = _dot3(lr, w_ref[...]) + b_ref[...]
    return -(jnp.maximum(-z, 0.0) + jnp.log(1.0 + jnp.exp(-jnp.abs(z)))) / GATE_NORM


def _gla_direction(q_ref, k_ref, v3b, g, st0, reverse, b_ref, att_ref, sall_ref):
    tb = g.shape[0]
    n = tb // CHUNK
    scale = DK_B ** -0.5
    ri = lax.broadcasted_iota(i32, (CHUNK, CHUNK), 0)
    ci = lax.broadcasted_iota(i32, (CHUNK, CHUNK), 1)
    keep = (ci >= ri) if reverse else (ci <= ri)
    tri = jnp.broadcast_to(jnp.where(keep, 1.0, 0.0).astype(bf16)[None], (n, CHUNK, CHUNK))

    g3 = g.reshape(n, CHUNK, DK_B)
    ghi, glo = _split(g3)
    bdot = lambda a, b: lax.dot_general(a, b, (((2,), (1,)), ((0,), (0,))), preferred_element_type=f32)
    b = bdot(tri, ghi) + bdot(tri, glo)
    tot = b[:, 0:1, :] if reverse else b[:, CHUNK - 1:CHUNK, :]
    q3 = q_ref[...].reshape(n, CHUNK, DK_B) * scale
    k3 = k_ref[...].reshape(n, CHUNK, DK_B)

    qs = (q3 * jnp.exp(b)).astype(bf16)
    kl = (k3 * jnp.exp(tot - b)).astype(bf16)
    kn = (k3 * jnp.exp(jnp.minimum(-b, 80.0))).astype(bf16)
    att = lax.dot_general(qs, kn, (((2,), (2,)), ((0,), (0,))), preferred_element_type=f32)
    att_ref[...] = jnp.where(keep[None], att, 0.0)

    b_ref[...] = b
    need_fix = jnp.min(tot) < FAST_DECAY_LIMIT

    @pl.when(need_fix)
    def _():
        lane = lax.broadcasted_iota(i32, (CHUNK, CHUNK), 1)

        def chunk_body(c, carry):
            bc = b_ref[c]
            row0 = pl.multiple_of(c * CHUNK, CHUNK)
            tot_c = bc[0:1] if reverse else bc[CHUNK - 1:CHUNK]

            @pl.when(jnp.min(tot_c) < FAST_DECAY_LIMIT)
            def _():
                qc = q_ref[pl.ds(row0, CHUNK), :] * scale

                def jbody(j, acc):
                    bj = b_ref[c, pl.ds(j, 1), :]
                    kj = k_ref[pl.ds(row0 + j, 1), :]
                    e = jnp.exp(jnp.minimum(bc - bj, 0.0))
                    col = jnp.sum(qc * e * kj, axis=-1, keepdims=True)
                    return acc + jnp.where(lane == j, col, 0.0)

                acc = lax.fori_loop(0, CHUNK, jbody, jnp.zeros((CHUNK, CHUNK), f32))
                att_ref[c] = jnp.where(keep, acc, 0.0)

            return carry

        lax.fori_loop(0, n, chunk_body, 0)

    u = lax.dot_general(v3b, kl, (((1,), (1,)), ((0,), (0,))), preferred_element_type=f32)
    dec = jnp.exp(tot)
    st = st0
    order = range(n - 1, -1, -1) if reverse else range(n)
    for c in order:
        sall_ref[c] = st
        st = st * dec[c] + u[c]
    inter = lax.dot_general(qs, sall_ref[...].astype(bf16), (((2,), (2,)), ((0,), (0,))),
                            preferred_element_type=f32)
    intra = lax.dot_general(att_ref[...].astype(bf16), v3b, (((2,), (1,)), ((0,), (0,))),
                            preferred_element_type=f32)
    return (inter + intra).reshape(tb, DV_B), st


def _gla_kernel(*refs, has_init):
    if has_init:
        (q_ref, k_ref, v_ref, bg_ref, lr_ref, wf_ref, wb_ref, bf_ref, bb_ref, gn_ref, sf0_ref, sb0_ref,
         o_ref, b_ref, att_ref, sall_ref) = refs
        stf0 = sf0_ref[...].T
        stb0 = sb0_ref[...].T
    else:
        (q_ref, k_ref, v_ref, bg_ref, lr_ref, wf_ref, wb_ref, bf_ref, bb_ref, gn_ref,
         o_ref, sf_ref, sb_ref, b_ref, att_ref, sall_ref) = refs
        stf0 = jnp.zeros((DV_B, DK_B), f32)
        stb0 = stf0
    tb = q_ref.shape[0]
    n = tb // CHUNK
    lr = lr_ref[...]
    v3b = v_ref[...].reshape(n, CHUNK, DV_B).astype(bf16)
    gf = _log_decay(lr, wf_ref, bf_ref)
    of, stf = _gla_direction(q_ref, k_ref, v3b, gf, stf0, False, b_ref, att_ref, sall_ref)
    gb = _log_decay(lr, wb_ref, bb_ref)
    ob, stb = _gla_direction(q_ref, k_ref, v3b, gb, stb0, True, b_ref, att_ref, sall_ref)
    o = of + ob
    o = o * lax.rsqrt(jnp.mean(o * o, axis=-1, keepdims=True) + EPS) * gn_ref[...]
    o_ref[...] = (o * _silu(bg_ref[...])).astype(bf16)
    if not has_init:
        sf_ref[...] = stf.T
        sb_ref[...] = stb.T


def _gla(bq, bk, bv, bg, lr, wf, wb, bfv, bbv, gn, tb, row_blk0, nb, states=None, layer=0):
    has_init = states is not None
    n = tb // CHUNK
    rows = lambda w: pl.BlockSpec((tb, w), lambda b, h: (row_blk0 + b, h))
    lrs = pl.BlockSpec((tb, LANES), lambda b, h: (row_blk0 + b, 0))
    wsp = pl.BlockSpec((LANES, DK_B), lambda b, h: (0, h))
    bsp = pl.BlockSpec((1, DK_B), lambda b, h: (0, h))
    gsp = pl.BlockSpec((1, DV_B), lambda b, h: (0, 0))
    in_specs = [rows(DK_B), rows(DK_B), rows(DV_B), rows(DV_B), lrs, wsp, wsp, bsp, bsp, gsp]
    args = [bq, bk, bv, bg, lr, wf, wb, bfv, bbv, gn]
    osp = pl.BlockSpec((tb, DV_B), lambda b, h: (b, h))
    oshape = jax.ShapeDtypeStruct((nb * tb, D), bf16)
    if has_init:
        ssp = pl.BlockSpec((None, None, None, DK_B, DV_B), lambda b, h: (b, layer, h, 0, 0))
        in_specs += [ssp, ssp]
        args += list(states)
        out_specs, out_shape = osp, oshape
    else:
        ssp = pl.BlockSpec((None, None, DK_B, DV_B), lambda b, h: (b, h, 0, 0))
        sshape = jax.ShapeDtypeStruct((nb, N_HEADS_B, DK_B, DV_B), f32)
        out_specs, out_shape = [osp, ssp, ssp], [oshape, sshape, sshape]
    return pl.pallas_call(
        functools.partial(_gla_kernel, has_init=has_init),
        grid=(nb, N_HEADS_B),
        in_specs=in_specs,
        out_specs=out_specs,
        out_shape=out_shape,
        scratch_shapes=[pltpu.VMEM((n, CHUNK, DK_B), f32), pltpu.VMEM((n, CHUNK, CHUNK), f32),
                        pltpu.VMEM((n, DV_B, DK_B), f32)],
        compiler_params=_cparams(("arbitrary", "arbitrary")),
        name="gla_lat" if has_init else "gla_ctx",
    )(*args)


def _merge_kernel(x_ref, oap_ref, oas_ref, ogp_ref, ogs_ref, ga_ref, gb_ref, mod_ref, wa_ref, wb_ref,
                  wo_ref, g_ref, rw_ref, rb_ref,
                  x1_ref, ht_ref, idx_ref, gate_ref, rank_ref, cnt_ref, base_ref):
    i = pl.program_id(0)

    @pl.when((i == 0) | (i == P_TILES))
    def _():
        base_ref[...] = jnp.zeros_like(base_ref)

    is_ctx = i < P_TILES
    oa = jnp.where(is_ctx, oap_ref[...], oas_ref[...])
    og = jnp.where(is_ctx, ogp_ref[...], ogs_ref[...])
    ya = _dot(oa, wa_ref[...])
    yb = _dot(og, wb_ref[...])
    merged = jax.nn.sigmoid(ga_ref[...]) * ya + jax.nn.sigmoid(gb_ref[...]) * yb
    mix = _dot(merged.astype(bf16), wo_ref[...])
    m = mod_ref[pl.ds(_mod_row(i), 1), :]
    x1 = x_ref[...] + m[:, 2 * D:3 * D] * mix
    x1_ref[...] = x1
    h = x1 * lax.rsqrt(jnp.mean(x1 * x1, axis=-1, keepdims=True) + EPS) * g_ref[...]
    h = h * (1.0 + m[:, 4 * D:5 * D]) + m[:, 3 * D:4 * D]
    for s in range(SUB):
        ht_ref[pl.ds(s, TM, stride=SUB), :] = h[:, s * LANES:(s + 1) * LANES]

    lg = _dot3_nt(rw_ref[...], h) + rb_ref[...][:, 0:1]
    eid = lax.broadcasted_iota(i32, (N_EXPERTS, TM), 0)
    vals, sels, idxs = [], [], []
    for _ in range(TOP_K):
        mx = jnp.max(lg, axis=0, keepdims=True)
        ik = jnp.min(jnp.where(lg == mx, eid, N_EXPERTS), axis=0, keepdims=True)
        sel = eid == ik
        vals.append(mx)
        idxs.append(ik)
        sels.append(sel)
        lg = jnp.where(sel, -jnp.inf, lg)
    es = [jnp.exp(v - vals[0]) for v in vals]
    den = es[0] + es[1] + es[2] + es[3]
    onehot = jnp.where(sels[0] | sels[1] | sels[2] | sels[3], 1.0, 0.0)
    tr = lax.broadcasted_iota(i32, (TM, TM), 0)
    tc = lax.broadcasted_iota(i32, (TM, TM), 1)
    before = jnp.where(tr < tc, 1.0, 0.0).astype(bf16)
    pos = base_ref[:, 0:1] + _dot(onehot.astype(bf16), before)
    zrow_i = jnp.zeros((8 - TOP_K, TM), i32)
    ranks = [jnp.sum(jnp.where(s, pos, 0.0), axis=0, keepdims=True).astype(i32) for s in sels]
    idx_ref[...] = jnp.concatenate(idxs + [zrow_i], axis=0)
    rank_ref[...] = jnp.concatenate(ranks + [zrow_i], axis=0)
    gate_ref[...] = jnp.concatenate([e / den for e in es] + [jnp.zeros((8 - TOP_K, TM), f32)], axis=0)
    base = base_ref[...] + jnp.sum(onehot, axis=1, keepdims=True)
    base_ref[...] = base
    cnt_ref[...] = base.astype(i32)


def _merge(x, oap, oas, ogp, ogs, ga, gb, mod, wa, wb, wo, g, rw, rb):
    tile = lambda w: pl.BlockSpec((TM, w), lambda i: (i, 0))
    ctx = pl.BlockSpec((TM, D), lambda i: (jnp.minimum(i, P_TILES - 1), 0))
    lat = pl.BlockSpec((TM, D), lambda i: (jnp.maximum(i - P_TILES, 0), 0))
    res = lambda shape: pl.BlockSpec(shape, lambda i: (0, 0), pipeline_mode=pl.Buffered(1))
    meta = pl.BlockSpec((8, TM), lambda i: (0, i))
    return pl.pallas_call(
        _merge_kernel,
        grid=(N_TILES,),
        in_specs=[tile(D), ctx, lat, ctx, lat, tile(D), tile(D), res((8, 6 * D)), res((D, D)), res((D, D)),
                  res((D, D)), res((1, D)), res((N_EXPERTS, D)), res((N_EXPERTS, LANES))],
        out_specs=[tile(D), pl.BlockSpec((TM * SUB, LANES), lambda i: (i, 0)), meta, meta, meta,
                   pl.BlockSpec((N_EXPERTS, LANES), lambda i: (jnp.where(i < P_TILES, 0, 1), 0))],
        out_shape=[jax.ShapeDtypeStruct((T, D), f32), jax.ShapeDtypeStruct((T * SUB, LANES), f32),
                   jax.ShapeDtypeStruct((8, T), i32), jax.ShapeDtypeStruct((8, T), f32),
                   jax.ShapeDtypeStruct((8, T), i32), jax.ShapeDtypeStruct((2 * N_EXPERTS, LANES), i32)],
        scratch_shapes=[pltpu.VMEM((N_EXPERTS, LANES), f32)],
        compiler_params=_cparams(("arbitrary",)),
        name="merge_route",
    )(x, oap, oas, ogp, ogs, ga, gb, mod, wa, wb, wo, g, rw, rb)


def _experts_kernel(blk_e_ref, nxt_e_ref, nused_ref, src_ref, meta_ref, h_hbm, wgu_hbm, bgu_ref, wdn_hbm,
                    bdn_ref, out_hbm, hv, acc, wgu_st, wdn_st, wgu_bf, wdn_bf, xg, yb, wsem, iosem, *,
                    layer, tok0):
    i = pl.program_id(0)
    n_used = nused_ref[0]

    @pl.when(i < n_used)
    def _():
        _experts_step(i, n_used, blk_e_ref, nxt_e_ref, src_ref, meta_ref, h_hbm, wgu_hbm, bgu_ref, wdn_hbm,
                      bdn_ref, out_hbm, hv, acc, wgu_st, wdn_st, wgu_bf, wdn_bf, xg, yb, wsem, iosem, layer,
                      tok0)


def _experts_step(i, n_used, blk_e_ref, nxt_e_ref, src_ref, meta_ref, h_hbm, wgu_hbm, bgu_ref, wdn_hbm,
                  bdn_ref, out_hbm, hv, acc, wgu_st, wdn_st, wgu_bf, wdn_bf, xg, yb, wsem, iosem, layer, tok0):
    t_str = hv.shape[0] // SUB

    def weight_copies(e):
        return (pltpu.make_async_copy(wgu_hbm.at[layer, e], wgu_st, wsem.at[0]),
                pltpu.make_async_copy(wdn_hbm.at[layer, e], wdn_st, wsem.at[1]))

    e = blk_e_ref[i]

    @pl.when(i == 0)
    def _():
        for cp in weight_copies(e):
            cp.start()
        cp = pltpu.make_async_copy(h_hbm.at[pl.ds(tok0 * SUB, t_str * SUB)], hv, iosem.at[0])
        cp.start()
        acc[...] = jnp.zeros_like(acc)
        cp.wait()

    @pl.when((i == 0) | (e != blk_e_ref[jnp.maximum(i - 1, 0)]))
    def _():
        for cp in weight_copies(e):
            cp.wait()
        wgu_bf[...] = wgu_st[...].astype(bf16)
        wdn_bf[...] = wdn_st[...].astype(bf16)
        en = nxt_e_ref[e]

        @pl.when(en >= 0)
        def _():
            for cp in weight_copies(en):
                cp.start()

    for r in range(MOE_ROWS):
        o = pl.multiple_of(jnp.minimum(src_ref[0, 0, r], t_str - 1) * SUB, SUB)
        xg[r * SUB:(r + 1) * SUB, :] = hv[pl.ds(o, SUB), :]

    xb = jnp.concatenate([xg[pl.ds(s, MOE_ROWS, stride=SUB), :] for s in range(SUB)],
                         axis=1).astype(bf16)
    gu = _dot(xb, wgu_bf[...]) + bgu_ref[...]
    gt = jnp.minimum(gu[:, :MOE_FF], SWIGLU_LIMIT)
    up = jnp.clip(gu[:, MOE_FF:], -SWIGLU_LIMIT, SWIGLU_LIMIT)
    act = (up + 1.0) * gt * jax.nn.sigmoid(SWIGLU_ALPHA * gt)
    y = _dot(act.astype(bf16), wdn_bf[...]) + bdn_ref[...]
    ri = lax.broadcasted_iota(i32, (MOE_ROWS, MOE_ROWS), 0)
    ci = lax.broadcasted_iota(i32, (MOE_ROWS, MOE_ROWS), 1)
    diag = ri == ci
    col = lambda v: jnp.sum(jnp.where(diag, jnp.broadcast_to(v, (MOE_ROWS, MOE_ROWS)), 0.0),
                            axis=1, keepdims=True)
    y = jnp.where(col(meta_ref[0, 1:2, :]) > 0.5, y * col(meta_ref[0, 0:1, :]), 0.0)
    for s in range(SUB):
        yb[pl.ds(s, MOE_ROWS, stride=SUB), :] = y[:, s * LANES:(s + 1) * LANES]
    group = 16
    for r0 in range(0, MOE_ROWS, group):
        offs = [pl.multiple_of(src_ref[0, 0, r] * SUB, SUB) for r in range(r0, r0 + group)]
        olds = [acc[pl.ds(o, SUB), :] for o in offs]
        for j, o in enumerate(offs):
            acc[pl.ds(o, SUB), :] = olds[j] + yb[(r0 + j) * SUB:(r0 + j + 1) * SUB, :]

    @pl.when(i == n_used - 1)
    def _():
        cp = pltpu.make_async_copy(acc.at[pl.ds(0, t_str * SUB)], out_hbm, iosem.at[0])
        cp.start()
        cp.wait()


def _experts(blk_e, nxt_e, n_used, row_src, meta, ht, wgu, bgu, wdn, bdn, layer, tok0, t_str):
    n_blocks = row_src.shape[0] // MOE_ROWS
    src3 = row_src.reshape(n_blocks, 1, MOE_ROWS)
    grid_spec = pltpu.PrefetchScalarGridSpec(
        num_scalar_prefetch=3,
        grid=(n_blocks,),
        in_specs=[
            pl.BlockSpec((1, 1, MOE_ROWS), lambda i, be, ne, nu: (i, 0, 0), memory_space=pltpu.SMEM),
            pl.BlockSpec((1, 8, MOE_ROWS), lambda i, be, ne, nu: (i, 0, 0)),
            pl.BlockSpec(memory_space=pl.ANY),
            pl.BlockSpec(memory_space=pl.ANY),
            pl.BlockSpec((None, None, 1, 2 * MOE_FF), lambda i, be, ne, nu: (layer, be[i], 0, 0)),
            pl.BlockSpec(memory_space=pl.ANY),
            pl.BlockSpec((None, None, 1, D), lambda i, be, ne, nu: (layer, be[i], 0, 0)),
        ],
        out_specs=pl.BlockSpec(memory_space=pl.ANY),
        scratch_shapes=[
            pltpu.VMEM((t_str * SUB, LANES), f32),
            pltpu.VMEM(((t_str + 8) * SUB, LANES), f32),
            pltpu.VMEM((D, 2 * MOE_FF), f32),
            pltpu.VMEM((MOE_FF, D), f32),
            pltpu.VMEM((D, 2 * MOE_FF), bf16),
            pltpu.VMEM((MOE_FF, D), bf16),
            pltpu.VMEM((MOE_ROWS * SUB, LANES), f32),
            pltpu.VMEM((MOE_ROWS * SUB, LANES), f32),
            pltpu.SemaphoreType.DMA((2,)),
            pltpu.SemaphoreType.DMA((1,)),
        ],
    )
    return pl.pallas_call(
        functools.partial(_experts_kernel, layer=layer, tok0=tok0),
        grid_spec=grid_spec,
        out_shape=jax.ShapeDtypeStruct((t_str * SUB, LANES), f32),
        compiler_params=_cparams(("arbitrary",), _EXPERTS_VMEM_LIMIT),
        name="experts",
    )(blk_e, nxt_e, n_used, src3, meta, ht, wgu, bgu.reshape(DEPTH, N_EXPERTS, 1, 2 * MOE_FF),
      wdn, bdn.reshape(DEPTH, N_EXPERTS, 1, D))


def _combine_kernel(x1_ref, moep_ref, moes_ref, mod_ref, fg_ref, x2_ref, yn_ref):
    i = pl.program_id(0)
    rows = lambda ref: jnp.concatenate([ref[pl.ds(s, TM, stride=SUB), :] for s in range(SUB)], axis=1)
    moe = jnp.where(i < P_TILES, rows(moep_ref), rows(moes_ref))
    m = mod_ref[pl.ds(_mod_row(i), 1), :]
    x2 = x1_ref[...] + m[:, 5 * D:6 * D] * moe
    x2_ref[...] = x2
    yn_ref[...] = x2 * lax.rsqrt(jnp.mean(x2 * x2, axis=-1, keepdims=True) + EPS) * fg_ref[...]


def _combine(x1, moep, moes, mod, fg):
    tile = pl.BlockSpec((TM, D), lambda i: (i, 0))
    res = lambda shape: pl.BlockSpec(shape, lambda i: (0, 0), pipeline_mode=pl.Buffered(1))
    ctx = pl.BlockSpec((TM * SUB, LANES), lambda i: (jnp.minimum(i, P_TILES - 1), 0))
    lat = pl.BlockSpec((TM * SUB, LANES), lambda i: (jnp.maximum(i - P_TILES, 0), 0))
    return pl.pallas_call(
        _combine_kernel,
        grid=(N_TILES,),
        in_specs=[tile, ctx, lat, res((8, 6 * D)), res((1, D))],
        out_specs=[tile, tile],
        out_shape=[jax.ShapeDtypeStruct((T, D), f32), jax.ShapeDtypeStruct((T, D), f32)],
        compiler_params=_cparams(("arbitrary",)),
        name="combine",
    )(x1, moep, moes, mod, fg)


def _rope_tables():
    nf = HD_A // 4
    inv = ROPE_BASE ** (-jnp.arange(nf, dtype=f32) / nf)
    n = jnp.arange(DEC_SEQ)
    row = (n // GRID_W).astype(f32)[:, None] * inv
    col = (n % GRID_W).astype(f32)[:, None] * inv
    cos64 = jnp.concatenate([jnp.cos(row), jnp.cos(row), jnp.cos(col), jnp.cos(col)], axis=-1)
    sin64 = jnp.concatenate([-jnp.sin(row), jnp.sin(row), -jnp.sin(col), jnp.sin(col)], axis=-1)
    cos_s = jnp.tile(jnp.tile(cos64, (1, 2)), (DEC_BATCH, 1))
    sin_s = jnp.tile(jnp.tile(sin64, (1, 2)), (DEC_BATCH, 1))
    cos = jnp.concatenate([jnp.ones((T_P, LANES), f32), cos_s], axis=0)
    sin = jnp.concatenate([jnp.zeros((T_P, LANES), f32), sin_s], axis=0)
    return cos, sin


def _routing_tables(idx, rank, gate, counts):
    t_str = idx.shape[1]
    n_blocks = (t_str * TOP_K + N_EXPERTS * (MOE_ROWS - 1)) // MOE_ROWS + 1
    padded = (counts + MOE_ROWS - 1) // MOE_ROWS * MOE_ROWS
    pad_end = jnp.cumsum(padded)
    pad_start = pad_end - padded
    eids = jnp.arange(N_EXPERTS, dtype=i32)
    start_of = jnp.sum(jnp.where(idx[..., None] == eids, pad_start, 0), axis=-1)
    dest = (start_of + rank).reshape(-1)
    inv = jnp.full((n_blocks * MOE_ROWS,), -1, i32).at[dest].set(jnp.arange(TOP_K * t_str, dtype=i32))
    valid = inv >= 0
    safe = jnp.maximum(inv, 0)
    row_src = jnp.where(valid, safe % t_str, t_str).astype(i32)
    row_gate = jnp.where(valid, gate.reshape(-1)[safe], 0.0)
    meta = jnp.zeros((n_blocks, 8, MOE_ROWS), f32)
    meta = meta.at[:, 0].set(row_gate.reshape(n_blocks, MOE_ROWS))
    meta = meta.at[:, 1].set(valid.astype(f32).reshape(n_blocks, MOE_ROWS))
    starts = jnp.arange(n_blocks, dtype=i32) * MOE_ROWS
    blk_e = jnp.sum((pad_end[None, :] <= starts[:, None]).astype(i32), axis=1)
    present = counts > 0
    blk_e = jnp.minimum(blk_e, jnp.max(jnp.where(present, eids, 0))).astype(i32)
    later = (eids[None, :] > eids[:, None]) & present[None, :]
    nxt = jnp.min(jnp.where(later, eids[None, :], N_EXPERTS), axis=1)
    nxt_e = jnp.where(nxt < N_EXPERTS, nxt, -1).astype(i32)
    n_used = (pad_end[-1:] // MOE_ROWS).astype(i32)
    return blk_e, nxt_e, n_used, row_src, meta


def kernel(x_prompt, x_sample, cache_k, cache_v, state_fwd, state_bwd, c, c_ctx, ada_w, ada_b, norm_mix_g,
           norm_ffn_g, w_in, diff_lambda, diff_subln_g, gla_gk_w, gla_gk_b, gla_norm_g, w_branch_a,
           w_branch_b, w_out, router_w, router_b, expert_w_gu, expert_b_gu, expert_w_down, expert_b_down,
           final_norm_g):
    x = jnp.concatenate([x_prompt.reshape(T_P, D), x_sample.reshape(T_S, D)], axis=0)
    cvec = jnp.concatenate([c_ctx[None], c, jnp.zeros((3, D), f32)], axis=0)
    mods = _ada_mod(cvec, ada_w, ada_b)
    cos, sin = _rope_tables()
    ck = cache_k.reshape(DEC_BATCH, DEPTH, PAST_LEN, D)
    cv = cache_v.reshape(DEC_BATCH, DEPTH, PAST_LEN, D)
    main_w = 6 * D
    lr0 = main_w
    g0 = main_w + 2 * GK_RANK

    ks, vs, sfs, sbs = [], [], [], []
    yn = None
    for l in range(DEPTH):
        lam_init = 0.8 - 0.6 * math.exp(-0.3 * l)
        mod = mods[l]
        wm = w_in[l, :, :main_w].astype(bf16)
        wl = jnp.pad(w_in[l, :, lr0:g0], ((0, 0), (0, LANES - 2 * GK_RANK))).astype(bf16)
        wg = w_in[l, :, g0:].astype(bf16)
        q, k, v, bq, bk, bv, bg, lr, ga, gb = _inproj(x, mod, norm_mix_g[l][None], wm, wg, wl, cos, sin)
        ks.append(k[:T_P].reshape(BATCH, SEQ, D))
        vs.append(v[:T_P].reshape(BATCH, SEQ, D))

        lq = jnp.pad(diff_lambda[l], ((0, 4), (0, LANES - HD_A)))
        sg = diff_subln_g[l][None]
        oap = _attn_ctx(q, k, v, lq, sg, lam_init)
        oas = _attn_lat(q, k, v, ck, cv, lq, sg, lam_init, l)

        wf = jnp.pad(gla_gk_w[l, 0], ((0, LANES - GK_RANK), (0, 0)))
        wb = jnp.pad(gla_gk_w[l, 1], ((GK_RANK, LANES - 2 * GK_RANK), (0, 0)))
        bfv = gla_gk_b[l, 0][None]
        bbv = gla_gk_b[l, 1][None]
        gn = gla_norm_g[l][None]
        ogp, sf, sb = _gla(bq, bk, bv, bg, lr, wf, wb, bfv, bbv, gn, SEQ, 0, BATCH)
        ogs = _gla(bq, bk, bv, bg, lr, wf, wb, bfv, bbv, gn, DEC_SEQ, T_P // DEC_SEQ, DEC_BATCH,
                   states=(state_fwd, state_bwd), layer=l)
        sfs.append(sf)
        sbs.append(sb)

        rb = jnp.broadcast_to(router_b[l][:, None], (N_EXPERTS, LANES))
        x1, ht, idx8, gate8, rank8, cnt = _merge(
            x, oap, oas, ogp, ogs, ga, gb, mod, w_branch_a[l].astype(bf16), w_branch_b[l].astype(bf16),
            w_out[l].astype(bf16), norm_ffn_g[l][None], router_w[l].T, rb)
        moes = []
        for tok0, t_str, cnt_s in ((0, T_P, cnt[:N_EXPERTS, 0]), (T_P, T_S, cnt[N_EXPERTS:, 0])):
            sl = slice(tok0, tok0 + t_str)
            tables = _routing_tables(idx8[:TOP_K, sl], rank8[:TOP_K, sl], gate8[:TOP_K, sl], cnt_s)
            moes.append(_experts(*tables, ht, expert_w_gu, expert_b_gu, expert_w_down, expert_b_down, l,
                                 tok0, t_str))
        x, yn = _combine(x1, moes[0], moes[1], mod, final_norm_g[None])

    y_prompt = yn[:T_P].reshape(BATCH, SEQ, D)
    y_sample = yn[T_P:].reshape(DEC_BATCH, DEC_SEQ, D)
    new_k = jnp.stack(ks, axis=1).reshape(BATCH, DEPTH, SEQ, N_HEADS_A, 2, HD_A)
    new_v = jnp.stack(vs, axis=1).reshape(BATCH, DEPTH, SEQ, N_HEADS_A, 2 * HD_A)
    new_sf = jnp.stack(sfs, axis=1)
    new_sb = jnp.stack(sbs, axis=1)
    return (y_prompt, y_sample, new_k, new_v, new_sf, new_sb)
```

```python
import functools
import math

import jax
import jax.numpy as jnp
from jax import lax
from jax.experimental import pallas as pl
from jax.experimental.pallas import tpu as pltpu

f32 = jnp.float32
bf16 = jnp.bfloat16
i32 = jnp.int32

D = 1024
BATCH = 16
SEQ = 256
DEPTH = 2
DEC_BATCH = 4
DEC_SEQ = 1024
PAST_LEN = 512
GRID_W = 64
HD_A = 64
N_HEADS_A = 8
ROPE_BASE = 10000.0
N_HEADS_B = 4
DK_B = 128
DV_B = 256
GK_RANK = 16
GATE_NORM = 16.0
CHUNK = 64
N_EXPERTS = 32
TOP_K = 4
MOE_FF = 1024
SWIGLU_LIMIT = 7.0
SWIGLU_ALPHA = 1.702
EPS = 1e-6

T_P = BATCH * SEQ
T_S = DEC_BATCH * DEC_SEQ
T = T_P + T_S
TM = 256
N_TILES = T // TM
P_TILES = T_P // TM
S_TILES_PER_BATCH = DEC_SEQ // TM

MOE_ROWS = 256
SUB = 8
LANES = 128
FAST_DECAY_LIMIT = -60.0

_VMEM_LIMIT = 56 * 1024 * 1024
_EXPERTS_VMEM_LIMIT = 60 * 1024 * 1024


def _cparams(sem, vmem=_VMEM_LIMIT):
    return pltpu.CompilerParams(dimension_semantics=sem, vmem_limit_bytes=vmem)


def _dot(a, b):
    return jnp.dot(a, b, preferred_element_type=f32)


def _dot_nt(a, b):
    return lax.dot_general(a, b, (((1,), (1,)), ((), ())), preferred_element_type=f32)


def _split(x):
    hi = x.astype(bf16)
    lo = (x - hi.astype(f32)).astype(bf16)
    return hi, lo


def _dot3(a, b):
    ah, al = _split(a)
    bh, bl = _split(b)
    return _dot(ah, bh) + _dot(ah, bl) + _dot(al, bh)


def _dot3_nt(a, b):
    ah, al = _split(a)
    bh, bl = _split(b)
    return _dot_nt(ah, bh) + _dot_nt(ah, bl) + _dot_nt(al, bh)


def _silu(x):
    return x * jax.nn.sigmoid(x)


def _mod_row(i):
    return jnp.where(i < P_TILES, 0, 1 + lax.shift_right_logical(jnp.maximum(i - P_TILES, 0), 2))


def _ada_kernel(c_ref, w_ref, b_ref, o_ref):
    o_ref[0] = _dot3(_silu(c_ref[...]), w_ref[0]) + b_ref[0]


def _ada_mod(cvec, ada_w, ada_b):
    tn = 1536
    return pl.pallas_call(
        _ada_kernel,
        grid=(DEPTH, 6 * D // tn),
        in_specs=[
            pl.BlockSpec((8, D), lambda l, j: (0, 0)),
            pl.BlockSpec((1, D, tn), lambda l, j: (l, 0, j)),
            pl.BlockSpec((1, 1, tn), lambda l, j: (l, 0, j)),
        ],
        out_specs=pl.BlockSpec((1, 8, tn), lambda l, j: (l, 0, j)),
        out_shape=jax.ShapeDtypeStruct((DEPTH, 8, 6 * D), f32),
        compiler_params=_cparams(("arbitrary", "arbitrary")),
        name="ada_mod",
    )(cvec, ada_w, ada_b.reshape(DEPTH, 1, 6 * D))


def _inproj_kernel(x_ref, mod_ref, g_ref, wm_ref, wg_ref, wl_ref, cos_ref, sin_ref,
                   q_ref, k_ref, v_ref, bq_ref, bk_ref, bv_ref, bg_ref, lr_ref, ga_ref, gb_ref):
    i = pl.program_id(0)
    m = mod_ref[pl.ds(_mod_row(i), 1), :]
    shift = m[:, 0:D]
    scale = m[:, D:2 * D]
    x = x_ref[...]
    h = x * lax.rsqrt(jnp.mean(x * x, axis=-1, keepdims=True) + EPS) * g_ref[...]
    hb = (h * (1.0 + scale) + shift).astype(bf16)

    cos = cos_ref[...]
    sin = sin_ref[...]
    lane = lax.broadcasted_iota(i32, (TM, LANES), 1)
    lo_mask = (lane & 31) < 16

    def rope(a):
        partner = jnp.where(lo_mask, pltpu.roll(a, LANES - 16, 1), pltpu.roll(a, 16, 1))
        return a * cos + partner * sin

    aq = _dot(hb, wm_ref[:, 0:D])
    for c in range(D // LANES):
        q_ref[:, c * LANES:(c + 1) * LANES] = rope(aq[:, c * LANES:(c + 1) * LANES]).astype(bf16)
    ak = _dot(hb, wm_ref[:, D:2 * D])
    for c in range(D // LANES):
        k_ref[:, c * LANES:(c + 1) * LANES] = rope(ak[:, c * LANES:(c + 1) * LANES])
    v_ref[...] = _dot(hb, wm_ref[:, 2 * D:3 * D])
    bq_ref[...] = _dot(hb, wm_ref[:, 3 * D:3 * D + 512])
    bk_ref[...] = _dot(hb, wm_ref[:, 3 * D + 512:4 * D])
    bv_ref[...] = _dot(hb, wm_ref[:, 4 * D:5 * D])
    bg_ref[...] = _dot(hb, wm_ref[:, 5 * D:6 * D])
    lr_ref[...] = _dot(hb, wl_ref[...])
    ga_ref[...] = _dot(hb, wg_ref[:, 0:D])
    gb_ref[...] = _dot(hb, wg_ref[:, D:2 * D])


def _inproj(x, mod, g, wm, wg, wl, cos, sin):
    tile = lambda w: pl.BlockSpec((TM, w), lambda i: (i, 0))
    res = lambda shape: pl.BlockSpec(shape, lambda i: (0, 0), pipeline_mode=pl.Buffered(1))
    widths = (D, D, D, 512, 512, D, D, LANES, D, D)
    dtypes = (bf16,) + (f32,) * 9
    return pl.pallas_call(
        _inproj_kernel,
        grid=(N_TILES,),
        in_specs=[tile(D), res((8, 6 * D)), res((1, D)), res((D, 6 * D)), res((D, 2 * D)),
                  res((D, LANES)), tile(LANES), tile(LANES)],
        out_specs=[tile(w) for w in widths],
        out_shape=[jax.ShapeDtypeStruct((T, w), dt) for w, dt in zip(widths, dtypes)],
        compiler_params=_cparams(("arbitrary",)),
        name="inproj",
    )(x, mod, g, wm, wg, wl, cos, sin)


def _lam_from(lq_ref, lam_init):
    lq = lq_ref[...]
    s01 = jnp.sum(lq[0:1] * lq[1:2], axis=-1, keepdims=True)
    s23 = jnp.sum(lq[2:3] * lq[3:4], axis=-1, keepdims=True)
    return jnp.exp(s01) - jnp.exp(s23) + lam_init


def _subln(o, g, lam_init):
    o = o * lax.rsqrt(jnp.mean(o * o, axis=-1, keepdims=True) + EPS)
    return o * g * (1.0 - lam_init)


def _attn_ctx_kernel(q_ref, k_ref, v_ref, lq_ref, g_ref, o_ref, *, lam_init):
    q = q_ref[...]
    kb = k_ref[...].astype(bf16)
    vb = v_ref[...].astype(bf16)
    lo = lax.broadcasted_iota(i32, q.shape, 1) < HD_A
    outs = []
    for qm in (jnp.where(lo, q, 0), jnp.where(lo, 0, q)):
        s = _dot_nt(qm, kb) * (HD_A ** -0.5)
        p = jnp.exp(s - jnp.max(s, axis=-1, keepdims=True))
        l = jnp.sum(p, axis=-1, keepdims=True)
        outs.append(_dot(p.astype(bf16), vb) / l)
    o = outs[0] - _lam_from(lq_ref, lam_init) * outs[1]
    o_ref[...] = _subln(o, g_ref[...], lam_init).astype(bf16)


def _attn_lat_kernel(q_ref, k_ref, v_ref, ck_ref, cv_ref, lq_ref, g_ref, o_ref, *, lam_init):
    q = q_ref[...]
    kb = k_ref[...].astype(bf16)
    vb = v_ref[...].astype(bf16)
    ckb = ck_ref[...].astype(bf16)
    cvb = cv_ref[...].astype(bf16)
    lo = lax.broadcasted_iota(i32, q.shape, 1) < HD_A
    outs = []
    for qm in (jnp.where(lo, q, 0), jnp.where(lo, 0, q)):
        sc = _dot_nt(qm, ckb) * (HD_A ** -0.5)
        ss = _dot_nt(qm, kb) * (HD_A ** -0.5)
        mx = jnp.maximum(jnp.max(sc, axis=-1, keepdims=True), jnp.max(ss, axis=-1, keepdims=True))
        pc = jnp.exp(sc - mx)
        ps = jnp.exp(ss - mx)
        l = jnp.sum(pc, axis=-1, keepdims=True) + jnp.sum(ps, axis=-1, keepdims=True)
        outs.append((_dot(pc.astype(bf16), cvb) + _dot(ps.astype(bf16), vb)) / l)
    o = outs[0] - _lam_from(lq_ref, lam_init) * outs[1]
    o_ref[...] = _subln(o, g_ref[...], lam_init).astype(bf16)


def _attn_ctx(q, k, v, lq, g, lam_init):
    blk = pl.BlockSpec((SEQ, LANES), lambda b, h: (b, h))
    small = lambda s: pl.BlockSpec(s, lambda b, h: (0, 0))
    return pl.pallas_call(
        functools.partial(_attn_ctx_kernel, lam_init=lam_init),
        grid=(BATCH, N_HEADS_A),
        in_specs=[blk, blk, blk, small((8, LANES)), small((1, LANES))],
        out_specs=blk,
        out_shape=jax.ShapeDtypeStruct((T_P, D), bf16),
        compiler_params=_cparams(("arbitrary", "arbitrary")),
        name="attn_ctx",
    )(q, k, v, lq, g)


def _attn_lat(q, k, v, ck, cv, lq, g, lam_init, layer):
    nq = DEC_SEQ // TM
    qblk = pl.BlockSpec((TM, LANES), lambda b, h, t: (P_TILES + b * nq + t, h))
    kblk = pl.BlockSpec((DEC_SEQ, LANES), lambda b, h, t: (T_P // DEC_SEQ + b, h))
    cblk = pl.BlockSpec((None, None, PAST_LEN, LANES), lambda b, h, t: (b, layer, 0, h))
    small = lambda s: pl.BlockSpec(s, lambda b, h, t: (0, 0))
    return pl.pallas_call(
        functools.partial(_attn_lat_kernel, lam_init=lam_init),
        grid=(DEC_BATCH, N_HEADS_A, nq),
        in_specs=[qblk, kblk, kblk, cblk, cblk, small((8, LANES)), small((1, LANES))],
        out_specs=pl.BlockSpec((TM, LANES), lambda b, h, t: (b * nq + t, h)),
        out_shape=jax.ShapeDtypeStruct((T_S, D), bf16),
        compiler_params=_cparams(("arbitrary", "arbitrary", "arbitrary")),
        name="attn_lat",
    )(q, k, v, ck, cv, lq, g)


def _log_decay(lr, w_ref, b_ref):
    z = _dot3(lr, w_ref[...]) + b_ref[...]
    return -(jnp.maximum(-z, 0.0) + jnp.log(1.0 + jnp.exp(-jnp.abs(z)))) / GATE_NORM


def _gla_direction(q_ref, k_ref, v3b, g, st0, reverse, b_ref, att_ref, sall_ref):
    tb = g.shape[0]
    n = tb // CHUNK
    scale = DK_B ** -0.5
    ri = lax.broadcasted_iota(i32, (CHUNK, CHUNK), 0)
    ci = lax.broadcasted_iota(i32, (CHUNK, CHUNK), 1)
    keep = (ci >= ri) if reverse else (ci <= ri)
    tri = jnp.broadcast_to(jnp.where(keep, 1.0, 0.0).astype(bf16)[None], (n, CHUNK, CHUNK))

    g3 = g.reshape(n, CHUNK, DK_B)
    ghi, glo = _split(g3)
    bdot = lambda a, b: lax.dot_general(a, b, (((2,), (1,)), ((0,), (0,))), preferred_element_type=f32)
    b = bdot(tri, ghi) + bdot(tri, glo)
    tot = b[:, 0:1, :] if reverse else b[:, CHUNK - 1:CHUNK, :]
    q3 = q_ref[...].reshape(n, CHUNK, DK_B) * scale
    k3 = k_ref[...].reshape(n, CHUNK, DK_B)

    qs = (q3 * jnp.exp(b)).astype(bf16)
    kl = (k3 * jnp.exp(tot - b)).astype(bf16)
    kn = (k3 * jnp.exp(jnp.minimum(-b, 80.0))).astype(bf16)
    att = lax.dot_general(qs, kn, (((2,), (2,)), ((0,), (0,))), preferred_element_type=f32)
    att_ref[...] = jnp.where(keep[None], att, 0.0)

    b_ref[...] = b
    need_fix = jnp.min(tot) < FAST_DECAY_LIMIT

    @pl.when(need_fix)
    def _():
        lane = lax.broadcasted_iota(i32, (CHUNK, CHUNK), 1)

        def chunk_body(c, carry):
            bc = b_ref[c]
            row0 = pl.multiple_of(c * CHUNK, CHUNK)
            tot_c = bc[0:1] if reverse else bc[CHUNK - 1:CHUNK]

            @pl.when(jnp.min(tot_c) < FAST_DECAY_LIMIT)
            def _():
                qc = q_ref[pl.ds(row0, CHUNK), :] * scale

                def jbody(j, acc):
                    bj = b_ref[c, pl.ds(j, 1), :]
                    kj = k_ref[pl.ds(row0 + j, 1), :]
                    e = jnp.exp(jnp.minimum(bc - bj, 0.0))
                    col = jnp.sum(qc * e * kj, axis=-1, keepdims=True)
                    return acc + jnp.where(lane == j, col, 0.0)

                acc = lax.fori_loop(0, CHUNK, jbody, jnp.zeros((CHUNK, CHUNK), f32))
                att_ref[c] = jnp.where(keep, acc, 0.0)

            return carry

        lax.fori_loop(0, n, chunk_body, 0)

    u = lax.dot_general(v3b, kl, (((1,), (1,)), ((0,), (0,))), preferred_element_type=f32)
    dec = jnp.exp(tot)
    st = st0
    order = range(n - 1, -1, -1) if reverse else range(n)
    for c in order:
        sall_ref[c] = st
        st = st * dec[c] + u[c]
    inter = lax.dot_general(qs, sall_ref[...].astype(bf16), (((2,), (2,)), ((0,), (0,))),
                            preferred_element_type=f32)
    intra = lax.dot_general(att_ref[...].astype(bf16), v3b, (((2,), (1,)), ((0,), (0,))),
                            preferred_element_type=f32)
    return (inter + intra).reshape(tb, DV_B), st


def _gla_kernel(*refs, has_init):
    if has_init:
        (q_ref, k_ref, v_ref, bg_ref, lr_ref, wf_ref, wb_ref, bf_ref, bb_ref, gn_ref, sf0_ref, sb0_ref,
         o_ref, b_ref, att_ref, sall_ref) = refs
        stf0 = sf0_ref[...].T
        stb0 = sb0_ref[...].T
    else:
        (q_ref, k_ref, v_ref, bg_ref, lr_ref, wf_ref, wb_ref, bf_ref, bb_ref, gn_ref,
         o_ref, sf_ref, sb_ref, b_ref, att_ref, sall_ref) = refs
        stf0 = jnp.zeros((DV_B, DK_B), f32)
        stb0 = stf0
    tb = q_ref.shape[0]
    n = tb // CHUNK
    lr = lr_ref[...]
    v3b = v_ref[...].reshape(n, CHUNK, DV_B).astype(bf16)
    gf = _log_decay(lr, wf_ref, bf_ref)
    of, stf = _gla_direction(q_ref, k_ref, v3b, gf, stf0, False, b_ref, att_ref, sall_ref)
    gb = _log_decay(lr, wb_ref, bb_ref)
    ob, stb = _gla_direction(q_ref, k_ref, v3b, gb, stb0, True, b_ref, att_ref, sall_ref)
    o = of + ob
    o = o * lax.rsqrt(jnp.mean(o * o, axis=-1, keepdims=True) + EPS) * gn_ref[...]
    o_ref[...] = (o * _silu(bg_ref[...])).astype(bf16)
    if not has_init:
        sf_ref[...] = stf.T
        sb_ref[...] = stb.T


def _gla(bq, bk, bv, bg, lr, wf, wb, bfv, bbv, gn, tb, row_blk0, nb, states=None, layer=0):
    has_init = states is not None
    n = tb // CHUNK
    rows = lambda w: pl.BlockSpec((tb, w), lambda b, h: (row_blk0 + b, h))
    lrs = pl.BlockSpec((tb, LANES), lambda b, h: (row_blk0 + b, 0))
    wsp = pl.BlockSpec((LANES, DK_B), lambda b, h: (0, h))
    bsp = pl.BlockSpec((1, DK_B), lambda b, h: (0, h))
    gsp = pl.BlockSpec((1, DV_B), lambda b, h: (0, 0))
    in_specs = [rows(DK_B), rows(DK_B), rows(DV_B), rows(DV_B), lrs, wsp, wsp, bsp, bsp, gsp]
    args = [bq, bk, bv, bg, lr, wf, wb, bfv, bbv, gn]
    osp = pl.BlockSpec((tb, DV_B), lambda b, h: (b, h))
    oshape = jax.ShapeDtypeStruct((nb * tb, D), bf16)
    if has_init:
        ssp = pl.BlockSpec((None, None, None, DK_B, DV_B), lambda b, h: (b, layer, h, 0, 0))
        in_specs += [ssp, ssp]
        args += list(states)
        out_specs, out_shape = osp, oshape
    else:
        ssp = pl.BlockSpec((None, None, DK_B, DV_B), lambda b, h: (b, h, 0, 0))
        sshape = jax.ShapeDtypeStruct((nb, N_HEADS_B, DK_B, DV_B), f32)
        out_specs, out_shape = [osp, ssp, ssp], [oshape, sshape, sshape]
    return pl.pallas_call(
        functools.partial(_gla_kernel, has_init=has_init),
        grid=(nb, N_HEADS_B),
        in_specs=in_specs,
        out_specs=out_specs,
        out_shape=out_shape,
        scratch_shapes=[pltpu.VMEM((n, CHUNK, DK_B), f32), pltpu.VMEM((n, CHUNK, CHUNK), f32),
                        pltpu.VMEM((n, DV_B, DK_B), f32)],
        compiler_params=_cparams(("arbitrary", "arbitrary")),
        name="gla_lat" if has_init else "gla_ctx",
    )(*args)


def _merge_kernel(x_ref, oap_ref, oas_ref, ogp_ref, ogs_ref, ga_ref, gb_ref, mod_ref, wa_ref, wb_ref,
                  wo_ref, g_ref, rw_ref, rb_ref,
                  x1_ref, ht_ref, idx_ref, gate_ref, rank_ref, cnt_ref, base_ref):
    i = pl.program_id(0)

    @pl.when((i == 0) | (i == P_TILES))
    def _():
        base_ref[...] = jnp.zeros_like(base_ref)

    is_ctx = i < P_TILES
    oa = jnp.where(is_ctx, oap_ref[...], oas_ref[...])
    og = jnp.where(is_ctx, ogp_ref[...], ogs_ref[...])
    ya = _dot(oa, wa_ref[...])
    yb = _dot(og, wb_ref[...])
    merged = jax.nn.sigmoid(ga_ref[...]) * ya + jax.nn.sigmoid(gb_ref[...]) * yb
    mix = _dot(merged.astype(bf16), wo_ref[...])
    m = mod_ref[pl.ds(_mod_row(i), 1), :]
    x1 = x_ref[...] + m[:, 2 * D:3 * D] * mix
    x1_ref[...] = x1
    h = x1 * lax.rsqrt(jnp.mean(x1 * x1, axis=-1, keepdims=True) + EPS) * g_ref[...]
    h = h * (1.0 + m[:, 4 * D:5 * D]) + m[:, 3 * D:4 * D]
    for s in range(SUB):
        ht_ref[pl.ds(s, TM, stride=SUB), :] = h[:, s * LANES:(s + 1) * LANES]

    lg = _dot3_nt(rw_ref[...], h) + rb_ref[...][:, 0:1]
    eid = lax.broadcasted_iota(i32, (N_EXPERTS, TM), 0)
    vals, sels, idxs = [], [], []
    for _ in range(TOP_K):
        mx = jnp.max(lg, axis=0, keepdims=True)
        ik = jnp.min(jnp.where(lg == mx, eid, N_EXPERTS), axis=0, keepdims=True)
        sel = eid == ik
        vals.append(mx)
        idxs.append(ik)
        sels.append(sel)
        lg = jnp.where(sel, -jnp.inf, lg)
    es = [jnp.exp(v - vals[0]) for v in vals]
    den = es[0] + es[1] + es[2] + es[3]
    onehot = jnp.where(sels[0] | sels[1] | sels[2] | sels[3], 1.0, 0.0)
    tr = lax.broadcasted_iota(i32, (TM, TM), 0)
    tc = lax.broadcasted_iota(i32, (TM, TM), 1)
    before = jnp.where(tr < tc, 1.0, 0.0).astype(bf16)
    pos = base_ref[:, 0:1] + _dot(onehot.astype(bf16), before)
    zrow_i = jnp.zeros((8 - TOP_K, TM), i32)
    ranks = [jnp.sum(jnp.where(s, pos, 0.0), axis=0, keepdims=True).astype(i32) for s in sels]
    idx_ref[...] = jnp.concatenate(idxs + [zrow_i], axis=0)
    rank_ref[...] = jnp.concatenate(ranks + [zrow_i], axis=0)
    gate_ref[...] = jnp.concatenate([e / den for e in es] + [jnp.zeros((8 - TOP_K, TM), f32)], axis=0)
    base = base_ref[...] + jnp.sum(onehot, axis=1, keepdims=True)
    base_ref[...] = base
    cnt_ref[...] = base.astype(i32)


def _merge(x, oap, oas, ogp, ogs, ga, gb, mod, wa, wb, wo, g, rw, rb):
    tile = lambda w: pl.BlockSpec((TM, w), lambda i: (i, 0))
    ctx = pl.BlockSpec((TM, D), lambda i: (jnp.minimum(i, P_TILES - 1), 0))
    lat = pl.BlockSpec((TM, D), lambda i: (jnp.maximum(i - P_TILES, 0), 0))
    res = lambda shape: pl.BlockSpec(shape, lambda i: (0, 0), pipeline_mode=pl.Buffered(1))
    meta = pl.BlockSpec((8, TM), lambda i: (0, i))
    return pl.pallas_call(
        _merge_kernel,
        grid=(N_TILES,),
        in_specs=[tile(D), ctx, lat, ctx, lat, tile(D), tile(D), res((8, 6 * D)), res((D, D)), res((D, D)),
                  res((D, D)), res((1, D)), res((N_EXPERTS, D)), res((N_EXPERTS, LANES))],
        out_specs=[tile(D), pl.BlockSpec((TM * SUB, LANES), lambda i: (i, 0)), meta, meta, meta,
                   pl.BlockSpec((N_EXPERTS, LANES), lambda i: (jnp.where(i < P_TILES, 0, 1), 0))],
        out_shape=[jax.ShapeDtypeStruct((T, D), f32), jax.ShapeDtypeStruct((T * SUB, LANES), f32),
                   jax.ShapeDtypeStruct((8, T), i32), jax.ShapeDtypeStruct((8, T), f32),
                   jax.ShapeDtypeStruct((8, T), i32), jax.ShapeDtypeStruct((2 * N_EXPERTS, LANES), i32)],
        scratch_shapes=[pltpu.VMEM((N_EXPERTS, LANES), f32)],
        compiler_params=_cparams(("arbitrary",)),
        name="merge_route",
    )(x, oap, oas, ogp, ogs, ga, gb, mod, wa, wb, wo, g, rw, rb)


def _experts_kernel(blk_e_ref, nxt_e_ref, nused_ref, src_ref, meta_ref, h_hbm, wgu_hbm, bgu_ref, wdn_hbm,
                    bdn_ref, out_hbm, hv, acc, wgu_st, wdn_st, wgu_bf, wdn_bf, xg, yb, wsem, iosem, *,
                    layer, tok0):
    i = pl.program_id(0)
    n_used = nused_ref[0]

    @pl.when(i < n_used)
    def _():
        _experts_step(i, n_used, blk_e_ref, nxt_e_ref, src_ref, meta_ref, h_hbm, wgu_hbm, bgu_ref, wdn_hbm,
                      bdn_ref, out_hbm, hv, acc, wgu_st, wdn_st, wgu_bf, wdn_bf, xg, yb, wsem, iosem, layer,
                      tok0)


def _experts_step(i, n_used, blk_e_ref, nxt_e_ref, src_ref, meta_ref, h_hbm, wgu_hbm, bgu_ref, wdn_hbm,
                  bdn_ref, out_hbm, hv, acc, wgu_st, wdn_st, wgu_bf, wdn_bf, xg, yb, wsem, iosem, layer, tok0):
    t_str = hv.shape[0] // SUB

    def weight_copies(e):
        return (pltpu.make_async_copy(wgu_hbm.at[layer, e], wgu_st, wsem.at[0]),
                pltpu.make_async_copy(wdn_hbm.at[layer, e], wdn_st, wsem.at[1]))

    e = blk_e_ref[i]

    @pl.when(i == 0)
    def _():
        for cp in weight_copies(e):
            cp.start()
        cp = pltpu.make_async_copy(h_hbm.at[pl.ds(tok0 * SUB, t_str * SUB)], hv, iosem.at[0])
        cp.start()
        acc[...] = jnp.zeros_like(acc)
        cp.wait()

    @pl.when((i == 0) | (e != blk_e_ref[jnp.maximum(i - 1, 0)]))
    def _():
        for cp in weight_copies(e):
            cp.wait()
        wgu_bf[...] = wgu_st[...].astype(bf16)
        wdn_bf[...] = wdn_st[...].astype(bf16)
        en = nxt_e_ref[e]

        @pl.when(en >= 0)
        def _():
            for cp in weight_copies(en):
                cp.start()

    for r in range(MOE_ROWS):
        o = pl.multiple_of(jnp.minimum(src_ref[0, 0, r], t_str - 1) * SUB, SUB)
        xg[r * SUB:(r + 1) * SUB, :] = hv[pl.ds(o, SUB), :]

    xb = jnp.concatenate([xg[pl.ds(s, MOE_ROWS, stride=SUB), :] for s in range(SUB)],
                         axis=1).astype(bf16)
    gu = _dot(xb, wgu_bf[...]) + bgu_ref[...]
    gt = jnp.minimum(gu[:, :MOE_FF], SWIGLU_LIMIT)
    up = jnp.clip(gu[:, MOE_FF:], -SWIGLU_LIMIT, SWIGLU_LIMIT)
    act = (up + 1.0) * gt * jax.nn.sigmoid(SWIGLU_ALPHA * gt)
    y = _dot(act.astype(bf16), wdn_bf[...]) + bdn_ref[...]
    ri = lax.broadcasted_iota(i32, (MOE_ROWS, MOE_ROWS), 0)
    ci = lax.broadcasted_iota(i32, (MOE_ROWS, MOE_ROWS), 1)
    diag = ri == ci
    col = lambda v: jnp.sum(jnp.where(diag, jnp.broadcast_to(v, (MOE_ROWS, MOE_ROWS)), 0.0),
                            axis=1, keepdims=True)
    y = jnp.where(col(meta_ref[0, 1:2, :]) > 0.5, y * col(meta_ref[0, 0:1, :]), 0.0)
    for s in range(SUB):
        yb[pl.ds(s, MOE_ROWS, stride=SUB), :] = y[:, s * LANES:(s + 1) * LANES]
    group = 16
    for r0 in range(0, MOE_ROWS, group):
        offs = [pl.multiple_of(src_ref[0, 0, r] * SUB, SUB) for r in range(r0, r0 + group)]
        olds = [acc[pl.ds(o, SUB), :] for o in offs]
        for j, o in enumerate(offs):
            acc[pl.ds(o, SUB), :] = olds[j] + yb[(r0 + j) * SUB:(r0 + j + 1) * SUB, :]

    @pl.when(i == n_used - 1)
    def _():
        cp = pltpu.make_async_copy(acc.at[pl.ds(0, t_str * SUB)], out_hbm, iosem.at[0])
        cp.start()
        cp.wait()


def _experts(blk_e, nxt_e, n_used, row_src, meta, ht, wgu, bgu, wdn, bdn, layer, tok0, t_str):
    n_blocks = row_src.shape[0] // MOE_ROWS
    src3 = row_src.reshape(n_blocks, 1, MOE_ROWS)
    grid_spec = pltpu.PrefetchScalarGridSpec(
        num_scalar_prefetch=3,
        grid=(n_blocks,),
        in_specs=[
            pl.BlockSpec((1, 1, MOE_ROWS), lambda i, be, ne, nu: (i, 0, 0), memory_space=pltpu.SMEM),
            pl.BlockSpec((1, 8, MOE_ROWS), lambda i, be, ne, nu: (i, 0, 0)),
            pl.BlockSpec(memory_space=pl.ANY),
            pl.BlockSpec(memory_space=pl.ANY),
            pl.BlockSpec((None, None, 1, 2 * MOE_FF), lambda i, be, ne, nu: (layer, be[i], 0, 0)),
            pl.BlockSpec(memory_space=pl.ANY),
            pl.BlockSpec((None, None, 1, D), lambda i, be, ne, nu: (layer, be[i], 0, 0)),
        ],
        out_specs=pl.BlockSpec(memory_space=pl.ANY),
        scratch_shapes=[
            pltpu.VMEM((t_str * SUB, LANES), f32),
            pltpu.VMEM(((t_str + 8) * SUB, LANES), f32),
            pltpu.VMEM((D, 2 * MOE_FF), f32),
            pltpu.VMEM((MOE_FF, D), f32),
            pltpu.VMEM((D, 2 * MOE_FF), bf16),
            pltpu.VMEM((MOE_FF, D), bf16),
            pltpu.VMEM((MOE_ROWS * SUB, LANES), f32),
            pltpu.VMEM((MOE_ROWS * SUB, LANES), f32),
            pltpu.SemaphoreType.DMA((2,)),
            pltpu.SemaphoreType.DMA((1,)),
        ],
    )
    return pl.pallas_call(
        functools.partial(_experts_kernel, layer=layer, tok0=tok0),
        grid_spec=grid_spec,
        out_shape=jax.ShapeDtypeStruct((t_str * SUB, LANES), f32),
        compiler_params=_cparams(("arbitrary",), _EXPERTS_VMEM_LIMIT),
        name="experts",
    )(blk_e, nxt_e, n_used, src3, meta, ht, wgu, bgu.reshape(DEPTH, N_EXPERTS, 1, 2 * MOE_FF),
      wdn, bdn.reshape(DEPTH, N_EXPERTS, 1, D))


def _combine_kernel(x1_ref, moep_ref, moes_ref, mod_ref, fg_ref, x2_ref, yn_ref):
    i = pl.program_id(0)
    rows = lambda ref: jnp.concatenate([ref[pl.ds(s, TM, stride=SUB), :] for s in range(SUB)], axis=1)
    moe = jnp.where(i < P_TILES, rows(moep_ref), rows(moes_ref))
    m = mod_ref[pl.ds(_mod_row(i), 1), :]
    x2 = x1_ref[...] + m[:, 5 * D:6 * D] * moe
    x2_ref[...] = x2
    yn_ref[...] = x2 * lax.rsqrt(jnp.mean(x2 * x2, axis=-1, keepdims=True) + EPS) * fg_ref[...]


def _combine(x1, moep, moes, mod, fg):
    tile = pl.BlockSpec((TM, D), lambda i: (i, 0))
    res = lambda shape: pl.BlockSpec(shape, lambda i: (0, 0), pipeline_mode=pl.Buffered(1))
    ctx = pl.BlockSpec((TM * SUB, LANES), lambda i: (jnp.minimum(i, P_TILES - 1), 0))
    lat = pl.BlockSpec((TM * SUB, LANES), lambda i: (jnp.maximum(i - P_TILES, 0), 0))
    return pl.pallas_call(
        _combine_kernel,
        grid=(N_TILES,),
        in_specs=[tile, ctx, lat, res((8, 6 * D)), res((1, D))],
        out_specs=[tile, tile],
        out_shape=[jax.ShapeDtypeStruct((T, D), f32), jax.ShapeDtypeStruct((T, D), f32)],
        compiler_params=_cparams(("arbitrary",)),
        name="combine",
    )(x1, moep, moes, mod, fg)


def _rope_tables():
    nf = HD_A // 4
    inv = ROPE_BASE ** (-jnp.arange(nf, dtype=f32) / nf)
    n = jnp.arange(DEC_SEQ)
    row = (n // GRID_W).astype(f32)[:, None] * inv
    col = (n % GRID_W).astype(f32)[:, None] * inv
    cos64 = jnp.concatenate([jnp.cos(row), jnp.cos(row), jnp.cos(col), jnp.cos(col)], axis=-1)
    sin64 = jnp.concatenate([-jnp.sin(row), jnp.sin(row), -jnp.sin(col), jnp.sin(col)], axis=-1)
    cos_s = jnp.tile(jnp.tile(cos64, (1, 2)), (DEC_BATCH, 1))
    sin_s = jnp.tile(jnp.tile(sin64, (1, 2)), (DEC_BATCH, 1))
    cos = jnp.concatenate([jnp.ones((T_P, LANES), f32), cos_s], axis=0)
    sin = jnp.concatenate([jnp.zeros((T_P, LANES), f32), sin_s], axis=0)
    return cos, sin


def _routing_tables(idx, rank, gate, counts):
    t_str = idx.shape[1]
    n_blocks = (t_str * TOP_K + N_EXPERTS * (MOE_ROWS - 1)) // MOE_ROWS + 1
    padded = (counts + MOE_ROWS - 1) // MOE_ROWS * MOE_ROWS
    pad_end = jnp.cumsum(padded)
    pad_start = pad_end - padded
    eids = jnp.arange(N_EXPERTS, dtype=i32)
    start_of = jnp.sum(jnp.where(idx[..., None] == eids, pad_start, 0), axis=-1)
    dest = (start_of + rank).reshape(-1)
    inv = jnp.full((n_blocks * MOE_ROWS,), -1, i32).at[dest].set(jnp.arange(TOP_K * t_str, dtype=i32))
    valid = inv >= 0
    safe = jnp.maximum(inv, 0)
    row_src = jnp.where(valid, safe % t_str, t_str).astype(i32)
    row_gate = jnp.where(valid, gate.reshape(-1)[safe], 0.0)
    meta = jnp.zeros((n_blocks, 8, MOE_ROWS), f32)
    meta = meta.at[:, 0].set(row_gate.reshape(n_blocks, MOE_ROWS))
    meta = meta.at[:, 1].set(valid.astype(f32).reshape(n_blocks, MOE_ROWS))
    starts = jnp.arange(n_blocks, dtype=i32) * MOE_ROWS
    blk_e = jnp.sum((pad_end[None, :] <= starts[:, None]).astype(i32), axis=1)
    present = counts > 0
    blk_e = jnp.minimum(blk_e, jnp.max(jnp.where(present, eids, 0))).astype(i32)
    later = (eids[None, :] > eids[:, None]) & present[None, :]
    nxt = jnp.min(jnp.where(later, eids[None, :], N_EXPERTS), axis=1)
    nxt_e = jnp.where(nxt < N_EXPERTS, nxt, -1).astype(i32)
    n_used = (pad_end[-1:] // MOE_ROWS).astype(i32)
    return blk_e, nxt_e, n_used, row_src, meta


def kernel(x_prompt, x_sample, cache_k, cache_v, state_fwd, state_bwd, c, c_ctx, ada_w, ada_b, norm_mix_g,
           norm_ffn_g, w_in, diff_lambda, diff_subln_g, gla_gk_w, gla_gk_b, gla_norm_g, w_branch_a,
           w_branch_b, w_out, router_w, router_b, expert_w_gu, expert_b_gu, expert_w_down, expert_b_down,
           final_norm_g):
    x = jnp.concatenate([x_prompt.reshape(T_P, D), x_sample.reshape(T_S, D)], axis=0)
    cvec = jnp.concatenate([c_ctx[None], c, jnp.zeros((3, D), f32)], axis=0)
    mods = _ada_mod(cvec, ada_w, ada_b)
    cos, sin = _rope_tables()
    ck = cache_k.reshape(DEC_BATCH, DEPTH, PAST_LEN, D)
    cv = cache_v.reshape(DEC_BATCH, DEPTH, PAST_LEN, D)
    main_w = 6 * D
    lr0 = main_w
    g0 = main_w + 2 * GK_RANK

    ks, vs, sfs, sbs = [], [], [], []
    yn = None
    for l in range(DEPTH):
        lam_init = 0.8 - 0.6 * math.exp(-0.3 * l)
        mod = mods[l]
        wm = w_in[l, :, :main_w].astype(bf16)
        wl = jnp.pad(w_in[l, :, lr0:g0], ((0, 0), (0, LANES - 2 * GK_RANK))).astype(bf16)
        wg = w_in[l, :, g0:].astype(bf16)
        q, k, v, bq, bk, bv, bg, lr, ga, gb = _inproj(x, mod, norm_mix_g[l][None], wm, wg, wl, cos, sin)
        ks.append(k[:T_P].reshape(BATCH, SEQ, D))
        vs.append(v[:T_P].reshape(BATCH, SEQ, D))

        lq = jnp.pad(diff_lambda[l], ((0, 4), (0, LANES - HD_A)))
        sg = diff_subln_g[l][None]
        oap = _attn_ctx(q, k, v, lq, sg, lam_init)
        oas = _attn_lat(q, k, v, ck, cv, lq, sg, lam_init, l)

        wf = jnp.pad(gla_gk_w[l, 0], ((0, LANES - GK_RANK), (0, 0)))
        wb = jnp.pad(gla_gk_w[l, 1], ((GK_RANK, LANES - 2 * GK_RANK), (0, 0)))
        bfv = gla_gk_b[l, 0][None]
        bbv = gla_gk_b[l, 1][None]
        gn = gla_norm_g[l][None]
        ogp, sf, sb = _gla(bq, bk, bv, bg, lr, wf, wb, bfv, bbv, gn, SEQ, 0, BATCH)
        ogs = _gla(bq, bk, bv, bg, lr, wf, wb, bfv, bbv, gn, DEC_SEQ, T_P // DEC_SEQ, DEC_BATCH,
                   states=(state_fwd, state_bwd), layer=l)
        sfs.append(sf)
        sbs.append(sb)

        rb = jnp.broadcast_to(router_b[l][:, None], (N_EXPERTS, LANES))
        x1, ht, idx8, gate8, rank8, cnt = _merge(
            x, oap, oas, ogp, ogs, ga, gb, mod, w_branch_a[l].astype(bf16), w_branch_b[l].astype(bf16),
            w_out[l].astype(bf16), norm_ffn_g[l][None], router_w[l].T, rb)
        moes = []
        for tok0, t_str, cnt_s in ((0, T_P, cnt[:N_EXPERTS, 0]), (T_P, T_S, cnt[N_EXPERTS:, 0])):
            sl = slice(tok0, tok0 + t_str)
            tables = _routing_tables(idx8[:TOP_K, sl], rank8[:TOP_K, sl], gate8[:TOP_K, sl], cnt_s)
            moes.append(_experts(*tables, ht, expert_w_gu, expert_b_gu, expert_w_down, expert_b_down, l,
                                 tok0, t_str))
        x, yn = _combine(x1, moes[0], moes[1], mod, final_norm_g[None])

    y_prompt = yn[:T_P].reshape(BATCH, SEQ, D)
    y_sample = yn[T_P:].reshape(DEC_BATCH, DEC_SEQ, D)
    new_k = jnp.stack(ks, axis=1).reshape(BATCH, DEPTH, SEQ, N_HEADS_A, 2, HD_A)
    new_v = jnp.stack(vs, axis=1).reshape(BATCH, DEPTH, SEQ, N_HEADS_A, 2 * HD_A)
    new_sf = jnp.stack(sfs, axis=1)
    new_sb = jnp.stack(sbs, axis=1)
    return (y_prompt, y_sample, new_k, new_v, new_sf, new_sb)
```

```python
import functools
import math

import jax
import jax.numpy as jnp
from jax import lax
from jax.experimental import pallas as pl
from jax.experimental.pallas import tpu as pltpu

f32 = jnp.float32
bf16 = jnp.bfloat16
i32 = jnp.int32

D = 1024
BATCH = 16
SEQ = 256
DEPTH = 2
DEC_BATCH = 4
DEC_SEQ = 1024
PAST_LEN = 512
GRID_W = 64
HD_A = 64
N_HEADS_A = 8
ROPE_BASE = 10000.0
N_HEADS_B = 4
DK_B = 128
DV_B = 256
GK_RANK = 16
GATE_NORM = 16.0
CHUNK = 64
N_EXPERTS = 32
TOP_K = 4
MOE_FF = 1024
SWIGLU_LIMIT = 7.0
SWIGLU_ALPHA = 1.702
EPS = 1e-6

T_P = BATCH * SEQ
T_S = DEC_BATCH * DEC_SEQ
T = T_P + T_S
TM = 256
N_TILES = T // TM
P_TILES = T_P // TM
S_TILES_PER_BATCH = DEC_SEQ // TM

MOE_ROWS = 256
SUB = 8
LANES = 128
FAST_DECAY_LIMIT = -60.0

_VMEM_LIMIT = 56 * 1024 * 1024
_EXPERTS_VMEM_LIMIT = 60 * 1024 * 1024


def _cparams(sem, vmem=_VMEM_LIMIT):
    return pltpu.CompilerParams(dimension_semantics=sem, vmem_limit_bytes=vmem)


def _dot(a, b):
    return jnp.dot(a, b, preferred_element_type=f32)


def _dot_nt(a, b):
    return lax.dot_general(a, b, (((1,), (1,)), ((), ())), preferred_element_type=f32)


def _split(x):
    hi = x.astype(bf16)
    lo = (x - hi.astype(f32)).astype(bf16)
    return hi, lo


def _dot3(a, b):
    ah, al = _split(a)
    bh, bl = _split(b)
    return _dot(ah, bh) + _dot(ah, bl) + _dot(al, bh)


def _dot3_nt(a, b):
    ah, al = _split(a)
    bh, bl = _split(b)
    return _dot_nt(ah, bh) + _dot_nt(ah, bl) + _dot_nt(al, bh)


def _silu(x):
    return x * jax.nn.sigmoid(x)


def _mod_row(i):
    return jnp.where(i < P_TILES, 0, 1 + lax.shift_right_logical(jnp.maximum(i - P_TILES, 0), 2))


def _ada_kernel(c_ref, w_ref, b_ref, o_ref):
    o_ref[0] = _dot3(_silu(c_ref[...]), w_ref[0]) + b_ref[0]


def _ada_mod(cvec, ada_w, ada_b):
    tn = 1536
    return pl.pallas_call(
        _ada_kernel,
        grid=(DEPTH, 6 * D // tn),
        in_specs=[
            pl.BlockSpec((8, D), lambda l, j: (0, 0)),
            pl.BlockSpec((1, D, tn), lambda l, j: (l, 0, j)),
            pl.BlockSpec((1, 1, tn), lambda l, j: (l, 0, j)),
        ],
        out_specs=pl.BlockSpec((1, 8, tn), lambda l, j: (l, 0, j)),
        out_shape=jax.ShapeDtypeStruct((DEPTH, 8, 6 * D), f32),
        compiler_params=_cparams(("arbitrary", "arbitrary")),
        name="ada_mod",
    )(cvec, ada_w, ada_b.reshape(DEPTH, 1, 6 * D))


def _inproj_kernel(x_ref, mod_ref, g_ref, wm_ref, wg_ref, wl_ref, cos_ref, sin_ref,
                   q_ref, k_ref, v_ref, bq_ref, bk_ref, bv_ref, bg_ref, lr_ref, ga_ref, gb_ref):
    i = pl.program_id(0)
    m = mod_ref[pl.ds(_mod_row(i), 1), :]
    shift = m[:, 0:D]
    scale = m[:, D:2 * D]
    x = x_ref[...]
    h = x * lax.rsqrt(jnp.mean(x * x, axis=-1, keepdims=True) + EPS) * g_ref[...]
    hb = (h * (1.0 + scale) + shift).astype(bf16)

    cos = cos_ref[...]
    sin = sin_ref[...]
    lane = lax.broadcasted_iota(i32, (TM, LANES), 1)
    lo_mask = (lane & 31) < 16

    def rope(a):
        partner = jnp.where(lo_mask, pltpu.roll(a, LANES - 16, 1), pltpu.roll(a, 16, 1))
        return a * cos + partner * sin

    aq = _dot(hb, wm_ref[:, 0:D])
    for c in range(D // LANES):
        q_ref[:, c * LANES:(c + 1) * LANES] = rope(aq[:, c * LANES:(c + 1) * LANES]).astype(bf16)
    ak = _dot(hb, wm_ref[:, D:2 * D])
    for c in range(D // LANES):
        k_ref[:, c * LANES:(c + 1) * LANES] = rope(ak[:, c * LANES:(c + 1) * LANES])
    v_ref[...] = _dot(hb, wm_ref[:, 2 * D:3 * D])
    bq_ref[...] = _dot(hb, wm_ref[:, 3 * D:3 * D + 512])
    bk_ref[...] = _dot(hb, wm_ref[:, 3 * D + 512:4 * D])
    bv_ref[...] = _dot(hb, wm_ref[:, 4 * D:5 * D])
    bg_ref[...] = _dot(hb, wm_ref[:, 5 * D:6 * D])
    lr_ref[...] = _dot(hb, wl_ref[...])
    ga_ref[...] = _dot(hb, wg_ref[:, 0:D])
    gb_ref[...] = _dot(hb, wg_ref[:, D:2 * D])


def _inproj(x, mod, g, wm, wg, wl, cos, sin):
    tile = lambda w: pl.BlockSpec((TM, w), lambda i: (i, 0))
    res = lambda shape: pl.BlockSpec(shape, lambda i: (0, 0), pipeline_mode=pl.Buffered(1))
    widths = (D, D, D, 512, 512, D, D, LANES, D, D)
    dtypes = (bf16,) + (f32,) * 9
    return pl.pallas_call(
        _inproj_kernel,
        grid=(N_TILES,),
        in_specs=[tile(D), res((8, 6 * D)), res((1, D)), res((D, 6 * D)), res((D, 2 * D)),
                  res((D, LANES)), tile(LANES), tile(LANES)],
        out_specs=[tile(w) for w in widths],
        out_shape=[jax.ShapeDtypeStruct((T, w), dt) for w, dt in zip(widths, dtypes)],
        compiler_params=_cparams(("arbitrary",)),
        name="inproj",
    )(x, mod, g, wm, wg, wl, cos, sin)


def _lam_from(lq_ref, lam_init):
    lq = lq_ref[...]
    s01 = jnp.sum(lq[0:1] * lq[1:2], axis=-1, keepdims=True)
    s23 = jnp.sum(lq[2:3] * lq[3:4], axis=-1, keepdims=True)
    return jnp.exp(s01) - jnp.exp(s23) + lam_init


def _subln(o, g, lam_init):
    o = o * lax.rsqrt(jnp.mean(o * o, axis=-1, keepdims=True) + EPS)
    return o * g * (1.0 - lam_init)


def _diff_heads(q_ref, segs, lq_ref, g_ref, o_ref, lam_init):
    lam = _lam_from(lq_ref, lam_init)
    g = g_ref[...]
    lo = lax.broadcasted_iota(i32, (q_ref.shape[0], LANES), 1) < HD_A
    for h in range(N_HEADS_A):
        cols = slice(h * LANES, (h + 1) * LANES)
        q = q_ref[:, cols] * (HD_A ** -0.5)
        kv = [(k_ref[:, cols], v_ref[:, cols]) for k_ref, v_ref in segs]
        outs = []
        for qm in (jnp.where(lo, q, 0), jnp.where(lo, 0, q)):
            ss = [_dot_nt(qm, kb) for kb, _ in kv]
            mx = functools.reduce(jnp.maximum, [jnp.max(s, axis=-1, keepdims=True) for s in ss])
            ps = [jnp.exp(s - mx) for s in ss]
            l = functools.reduce(jnp.add, [jnp.sum(p, axis=-1, keepdims=True) for p in ps])
            pv = functools.reduce(jnp.add, [_dot(p.astype(bf16), vb) for p, (_, vb) in zip(ps, kv)])
            outs.append(pv / l)
        o = outs[0] - lam * outs[1]
        o_ref[:, cols] = _subln(o, g, lam_init).astype(bf16)


def _attn_ctx_kernel(q_ref, k_ref, v_ref, lq_ref, g_ref, o_ref, kb_ref, vb_ref, *, lam_init):
    kb_ref[...] = k_ref[...].astype(bf16)
    vb_ref[...] = v_ref[...].astype(bf16)
    _diff_heads(q_ref, [(kb_ref, vb_ref)], lq_ref, g_ref, o_ref, lam_init)


def _attn_lat_kernel(q_ref, k_ref, v_ref, ck_ref, cv_ref, lq_ref, g_ref, o_ref, kb_ref, vb_ref, ckb_ref,
                     cvb_ref, *, lam_init):
    @pl.when(pl.program_id(1) == 0)
    def _():
        kb_ref[...] = k_ref[...].astype(bf16)
        vb_ref[...] = v_ref[...].astype(bf16)
        ckb_ref[...] = ck_ref[...].astype(bf16)
        cvb_ref[...] = cv_ref[...].astype(bf16)

    _diff_heads(q_ref, [(ckb_ref, cvb_ref), (kb_ref, vb_ref)], lq_ref, g_ref, o_ref, lam_init)


def _attn_ctx(q, k, v, lq, g, lam_init):
    blk = pl.BlockSpec((SEQ, D), lambda b: (b, 0))
    small = lambda s: pl.BlockSpec(s, lambda b: (0, 0))
    return pl.pallas_call(
        functools.partial(_attn_ctx_kernel, lam_init=lam_init),
        grid=(BATCH,),
        in_specs=[blk, blk, blk, small((8, LANES)), small((1, LANES))],
        out_specs=blk,
        out_shape=jax.ShapeDtypeStruct((T_P, D), bf16),
        scratch_shapes=[pltpu.VMEM((SEQ, D), bf16), pltpu.VMEM((SEQ, D), bf16)],
        compiler_params=_cparams(("arbitrary",)),
        name="attn_ctx",
    )(q, k, v, lq, g)


def _attn_lat(q, k, v, ck, cv, lq, g, lam_init, layer):
    nq = DEC_SEQ // TM
    qblk = pl.BlockSpec((TM, D), lambda b, t: (P_TILES + b * nq + t, 0))
    kblk = pl.BlockSpec((DEC_SEQ, D), lambda b, t: (T_P // DEC_SEQ + b, 0))
    cblk = pl.BlockSpec((None, None, PAST_LEN, D), lambda b, t: (b, layer, 0, 0))
    small = lambda s: pl.BlockSpec(s, lambda b, t: (0, 0))
    return pl.pallas_call(
        functools.partial(_attn_lat_kernel, lam_init=lam_init),
        grid=(DEC_BATCH, nq),
        in_specs=[qblk, kblk, kblk, cblk, cblk, small((8, LANES)), small((1, LANES))],
        out_specs=pl.BlockSpec((TM, D), lambda b, t: (b * nq + t, 0)),
        out_shape=jax.ShapeDtypeStruct((T_S, D), bf16),
        scratch_shapes=[pltpu.VMEM((DEC_SEQ, D), bf16), pltpu.VMEM((DEC_SEQ, D), bf16),
                        pltpu.VMEM((PAST_LEN, D), bf16), pltpu.VMEM((PAST_LEN, D), bf16)],
        compiler_params=_cparams(("arbitrary", "arbitrary")),
        name="attn_lat",
    )(q, k, v, ck, cv, lq, g)


def _log_decay(lr, w_ref, b_ref):
    z = _dot3(lr, w_ref[...]) + b_ref[...]
    return -(jnp.maximum(-z, 0.0) + jnp.log(1.0 + jnp.exp(-jnp.abs(z)))) / GATE_NORM


def _gla_direction(q_ref, k_ref, v3b, g, st0, reverse, b_ref, att_ref, sall_ref):
    tb = g.shape[0]
    n = tb // CHUNK
    scale = DK_B ** -0.5
    ri = lax.broadcasted_iota(i32, (CHUNK, CHUNK), 0)
    ci = lax.broadcasted_iota(i32, (CHUNK, CHUNK), 1)
    keep = (ci >= ri) if reverse else (ci <= ri)
    tri = jnp.broadcast_to(jnp.where(keep, 1.0, 0.0).astype(bf16)[None], (n, CHUNK, CHUNK))

    g3 = g.reshape(n, CHUNK, DK_B)
    ghi, glo = _split(g3)
    bdot = lambda a, b: lax.dot_general(a, b, (((2,), (1,)), ((0,), (0,))), preferred_element_type=f32)
    b = bdot(tri, ghi) + bdot(tri, glo)
    tot = b[:, 0:1, :] if reverse else b[:, CHUNK - 1:CHUNK, :]
    q3 = q_ref[...].reshape(n, CHUNK, DK_B) * scale
    k3 = k_ref[...].reshape(n, CHUNK, DK_B)

    qs = (q3 * jnp.exp(b)).astype(bf16)
    kl = (k3 * jnp.exp(tot - b)).astype(bf16)
    kn = (k3 * jnp.exp(jnp.minimum(-b, 80.0))).astype(bf16)
    att = lax.dot_general(qs, kn, (((2,), (2,)), ((0,), (0,))), preferred_element_type=f32)
    att_ref[...] = jnp.where(keep[None], att, 0.0)

    b_ref[...] = b
    need_fix = jnp.min(tot) < FAST_DECAY_LIMIT

    @pl.when(need_fix)
    def _():
        lane = lax.broadcasted_iota(i32, (CHUNK, CHUNK), 1)

        def chunk_body(c, carry):
            bc = b_ref[c]
            row0 = pl.multiple_of(c * CHUNK, CHUNK)
            tot_c = bc[0:1] if reverse else bc[CHUNK - 1:CHUNK]

            @pl.when(jnp.min(tot_c) < FAST_DECAY_LIMIT)
            def _():
                qc = q_ref[pl.ds(row0, CHUNK), :] * scale
                kc = k_ref[pl.ds(row0, CHUNK), :]
                row = lax.broadcasted_iota(i32, (CHUNK, DK_B), 0)

                def jbody(j, acc):
                    bj = b_ref[c, pl.ds(j, 1), :]
                    kj = jnp.sum(jnp.where(row == j, kc, 0.0), axis=0, keepdims=True)
                    e = jnp.exp(jnp.minimum(bc - bj, 0.0))
                    col = jnp.sum(qc * e * kj, axis=-1, keepdims=True)
                    return acc + jnp.where(lane == j, col, 0.0)

                acc = lax.fori_loop(0, CHUNK, jbody, jnp.zeros((CHUNK, CHUNK), f32))
                att_ref[c] = jnp.where(keep, acc, 0.0)

            return carry

        lax.fori_loop(0, n, chunk_body, 0)

    u = lax.dot_general(v3b, kl, (((1,), (1,)), ((0,), (0,))), preferred_element_type=f32)
    dec = jnp.exp(tot)
    st = st0
    order = range(n - 1, -1, -1) if reverse else range(n)
    for c in order:
        sall_ref[c] = st
        st = st * dec[c] + u[c]
    inter = lax.dot_general(qs, sall_ref[...].astype(bf16), (((2,), (2,)), ((0,), (0,))),
                            preferred_element_type=f32)
    intra = lax.dot_general(att_ref[...].astype(bf16), v3b, (((2,), (1,)), ((0,), (0,))),
                            preferred_element_type=f32)
    return (inter + intra).reshape(tb, DV_B), st


def _gla_kernel(*refs, has_init, hp):
    if has_init:
        (q_ref, k_ref, v_ref, bg_ref, lr_ref, wf_ref, wb_ref, bf_ref, bb_ref, gn_ref, sf0_ref, sb0_ref,
         o_ref, b_ref, att_ref, sall_ref) = refs
    else:
        (q_ref, k_ref, v_ref, bg_ref, lr_ref, wf_ref, wb_ref, bf_ref, bb_ref, gn_ref,
         o_ref, sf_ref, sb_ref, b_ref, att_ref, sall_ref) = refs
    tb = q_ref.shape[0]
    n = tb // CHUNK
    lr = lr_ref[...]
    gn = gn_ref[...]
    for h in range(hp):
        kc = slice(h * DK_B, (h + 1) * DK_B)
        vc = slice(h * DV_B, (h + 1) * DV_B)
        if has_init:
            stf0 = sf0_ref[...].T
            stb0 = sb0_ref[...].T
        else:
            stf0 = jnp.zeros((DV_B, DK_B), f32)
            stb0 = stf0
        qh = q_ref.at[:, kc]
        kh = k_ref.at[:, kc]
        v3b = v_ref[:, vc].reshape(n, CHUNK, DV_B).astype(bf16)
        gf = _log_decay(lr, wf_ref.at[:, kc], bf_ref.at[:, kc])
        of, stf = _gla_direction(qh, kh, v3b, gf, stf0, False, b_ref.at[h, 0], att_ref.at[h, 0],
                                 sall_ref.at[h, 0])
        gb = _log_decay(lr, wb_ref.at[:, kc], bb_ref.at[:, kc])
        ob, stb = _gla_direction(qh, kh, v3b, gb, stb0, True, b_ref.at[h, 1], att_ref.at[h, 1],
                                 sall_ref.at[h, 1])
        o = of + ob
        o = o * lax.rsqrt(jnp.mean(o * o, axis=-1, keepdims=True) + EPS) * gn
        o_ref[:, vc] = (o * _silu(bg_ref[:, vc])).astype(bf16)
        if not has_init:
            sf_ref[h] = stf.T
            sb_ref[h] = stb.T


def _gla(bq, bk, bv, bg, lr, wf, wb, bfv, bbv, gn, tb, row_blk0, nb, states=None, layer=0):
    has_init = states is not None
    hp = 1 if has_init else N_HEADS_B
    n = tb // CHUNK
    rows = lambda w: pl.BlockSpec((tb, hp * w), lambda b, h: (row_blk0 + b, h))
    lrs = pl.BlockSpec((tb, LANES), lambda b, h: (row_blk0 + b, 0))
    wsp = pl.BlockSpec((LANES, hp * DK_B), lambda b, h: (0, h))
    bsp = pl.BlockSpec((1, hp * DK_B), lambda b, h: (0, h))
    gsp = pl.BlockSpec((1, DV_B), lambda b, h: (0, 0))
    in_specs = [rows(DK_B), rows(DK_B), rows(DV_B), rows(DV_B), lrs, wsp, wsp, bsp, bsp, gsp]
    args = [bq, bk, bv, bg, lr, wf, wb, bfv, bbv, gn]
    osp = pl.BlockSpec((tb, hp * DV_B), lambda b, h: (b, h))
    oshape = jax.ShapeDtypeStruct((nb * tb, D), bf16)
    if has_init:
        ssp = pl.BlockSpec((None, None, None, DK_B, DV_B), lambda b, h: (b, layer, h, 0, 0))
        in_specs += [ssp, ssp]
        args += list(states)
        out_specs, out_shape = osp, oshape
    else:
        ssp = pl.BlockSpec((None, hp, DK_B, DV_B), lambda b, h: (b, h, 0, 0))
        sshape = jax.ShapeDtypeStruct((nb, N_HEADS_B, DK_B, DV_B), f32)
        out_specs, out_shape = [osp, ssp, ssp], [oshape, sshape, sshape]
    return pl.pallas_call(
        functools.partial(_gla_kernel, has_init=has_init, hp=hp),
        grid=(nb, N_HEADS_B // hp),
        in_specs=in_specs,
        out_specs=out_specs,
        out_shape=out_shape,
        scratch_shapes=[pltpu.VMEM((hp, 2, n, CHUNK, DK_B), f32), pltpu.VMEM((hp, 2, n, CHUNK, CHUNK), f32),
                        pltpu.VMEM((hp, 2, n, DV_B, DK_B), f32)],
        compiler_params=_cparams(("arbitrary", "arbitrary")),
        name="gla_lat" if has_init else "gla_ctx",
    )(*args)


def _merge_kernel(x_ref, oap_ref, oas_ref, ogp_ref, ogs_ref, ga_ref, gb_ref, mod_ref, wa_ref, wb_ref,
                  wo_ref, g_ref, rw_ref, rb_ref,
                  x1_ref, ht_ref, idx_ref, gate_ref, rank_ref, cnt_ref, base_ref):
    i = pl.program_id(0)

    @pl.when((i == 0) | (i == P_TILES))
    def _():
        base_ref[...] = jnp.zeros_like(base_ref)

    is_ctx = i < P_TILES
    oa = jnp.where(is_ctx, oap_ref[...], oas_ref[...])
    og = jnp.where(is_ctx, ogp_ref[...], ogs_ref[...])
    ya = _dot(oa, wa_ref[...])
    yb = _dot(og, wb_ref[...])
    merged = jax.nn.sigmoid(ga_ref[...]) * ya + jax.nn.sigmoid(gb_ref[...]) * yb
    mix = _dot(merged.astype(bf16), wo_ref[...])
    m = mod_ref[pl.ds(_mod_row(i), 1), :]
    x1 = x_ref[...] + m[:, 2 * D:3 * D] * mix
    x1_ref[...] = x1
    h = x1 * lax.rsqrt(jnp.mean(x1 * x1, axis=-1, keepdims=True) + EPS) * g_ref[...]
    h = h * (1.0 + m[:, 4 * D:5 * D]) + m[:, 3 * D:4 * D]
    for s in range(SUB):
        ht_ref[pl.ds(s, TM, stride=SUB), :] = h[:, s * LANES:(s + 1) * LANES]

    lg = _dot3_nt(rw_ref[...], h) + rb_ref[...][:, 0:1]
    eid = lax.broadcasted_iota(i32, (N_EXPERTS, TM), 0)
    vals, sels, idxs = [], [], []
    for _ in range(TOP_K):
        mx = jnp.max(lg, axis=0, keepdims=True)
        ik = jnp.min(jnp.where(lg == mx, eid, N_EXPERTS), axis=0, keepdims=True)
        sel = eid == ik
        vals.append(mx)
        idxs.append(ik)
        sels.append(sel)
        lg = jnp.where(sel, -jnp.inf, lg)
    es = [jnp.exp(v - vals[0]) for v in vals]
    den = es[0] + es[1] + es[2] + es[3]
    onehot = jnp.where(sels[0] | sels[1] | sels[2] | sels[3], 1.0, 0.0)
    tr = lax.broadcasted_iota(i32, (TM, TM), 0)
    tc = lax.broadcasted_iota(i32, (TM, TM), 1)
    before = jnp.where(tr < tc, 1.0, 0.0).astype(bf16)
    pos = base_ref[:, 0:1] + _dot(onehot.astype(bf16), before)
    zrow_i = jnp.zeros((8 - TOP_K, TM), i32)
    ranks = [jnp.sum(jnp.where(s, pos, 0.0), axis=0, keepdims=True).astype(i32) for s in sels]
    idx_ref[...] = jnp.concatenate(idxs + [zrow_i], axis=0)
    rank_ref[...] = jnp.concatenate(ranks + [zrow_i], axis=0)
    gate_ref[...] = jnp.concatenate([e / den for e in es] + [jnp.zeros((8 - TOP_K, TM), f32)], axis=0)
    base = base_ref[...] + jnp.sum(onehot, axis=1, keepdims=True)
    base_ref[...] = base
    cnt_ref[...] = base.astype(i32)


def _merge(x, oap, oas, ogp, ogs, ga, gb, mod, wa, wb, wo, g, rw, rb):
    tile = lambda w: pl.BlockSpec((TM, w), lambda i: (i, 0))
    ctx = pl.BlockSpec((TM, D), lambda i: (jnp.minimum(i, P_TILES - 1), 0))
    lat = pl.BlockSpec((TM, D), lambda i: (jnp.maximum(i - P_TILES, 0), 0))
    res = lambda shape: pl.BlockSpec(shape, lambda i: (0, 0), pipeline_mode=pl.Buffered(1))
    meta = pl.BlockSpec((8, TM), lambda i: (0, i))
    return pl.pallas_call(
        _merge_kernel,
        grid=(N_TILES,),
        in_specs=[tile(D), ctx, lat, ctx, lat, tile(D), tile(D), res((8, 6 * D)), res((D, D)), res((D, D)),
                  res((D, D)), res((1, D)), res((N_EXPERTS, D)), res((N_EXPERTS, LANES))],
        out_specs=[tile(D), pl.BlockSpec((TM * SUB, LANES), lambda i: (i, 0)), meta, meta, meta,
                   pl.BlockSpec((N_EXPERTS, LANES), lambda i: (jnp.where(i < P_TILES, 0, 1), 0))],
        out_shape=[jax.ShapeDtypeStruct((T, D), f32), jax.ShapeDtypeStruct((T * SUB, LANES), f32),
                   jax.ShapeDtypeStruct((8, T), i32), jax.ShapeDtypeStruct((8, T), f32),
                   jax.ShapeDtypeStruct((8, T), i32), jax.ShapeDtypeStruct((2 * N_EXPERTS, LANES), i32)],
        scratch_shapes=[pltpu.VMEM((N_EXPERTS, LANES), f32)],
        compiler_params=_cparams(("arbitrary",)),
        name="merge_route",
    )(x, oap, oas, ogp, ogs, ga, gb, mod, wa, wb, wo, g, rw, rb)


def _experts_kernel(blk_e_ref, nxt_e_ref, nused_ref, src_ref, meta_ref, h_hbm, wgu_hbm, bgu_ref, wdn_hbm,
                    bdn_ref, out_hbm, hv, acc, wgu_st, wdn_st, wgu_bf, wdn_bf, xg, yb, wsem, iosem, *,
                    layer, tok0):
    i = pl.program_id(0)
    n_used = nused_ref[0]

    @pl.when(i < n_used)
    def _():
        _experts_step(i, n_used, blk_e_ref, nxt_e_ref, src_ref, meta_ref, h_hbm, wgu_hbm, bgu_ref, wdn_hbm,
                      bdn_ref, out_hbm, hv, acc, wgu_st, wdn_st, wgu_bf, wdn_bf, xg, yb, wsem, iosem, layer,
                      tok0)


def _experts_step(i, n_used, blk_e_ref, nxt_e_ref, src_ref, meta_ref, h_hbm, wgu_hbm, bgu_ref, wdn_hbm,
                  bdn_ref, out_hbm, hv, acc, wgu_st, wdn_st, wgu_bf, wdn_bf, xg, yb, wsem, iosem, layer, tok0):
    t_str = hv.shape[0] // SUB

    def weight_copies(e):
        return (pltpu.make_async_copy(wgu_hbm.at[layer, e], wgu_st, wsem.at[0]),
                pltpu.make_async_copy(wdn_hbm.at[layer, e], wdn_st, wsem.at[1]))

    e = blk_e_ref[i]

    @pl.when(i == 0)
    def _():
        for cp in weight_copies(e):
            cp.start()
        cp = pltpu.make_async_copy(h_hbm.at[pl.ds(tok0 * SUB, t_str * SUB)], hv, iosem.at[0])
        cp.start()
        acc[...] = jnp.zeros_like(acc)
        cp.wait()

    @pl.when((i == 0) | (e != blk_e_ref[jnp.maximum(i - 1, 0)]))
    def _():
        for cp in weight_copies(e):
            cp.wait()
        wgu_bf[...] = wgu_st[...].astype(bf16)
        wdn_bf[...] = wdn_st[...].astype(bf16)
        en = nxt_e_ref[e]

        @pl.when(en >= 0)
        def _():
            for cp in weight_copies(en):
                cp.start()

    for r in range(MOE_ROWS):
        o = pl.multiple_of(jnp.minimum(src_ref[0, 0, r], t_str - 1) * SUB, SUB)
        xg[r * SUB:(r + 1) * SUB, :] = hv[pl.ds(o, SUB), :]

    xb = jnp.concatenate([xg[pl.ds(s, MOE_ROWS, stride=SUB), :] for s in range(SUB)],
                         axis=1).astype(bf16)
    gu = _dot(xb, wgu_bf[...]) + bgu_ref[...]
    gt = jnp.minimum(gu[:, :MOE_FF], SWIGLU_LIMIT)
    up = jnp.clip(gu[:, MOE_FF:], -SWIGLU_LIMIT, SWIGLU_LIMIT)
    act = (up + 1.0) * gt * jax.nn.sigmoid(SWIGLU_ALPHA * gt)
    y = _dot(act.astype(bf16), wdn_bf[...]) + bdn_ref[...]
    ri = lax.broadcasted_iota(i32, (MOE_ROWS, MOE_ROWS), 0)
    ci = lax.broadcasted_iota(i32, (MOE_ROWS, MOE_ROWS), 1)
    diag = ri == ci
    col = lambda v: jnp.sum(jnp.where(diag, jnp.broadcast_to(v, (MOE_ROWS, MOE_ROWS)), 0.0),
                            axis=1, keepdims=True)
    y = jnp.where(col(meta_ref[0, 1:2, :]) > 0.5, y * col(meta_ref[0, 0:1, :]), 0.0)
    for s in range(SUB):
        yb[pl.ds(s, MOE_ROWS, stride=SUB), :] = y[:, s * LANES:(s + 1) * LANES]
    group = 16
    for r0 in range(0, MOE_ROWS, group):
        offs = [pl.multiple_of(src_ref[0, 0, r] * SUB, SUB) for r in range(r0, r0 + group)]
        olds = [acc[pl.ds(o, SUB), :] for o in offs]
        for j, o in enumerate(offs):
            acc[pl.ds(o, SUB), :] = olds[j] + yb[(r0 + j) * SUB:(r0 + j + 1) * SUB, :]

    @pl.when(i == n_used - 1)
    def _():
        cp = pltpu.make_async_copy(acc.at[pl.ds(0, t_str * SUB)], out_hbm, iosem.at[0])
        cp.start()
        cp.wait()


def _experts(blk_e, nxt_e, n_used, row_src, meta, ht, wgu, bgu, wdn, bdn, layer, tok0, t_str):
    n_blocks = row_src.shape[0] // MOE_ROWS
    src3 = row_src.reshape(n_blocks, 1, MOE_ROWS)
    grid_spec = pltpu.PrefetchScalarGridSpec(
        num_scalar_prefetch=3,
        grid=(n_blocks,),
        in_specs=[
            pl.BlockSpec((1, 1, MOE_ROWS), lambda i, be, ne, nu: (i, 0, 0), memory_space=pltpu.SMEM),
            pl.BlockSpec((1, 8, MOE_ROWS), lambda i, be, ne, nu: (i, 0, 0)),
            pl.BlockSpec(memory_space=pl.ANY),
            pl.BlockSpec(memory_space=pl.ANY),
            pl.BlockSpec((None, None, 1, 2 * MOE_FF), lambda i, be, ne, nu: (layer, be[i], 0, 0)),
            pl.BlockSpec(memory_space=pl.ANY),
            pl.BlockSpec((None, None, 1, D), lambda i, be, ne, nu: (layer, be[i], 0, 0)),
        ],
        out_specs=pl.BlockSpec(memory_space=pl.ANY),
        scratch_shapes=[
            pltpu.VMEM((t_str * SUB, LANES), f32),
            pltpu.VMEM(((t_str + 8) * SUB, LANES), f32),
            pltpu.VMEM((D, 2 * MOE_FF), f32),
            pltpu.VMEM((MOE_FF, D), f32),
            pltpu.VMEM((D, 2 * MOE_FF), bf16),
            pltpu.VMEM((MOE_FF, D), bf16),
            pltpu.VMEM((MOE_ROWS * SUB, LANES), f32),
            pltpu.VMEM((MOE_ROWS * SUB, LANES), f32),
            pltpu.SemaphoreType.DMA((2,)),
            pltpu.SemaphoreType.DMA((1,)),
        ],
    )
    return pl.pallas_call(
        functools.partial(_experts_kernel, layer=layer, tok0=tok0),
        grid_spec=grid_spec,
        out_shape=jax.ShapeDtypeStruct((t_str * SUB, LANES), f32),
        compiler_params=_cparams(("arbitrary",), _EXPERTS_VMEM_LIMIT),
        name="experts",
    )(blk_e, nxt_e, n_used, src3, meta, ht, wgu, bgu.reshape(DEPTH, N_EXPERTS, 1, 2 * MOE_FF),
      wdn, bdn.reshape(DEPTH, N_EXPERTS, 1, D))


def _combine_kernel(x1_ref, moep_ref, moes_ref, mod_ref, fg_ref, x2_ref, yn_ref):
    i = pl.program_id(0)
    rows = lambda ref: jnp.concatenate([ref[pl.ds(s, TM, stride=SUB), :] for s in range(SUB)], axis=1)
    moe = jnp.where(i < P_TILES, rows(moep_ref), rows(moes_ref))
    m = mod_ref[pl.ds(_mod_row(i), 1), :]
    x2 = x1_ref[...] + m[:, 5 * D:6 * D] * moe
    x2_ref[...] = x2
    yn_ref[...] = x2 * lax.rsqrt(jnp.mean(x2 * x2, axis=-1, keepdims=True) + EPS) * fg_ref[...]


def _combine(x1, moep, moes, mod, fg):
    tile = pl.BlockSpec((TM, D), lambda i: (i, 0))
    res = lambda shape: pl.BlockSpec(shape, lambda i: (0, 0), pipeline_mode=pl.Buffered(1))
    ctx = pl.BlockSpec((TM * SUB, LANES), lambda i: (jnp.minimum(i, P_TILES - 1), 0))
    lat = pl.BlockSpec((TM * SUB, LANES), lambda i: (jnp.maximum(i - P_TILES, 0), 0))
    return pl.pallas_call(
        _combine_kernel,
        grid=(N_TILES,),
        in_specs=[tile, ctx, lat, res((8, 6 * D)), res((1, D))],
        out_specs=[tile, tile],
        out_shape=[jax.ShapeDtypeStruct((T, D), f32), jax.ShapeDtypeStruct((T, D), f32)],
        compiler_params=_cparams(("arbitrary",)),
        name="combine",
    )(x1, moep, moes, mod, fg)


def _rope_tables():
    nf = HD_A // 4
    inv = ROPE_BASE ** (-jnp.arange(nf, dtype=f32) / nf)
    n = jnp.arange(DEC_SEQ)
    row = (n // GRID_W).astype(f32)[:, None] * inv
    col = (n % GRID_W).astype(f32)[:, None] * inv
    cos64 = jnp.concatenate([jnp.cos(row), jnp.cos(row), jnp.cos(col), jnp.cos(col)], axis=-1)
    sin64 = jnp.concatenate([-jnp.sin(row), jnp.sin(row), -jnp.sin(col), jnp.sin(col)], axis=-1)
    cos_s = jnp.tile(jnp.tile(cos64, (1, 2)), (DEC_BATCH, 1))
    sin_s = jnp.tile(jnp.tile(sin64, (1, 2)), (DEC_BATCH, 1))
    cos = jnp.concatenate([jnp.ones((T_P, LANES), f32), cos_s], axis=0)
    sin = jnp.concatenate([jnp.zeros((T_P, LANES), f32), sin_s], axis=0)
    return cos, sin


def _routing_tables(idx, rank, gate, counts):
    t_str = idx.shape[1]
    n_blocks = (t_str * TOP_K + N_EXPERTS * (MOE_ROWS - 1)) // MOE_ROWS + 1
    padded = (counts + MOE_ROWS - 1) // MOE_ROWS * MOE_ROWS
    pad_end = jnp.cumsum(padded)
    pad_start = pad_end - padded
    eids = jnp.arange(N_EXPERTS, dtype=i32)
    start_of = jnp.sum(jnp.where(idx[..., None] == eids, pad_start, 0), axis=-1)
    dest = (start_of + rank).reshape(-1)
    inv = jnp.full((n_blocks * MOE_ROWS,), -1, i32).at[dest].set(jnp.arange(TOP_K * t_str, dtype=i32))
    valid = inv >= 0
    safe = jnp.maximum(inv, 0)
    row_src = jnp.where(valid, safe % t_str, t_str).astype(i32)
    row_gate = jnp.where(valid, gate.reshape(-1)[safe], 0.0)
    meta = jnp.zeros((n_blocks, 8, MOE_ROWS), f32)
    meta = meta.at[:, 0].set(row_gate.reshape(n_blocks, MOE_ROWS))
    meta = meta.at[:, 1].set(valid.astype(f32).reshape(n_blocks, MOE_ROWS))
    starts = jnp.arange(n_blocks, dtype=i32) * MOE_ROWS
    blk_e = jnp.sum((pad_end[None, :] <= starts[:, None]).astype(i32), axis=1)
    present = counts > 0
    blk_e = jnp.minimum(blk_e, jnp.max(jnp.where(present, eids, 0))).astype(i32)
    later = (eids[None, :] > eids[:, None]) & present[None, :]
    nxt = jnp.min(jnp.where(later, eids[None, :], N_EXPERTS), axis=1)
    nxt_e = jnp.where(nxt < N_EXPERTS, nxt, -1).astype(i32)
    n_used = (pad_end[-1:] // MOE_ROWS).astype(i32)
    return blk_e, nxt_e, n_used, row_src, meta


def kernel(x_prompt, x_sample, cache_k, cache_v, state_fwd, state_bwd, c, c_ctx, ada_w, ada_b, norm_mix_g,
           norm_ffn_g, w_in, diff_lambda, diff_subln_g, gla_gk_w, gla_gk_b, gla_norm_g, w_branch_a,
           w_branch_b, w_out, router_w, router_b, expert_w_gu, expert_b_gu, expert_w_down, expert_b_down,
           final_norm_g):
    x = jnp.concatenate([x_prompt.reshape(T_P, D), x_sample.reshape(T_S, D)], axis=0)
    cvec = jnp.concatenate([c_ctx[None], c, jnp.zeros((3, D), f32)], axis=0)
    mods = _ada_mod(cvec, ada_w, ada_b)
    cos, sin = _rope_tables()
    ck = cache_k.reshape(DEC_BATCH, DEPTH, PAST_LEN, D)
    cv = cache_v.reshape(DEC_BATCH, DEPTH, PAST_LEN, D)
    main_w = 6 * D
    lr0 = main_w
    g0 = main_w + 2 * GK_RANK

    ks, vs, sfs, sbs = [], [], [], []
    yn = None
    for l in range(DEPTH):
        lam_init = 0.8 - 0.6 * math.exp(-0.3 * l)
        mod = mods[l]
        wm = w_in[l, :, :main_w].astype(bf16)
        wl = jnp.pad(w_in[l, :, lr0:g0], ((0, 0), (0, LANES - 2 * GK_RANK))).astype(bf16)
        wg = w_in[l, :, g0:].astype(bf16)
        q, k, v, bq, bk, bv, bg, lr, ga, gb = _inproj(x, mod, norm_mix_g[l][None], wm, wg, wl, cos, sin)
        ks.append(k[:T_P].reshape(BATCH, SEQ, D))
        vs.append(v[:T_P].reshape(BATCH, SEQ, D))

        lq = jnp.pad(diff_lambda[l], ((0, 4), (0, LANES - HD_A)))
        sg = diff_subln_g[l][None]
        oap = _attn_ctx(q, k, v, lq, sg, lam_init)
        oas = _attn_lat(q, k, v, ck, cv, lq, sg, lam_init, l)

        wf = jnp.pad(gla_gk_w[l, 0], ((0, LANES - GK_RANK), (0, 0)))
        wb = jnp.pad(gla_gk_w[l, 1], ((GK_RANK, LANES - 2 * GK_RANK), (0, 0)))
        bfv = gla_gk_b[l, 0][None]
        bbv = gla_gk_b[l, 1][None]
        gn = gla_norm_g[l][None]
        ogp, sf, sb = _gla(bq, bk, bv, bg, lr, wf, wb, bfv, bbv, gn, SEQ, 0, BATCH)
        ogs = _gla(bq, bk, bv, bg, lr, wf, wb, bfv, bbv, gn, DEC_SEQ, T_P // DEC_SEQ, DEC_BATCH,
                   states=(state_fwd, state_bwd), layer=l)
        sfs.append(sf)
        sbs.append(sb)

        rb = jnp.broadcast_to(router_b[l][:, None], (N_EXPERTS, LANES))
        x1, ht, idx8, gate8, rank8, cnt = _merge(
            x, oap, oas, ogp, ogs, ga, gb, mod, w_branch_a[l].astype(bf16), w_branch_b[l].astype(bf16),
            w_out[l].astype(bf16), norm_ffn_g[l][None], router_w[l].T, rb)
        moes = []
        for tok0, t_str, cnt_s in ((0, T_P, cnt[:N_EXPERTS, 0]), (T_P, T_S, cnt[N_EXPERTS:, 0])):
            sl = slice(tok0, tok0 + t_str)
            tables = _routing_tables(idx8[:TOP_K, sl], rank8[:TOP_K, sl], gate8[:TOP_K, sl], cnt_s)
            moes.append(_experts(*tables, ht, expert_w_gu, expert_b_gu, expert_w_down, expert_b_down, l,
                                 tok0, t_str))
        x, yn = _combine(x1, moes[0], moes[1], mod, final_norm_g[None])

    y_prompt = yn[:T_P].reshape(BATCH, SEQ, D)
    y_sample = yn[T_P:].reshape(DEC_BATCH, DEC_SEQ, D)
    new_k = jnp.stack(ks, axis=1).reshape(BATCH, DEPTH, SEQ, N_HEADS_A, 2, HD_A)
    new_v = jnp.stack(vs, axis=1).reshape(BATCH, DEPTH, SEQ, N_HEADS_A, 2 * HD_A)
    new_sf = jnp.stack(sfs, axis=1)
    new_sb = jnp.stack(sbs, axis=1)
    return (y_prompt, y_sample, new_k, new_v, new_sf, new_sb)
```

```python
import functools
import math

import jax
import jax.numpy as jnp
from jax import lax
from jax.experimental import pallas as pl
from jax.experimental.pallas import tpu as pltpu

f32 = jnp.float32
bf16 = jnp.bfloat16
i32 = jnp.int32

D = 1024
BATCH = 16
SEQ = 256
DEPTH = 2
DEC_BATCH = 4
DEC_SEQ = 1024
PAST_LEN = 512
GRID_W = 64
HD_A = 64
N_HEADS_A = 8
ROPE_BASE = 10000.0
N_HEADS_B = 4
DK_B = 128
DV_B = 256
GK_RANK = 16
GATE_NORM = 16.0
CHUNK = 64
N_EXPERTS = 32
TOP_K = 4
MOE_FF = 1024
SWIGLU_LIMIT = 7.0
SWIGLU_ALPHA = 1.702
EPS = 1e-6

T_P = BATCH * SEQ
T_S = DEC_BATCH * DEC_SEQ
T = T_P + T_S
TM = 256
N_TILES = T // TM
P_TILES = T_P // TM
S_TILES_PER_BATCH = DEC_SEQ // TM

MOE_ROWS = 256
MOE_SUB_ROWS = 128
SUB = 8
LANES = 128
FAST_DECAY_LIMIT = -60.0

_VMEM_LIMIT = 56 * 1024 * 1024
_EXPERTS_VMEM_LIMIT = 60 * 1024 * 1024


def _cparams(sem, vmem=_VMEM_LIMIT):
    return pltpu.CompilerParams(dimension_semantics=sem, vmem_limit_bytes=vmem)


def _dot(a, b):
    return jnp.dot(a, b, preferred_element_type=f32)


def _dot_nt(a, b):
    return lax.dot_general(a, b, (((1,), (1,)), ((), ())), preferred_element_type=f32)


def _split(x):
    hi = x.astype(bf16)
    lo = (x - hi.astype(f32)).astype(bf16)
    return hi, lo


def _dot3(a, b):
    ah, al = _split(a)
    bh, bl = _split(b)
    return _dot(ah, bh) + _dot(ah, bl) + _dot(al, bh)


def _dot3_nt(a, b):
    ah, al = _split(a)
    bh, bl = _split(b)
    return _dot_nt(ah, bh) + _dot_nt(ah, bl) + _dot_nt(al, bh)


def _silu(x):
    return x * jax.nn.sigmoid(x)


def _mod_row(i):
    return jnp.where(i < P_TILES, 0, 1 + lax.shift_right_logical(jnp.maximum(i - P_TILES, 0), 2))


def _ada_kernel(c_ref, w_ref, b_ref, o_ref):
    o_ref[0] = _dot3(_silu(c_ref[...]), w_ref[0]) + b_ref[0]


def _ada_mod(cvec, ada_w, ada_b):
    tn = 1536
    return pl.pallas_call(
        _ada_kernel,
        grid=(DEPTH, 6 * D // tn),
        in_specs=[
            pl.BlockSpec((8, D), lambda l, j: (0, 0)),
            pl.BlockSpec((1, D, tn), lambda l, j: (l, 0, j)),
            pl.BlockSpec((1, 1, tn), lambda l, j: (l, 0, j)),
        ],
        out_specs=pl.BlockSpec((1, 8, tn), lambda l, j: (l, 0, j)),
        out_shape=jax.ShapeDtypeStruct((DEPTH, 8, 6 * D), f32),
        compiler_params=_cparams(("arbitrary", "arbitrary")),
        name="ada_mod",
    )(cvec, ada_w, ada_b.reshape(DEPTH, 1, 6 * D))


def _inproj_kernel(x_ref, mod_ref, g_ref, wm_ref, wg_ref, wl_ref, cos_ref, sin_ref,
                   q_ref, k_ref, v_ref, bq_ref, bk_ref, bv_ref, bg_ref, lr_ref, ga_ref, gb_ref):
    i = pl.program_id(0)
    m = mod_ref[pl.ds(_mod_row(i), 1), :]
    shift = m[:, 0:D]
    scale = m[:, D:2 * D]
    x = x_ref[...]
    h = x * lax.rsqrt(jnp.mean(x * x, axis=-1, keepdims=True) + EPS) * g_ref[...]
    hb = (h * (1.0 + scale) + shift).astype(bf16)

    cos = cos_ref[...]
    sin = sin_ref[...]
    lane = lax.broadcasted_iota(i32, (TM, LANES), 1)
    lo_mask = (lane & 31) < 16

    def rope(a):
        partner = jnp.where(lo_mask, pltpu.roll(a, LANES - 16, 1), pltpu.roll(a, 16, 1))
        return a * cos + partner * sin

    aq = _dot(hb, wm_ref[:, 0:D])
    for c in range(D // LANES):
        q_ref[:, c * LANES:(c + 1) * LANES] = rope(aq[:, c * LANES:(c + 1) * LANES]).astype(bf16)
    ak = _dot(hb, wm_ref[:, D:2 * D])
    for c in range(D // LANES):
        k_ref[:, c * LANES:(c + 1) * LANES] = rope(ak[:, c * LANES:(c + 1) * LANES])
    v_ref[...] = _dot(hb, wm_ref[:, 2 * D:3 * D])
    bq_ref[...] = _dot(hb, wm_ref[:, 3 * D:3 * D + 512]).astype(bf16)
    bk_ref[...] = _dot(hb, wm_ref[:, 3 * D + 512:4 * D]).astype(bf16)
    bv_ref[...] = _dot(hb, wm_ref[:, 4 * D:5 * D]).astype(bf16)
    bg_ref[...] = _dot(hb, wm_ref[:, 5 * D:6 * D]).astype(bf16)
    lr_ref[...] = _dot(hb, wl_ref[...])
    ga_ref[...] = _dot(hb, wg_ref[:, 0:D]).astype(bf16)
    gb_ref[...] = _dot(hb, wg_ref[:, D:2 * D]).astype(bf16)


def _inproj(x, mod, g, wm, wg, wl, cos, sin):
    tile = lambda w: pl.BlockSpec((TM, w), lambda i: (i, 0))
    res = lambda shape: pl.BlockSpec(shape, lambda i: (0, 0), pipeline_mode=pl.Buffered(1))
    widths = (D, D, D, 512, 512, D, D, LANES, D, D)
    dtypes = (bf16, f32, f32, bf16, bf16, bf16, bf16, f32, bf16, bf16)
    return pl.pallas_call(
        _inproj_kernel,
        grid=(N_TILES,),
        in_specs=[tile(D), res((8, 6 * D)), res((1, D)), res((D, 6 * D)), res((D, 2 * D)),
                  res((D, LANES)), tile(LANES), tile(LANES)],
        out_specs=[tile(w) for w in widths],
        out_shape=[jax.ShapeDtypeStruct((T, w), dt) for w, dt in zip(widths, dtypes)],
        compiler_params=_cparams(("arbitrary",)),
        name="inproj",
    )(x, mod, g, wm, wg, wl, cos, sin)


def _lam_from(lq_ref, lam_init):
    lq = lq_ref[...]
    s01 = jnp.sum(lq[0:1] * lq[1:2], axis=-1, keepdims=True)
    s23 = jnp.sum(lq[2:3] * lq[3:4], axis=-1, keepdims=True)
    return jnp.exp(s01) - jnp.exp(s23) + lam_init


def _subln(o, g, lam_init):
    o = o * lax.rsqrt(jnp.mean(o * o, axis=-1, keepdims=True) + EPS)
    return o * g * (1.0 - lam_init)


def _diff_heads(q_ref, segs, lq_ref, g_ref, o_ref, lam_init):
    lam = _lam_from(lq_ref, lam_init)
    g = g_ref[...]
    lo = lax.broadcasted_iota(i32, (q_ref.shape[0], LANES), 1) < HD_A
    for h in range(N_HEADS_A):
        cols = slice(h * LANES, (h + 1) * LANES)
        q = q_ref[:, cols] * (HD_A ** -0.5)
        kv = [(k_ref[:, cols], v_ref[:, cols]) for k_ref, v_ref in segs]
        outs = []
        for qm in (jnp.where(lo, q, 0), jnp.where(lo, 0, q)):
            ss = [_dot_nt(qm, kb) for kb, _ in kv]
            mx = functools.reduce(jnp.maximum, [jnp.max(s, axis=-1, keepdims=True) for s in ss])
            ps = [jnp.exp(s - mx) for s in ss]
            l = functools.reduce(jnp.add, [jnp.sum(p, axis=-1, keepdims=True) for p in ps])
            pv = functools.reduce(jnp.add, [_dot(p.astype(bf16), vb) for p, (_, vb) in zip(ps, kv)])
            outs.append(pv / l)
        o = outs[0] - lam * outs[1]
        o_ref[:, cols] = _subln(o, g, lam_init).astype(bf16)


def _attn_ctx_kernel(q_ref, k_ref, v_ref, lq_ref, g_ref, o_ref, kb_ref, vb_ref, *, lam_init):
    kb_ref[...] = k_ref[...].astype(bf16)
    vb_ref[...] = v_ref[...].astype(bf16)
    _diff_heads(q_ref, [(kb_ref, vb_ref)], lq_ref, g_ref, o_ref, lam_init)


def _attn_lat_kernel(q_ref, k_ref, v_ref, ck_ref, cv_ref, lq_ref, g_ref, o_ref, kb_ref, vb_ref, ckb_ref,
                     cvb_ref, *, lam_init):
    @pl.when(pl.program_id(1) == 0)
    def _():
        kb_ref[...] = k_ref[...].astype(bf16)
        vb_ref[...] = v_ref[...].astype(bf16)
        ckb_ref[...] = ck_ref[...].astype(bf16)
        cvb_ref[...] = cv_ref[...].astype(bf16)

    _diff_heads(q_ref, [(ckb_ref, cvb_ref), (kb_ref, vb_ref)], lq_ref, g_ref, o_ref, lam_init)


def _attn_ctx(q, k, v, lq, g, lam_init):
    blk = pl.BlockSpec((SEQ, D), lambda b: (b, 0))
    small = lambda s: pl.BlockSpec(s, lambda b: (0, 0))
    return pl.pallas_call(
        functools.partial(_attn_ctx_kernel, lam_init=lam_init),
        grid=(BATCH,),
        in_specs=[blk, blk, blk, small((8, LANES)), small((1, LANES))],
        out_specs=blk,
        out_shape=jax.ShapeDtypeStruct((T_P, D), bf16),
        scratch_shapes=[pltpu.VMEM((SEQ, D), bf16), pltpu.VMEM((SEQ, D), bf16)],
        compiler_params=_cparams(("arbitrary",)),
        name="attn_ctx",
    )(q, k, v, lq, g)


def _attn_lat(q, k, v, ck, cv, lq, g, lam_init, layer):
    nq = DEC_SEQ // TM
    qblk = pl.BlockSpec((TM, D), lambda b, t: (P_TILES + b * nq + t, 0))
    kblk = pl.BlockSpec((DEC_SEQ, D), lambda b, t: (T_P // DEC_SEQ + b, 0))
    cblk = pl.BlockSpec((None, None, PAST_LEN, D), lambda b, t: (b, layer, 0, 0))
    small = lambda s: pl.BlockSpec(s, lambda b, t: (0, 0))
    return pl.pallas_call(
        functools.partial(_attn_lat_kernel, lam_init=lam_init),
        grid=(DEC_BATCH, nq),
        in_specs=[qblk, kblk, kblk, cblk, cblk, small((8, LANES)), small((1, LANES))],
        out_specs=pl.BlockSpec((TM, D), lambda b, t: (b * nq + t, 0)),
        out_shape=jax.ShapeDtypeStruct((T_S, D), bf16),
        scratch_shapes=[pltpu.VMEM((DEC_SEQ, D), bf16), pltpu.VMEM((DEC_SEQ, D), bf16),
                        pltpu.VMEM((PAST_LEN, D), bf16), pltpu.VMEM((PAST_LEN, D), bf16)],
        compiler_params=_cparams(("arbitrary", "arbitrary")),
        name="attn_lat",
    )(q, k, v, ck, cv, lq, g)


def _log_decay(lr, w_ref, b_ref):
    z = _dot3(lr, w_ref[...]) + b_ref[...]
    return -(jnp.maximum(-z, 0.0) + jnp.log(1.0 + jnp.exp(-jnp.abs(z)))) / GATE_NORM


def _gla_direction(q_ref, k_ref, v3b, g, st0, reverse, b_ref, att_ref, sall_ref):
    tb = g.shape[0]
    n = tb // CHUNK
    scale = DK_B ** -0.5
    ri = lax.broadcasted_iota(i32, (CHUNK, CHUNK), 0)
    ci = lax.broadcasted_iota(i32, (CHUNK, CHUNK), 1)
    keep = (ci >= ri) if reverse else (ci <= ri)
    tri = jnp.broadcast_to(jnp.where(keep, 1.0, 0.0).astype(bf16)[None], (n, CHUNK, CHUNK))

    g3 = g.reshape(n, CHUNK, DK_B)
    ghi, glo = _split(g3)
    bdot = lambda a, b: lax.dot_general(a, b, (((2,), (1,)), ((0,), (0,))), preferred_element_type=f32)
    b = bdot(tri, ghi) + bdot(tri, glo)
    tot = b[:, 0:1, :] if reverse else b[:, CHUNK - 1:CHUNK, :]
    q3 = q_ref[...].astype(f32).reshape(n, CHUNK, DK_B) * scale
    k3 = k_ref[...].astype(f32).reshape(n, CHUNK, DK_B)

    qs = (q3 * jnp.exp(b)).astype(bf16)
    kl = (k3 * jnp.exp(tot - b)).astype(bf16)
    kn = (k3 * jnp.exp(jnp.minimum(-b, 80.0))).astype(bf16)
    att = lax.dot_general(qs, kn, (((2,), (2,)), ((0,), (0,))), preferred_element_type=f32)
    att_ref[...] = jnp.where(keep[None], att, 0.0)

    b_ref[...] = b
    need_fix = jnp.min(tot) < FAST_DECAY_LIMIT

    @pl.when(need_fix)
    def _():
        lane = lax.broadcasted_iota(i32, (CHUNK, CHUNK), 1)

        def chunk_body(c, carry):
            bc = b_ref[c]
            row0 = pl.multiple_of(c * CHUNK, CHUNK)
            tot_c = bc[0:1] if reverse else bc[CHUNK - 1:CHUNK]

            @pl.when(jnp.min(tot_c) < FAST_DECAY_LIMIT)
            def _():
                qc = q_ref[pl.ds(row0, CHUNK), :].astype(f32) * scale
                kc = k_ref[pl.ds(row0, CHUNK), :].astype(f32)
                row = lax.broadcasted_iota(i32, (CHUNK, DK_B), 0)

                def jbody(j, acc):
                    bj = b_ref[c, pl.ds(j, 1), :]
                    kj = jnp.sum(jnp.where(row == j, kc, 0.0), axis=0, keepdims=True)
                    e = jnp.exp(jnp.minimum(bc - bj, 0.0))
                    col = jnp.sum(qc * e * kj, axis=-1, keepdims=True)
                    return acc + jnp.where(lane == j, col, 0.0)

                acc = lax.fori_loop(0, CHUNK, jbody, jnp.zeros((CHUNK, CHUNK), f32))
                att_ref[c] = jnp.where(keep, acc, 0.0)

            return carry

        lax.fori_loop(0, n, chunk_body, 0)

    u = lax.dot_general(v3b, kl, (((1,), (1,)), ((0,), (0,))), preferred_element_type=f32)
    dec = jnp.exp(tot)
    st = st0
    order = range(n - 1, -1, -1) if reverse else range(n)
    for c in order:
        sall_ref[c] = st
        st = st * dec[c] + u[c]
    inter = lax.dot_general(qs, sall_ref[...].astype(bf16), (((2,), (2,)), ((0,), (0,))),
                            preferred_element_type=f32)
    intra = lax.dot_general(att_ref[...].astype(bf16), v3b, (((2,), (1,)), ((0,), (0,))),
                            preferred_element_type=f32)
    return (inter + intra).reshape(tb, DV_B), st


def _gla_kernel(*refs, has_init, hp):
    if has_init:
        (q_ref, k_ref, v_ref, bg_ref, lr_ref, wf_ref, wb_ref, bf_ref, bb_ref, gn_ref, sf0_ref, sb0_ref,
         o_ref, b_ref, att_ref, sall_ref) = refs
    else:
        (q_ref, k_ref, v_ref, bg_ref, lr_ref, wf_ref, wb_ref, bf_ref, bb_ref, gn_ref,
         o_ref, sf_ref, sb_ref, b_ref, att_ref, sall_ref) = refs
    tb = q_ref.shape[0]
    n = tb // CHUNK
    lr = lr_ref[...]
    gn = gn_ref[...]
    for h in range(hp):
        kc = slice(h * DK_B, (h + 1) * DK_B)
        vc = slice(h * DV_B, (h + 1) * DV_B)
        if has_init:
            stf0 = sf0_ref[...].T
            stb0 = sb0_ref[...].T
        else:
            stf0 = jnp.zeros((DV_B, DK_B), f32)
            stb0 = stf0
        qh = q_ref.at[:, kc]
        kh = k_ref.at[:, kc]
        v3b = v_ref[:, vc].reshape(n, CHUNK, DV_B).astype(bf16)
        gf = _log_decay(lr, wf_ref.at[:, kc], bf_ref.at[:, kc])
        of, stf = _gla_direction(qh, kh, v3b, gf, stf0, False, b_ref.at[h, 0], att_ref.at[h, 0],
                                 sall_ref.at[h, 0])
        gb = _log_decay(lr, wb_ref.at[:, kc], bb_ref.at[:, kc])
        ob, stb = _gla_direction(qh, kh, v3b, gb, stb0, True, b_ref.at[h, 1], att_ref.at[h, 1],
                                 sall_ref.at[h, 1])
        o = of + ob
        o = o * lax.rsqrt(jnp.mean(o * o, axis=-1, keepdims=True) + EPS) * gn
        o_ref[:, vc] = (o * _silu(bg_ref[:, vc].astype(f32))).astype(bf16)
        if not has_init:
            sf_ref[h] = stf.T
            sb_ref[h] = stb.T


def _gla(bq, bk, bv, bg, lr, wf, wb, bfv, bbv, gn, tb, row_blk0, nb, states=None, layer=0):
    has_init = states is not None
    hp = 1 if has_init else N_HEADS_B
    n = tb // CHUNK
    rows = lambda w: pl.BlockSpec((tb, hp * w), lambda b, h: (row_blk0 + b, h))
    lrs = pl.BlockSpec((tb, LANES), lambda b, h: (row_blk0 + b, 0))
    wsp = pl.BlockSpec((LANES, hp * DK_B), lambda b, h: (0, h))
    bsp = pl.BlockSpec((1, hp * DK_B), lambda b, h: (0, h))
    gsp = pl.BlockSpec((1, DV_B), lambda b, h: (0, 0))
    in_specs = [rows(DK_B), rows(DK_B), rows(DV_B), rows(DV_B), lrs, wsp, wsp, bsp, bsp, gsp]
    args = [bq, bk, bv, bg, lr, wf, wb, bfv, bbv, gn]
    osp = pl.BlockSpec((tb, hp * DV_B), lambda b, h: (b, h))
    oshape = jax.ShapeDtypeStruct((nb * tb, D), bf16)
    if has_init:
        ssp = pl.BlockSpec((None, None, None, DK_B, DV_B), lambda b, h: (b, layer, h, 0, 0))
        in_specs += [ssp, ssp]
        args += list(states)
        out_specs, out_shape = osp, oshape
    else:
        ssp = pl.BlockSpec((None, hp, DK_B, DV_B), lambda b, h: (b, h, 0, 0))
        sshape = jax.ShapeDtypeStruct((nb, N_HEADS_B, DK_B, DV_B), f32)
        out_specs, out_shape = [osp, ssp, ssp], [oshape, sshape, sshape]
    return pl.pallas_call(
        functools.partial(_gla_kernel, has_init=has_init, hp=hp),
        grid=(nb, N_HEADS_B // hp),
        in_specs=in_specs,
        out_specs=out_specs,
        out_shape=out_shape,
        scratch_shapes=[pltpu.VMEM((hp, 2, n, CHUNK, DK_B), f32), pltpu.VMEM((hp, 2, n, CHUNK, CHUNK), f32),
                        pltpu.VMEM((hp, 2, n, DV_B, DK_B), f32)],
        compiler_params=_cparams(("arbitrary", "arbitrary")),
        name="gla_lat" if has_init else "gla_ctx",
    )(*args)


def _merge_kernel(x_ref, oap_ref, oas_ref, ogp_ref, ogs_ref, ga_ref, gb_ref, mod_ref, wa_ref, wb_ref,
                  wo_ref, g_ref, rw_ref, rb_ref,
                  x1_ref, ht_ref, idx_ref, gate_ref, rank_ref, cnt_ref, base_ref):
    i = pl.program_id(0)

    @pl.when((i == 0) | (i == P_TILES))
    def _():
        base_ref[...] = jnp.zeros_like(base_ref)

    is_ctx = i < P_TILES
    oa = jnp.where(is_ctx, oap_ref[...], oas_ref[...])
    og = jnp.where(is_ctx, ogp_ref[...], ogs_ref[...])
    ya = _dot(oa, wa_ref[...])
    yb = _dot(og, wb_ref[...])
    merged = jax.nn.sigmoid(ga_ref[...].astype(f32)) * ya + jax.nn.sigmoid(gb_ref[...].astype(f32)) * yb
    mix = _dot(merged.astype(bf16), wo_ref[...])
    m = mod_ref[pl.ds(_mod_row(i), 1), :]
    x1 = x_ref[...] + m[:, 2 * D:3 * D] * mix
    x1_ref[...] = x1
    h = x1 * lax.rsqrt(jnp.mean(x1 * x1, axis=-1, keepdims=True) + EPS) * g_ref[...]
    h = h * (1.0 + m[:, 4 * D:5 * D]) + m[:, 3 * D:4 * D]
    for s in range(SUB):
        ht_ref[pl.ds(s, TM, stride=SUB), :] = h[:, s * LANES:(s + 1) * LANES]

    lg = _dot3_nt(rw_ref[...], h) + rb_ref[...][:, 0:1]
    eid = lax.broadcasted_iota(i32, (N_EXPERTS, TM), 0)
    vals, sels, idxs = [], [], []
    for _ in range(TOP_K):
        mx = jnp.max(lg, axis=0, keepdims=True)
        ik = jnp.min(jnp.where(lg == mx, eid, N_EXPERTS), axis=0, keepdims=True)
        sel = eid == ik
        vals.append(mx)
        idxs.append(ik)
        sels.append(sel)
        lg = jnp.where(sel, -jnp.inf, lg)
    es = [jnp.exp(v - vals[0]) for v in vals]
    den = es[0] + es[1] + es[2] + es[3]
    onehot = jnp.where(sels[0] | sels[1] | sels[2] | sels[3], 1.0, 0.0)
    tr = lax.broadcasted_iota(i32, (TM, TM), 0)
    tc = lax.broadcasted_iota(i32, (TM, TM), 1)
    before = jnp.where(tr < tc, 1.0, 0.0).astype(bf16)
    pos = base_ref[:, 0:1] + _dot(onehot.astype(bf16), before)
    zrow_i = jnp.zeros((8 - TOP_K, TM), i32)
    ranks = [jnp.sum(jnp.where(s, pos, 0.0), axis=0, keepdims=True).astype(i32) for s in sels]
    idx_ref[...] = jnp.concatenate(idxs + [zrow_i], axis=0)
    rank_ref[...] = jnp.concatenate(ranks + [zrow_i], axis=0)
    gate_ref[...] = jnp.concatenate([e / den for e in es] + [jnp.zeros((8 - TOP_K, TM), f32)], axis=0)
    base = base_ref[...] + jnp.sum(onehot, axis=1, keepdims=True)
    base_ref[...] = base
    cnt_ref[...] = base.astype(i32)


def _merge(x, oap, oas, ogp, ogs, ga, gb, mod, wa, wb, wo, g, rw, rb):
    tile = lambda w: pl.BlockSpec((TM, w), lambda i: (i, 0))
    ctx = pl.BlockSpec((TM, D), lambda i: (jnp.minimum(i, P_TILES - 1), 0))
    lat = pl.BlockSpec((TM, D), lambda i: (jnp.maximum(i - P_TILES, 0), 0))
    res = lambda shape: pl.BlockSpec(shape, lambda i: (0, 0), pipeline_mode=pl.Buffered(1))
    meta = pl.BlockSpec((8, TM), lambda i: (0, i))
    return pl.pallas_call(
        _merge_kernel,
        grid=(N_TILES,),
        in_specs=[tile(D), ctx, lat, ctx, lat, tile(D), tile(D), res((8, 6 * D)), res((D, D)), res((D, D)),
                  res((D, D)), res((1, D)), res((N_EXPERTS, D)), res((N_EXPERTS, LANES))],
        out_specs=[tile(D), pl.BlockSpec((TM * SUB, LANES), lambda i: (i, 0)), meta, meta, meta,
                   pl.BlockSpec((N_EXPERTS, LANES), lambda i: (jnp.where(i < P_TILES, 0, 1), 0))],
        out_shape=[jax.ShapeDtypeStruct((T, D), f32), jax.ShapeDtypeStruct((T * SUB, LANES), f32),
                   jax.ShapeDtypeStruct((8, T), i32), jax.ShapeDtypeStruct((8, T), f32),
                   jax.ShapeDtypeStruct((8, T), i32), jax.ShapeDtypeStruct((2 * N_EXPERTS, LANES), i32)],
        scratch_shapes=[pltpu.VMEM((N_EXPERTS, LANES), f32)],
        compiler_params=_cparams(("arbitrary",)),
        name="merge_route",
    )(x, oap, oas, ogp, ogs, ga, gb, mod, wa, wb, wo, g, rw, rb)


def _experts_kernel(blk_e_ref, nxt_e_ref, nused_ref, src_ref, gate_ref, h_hbm, wgu_hbm, bgu_ref, wdn_hbm,
                    bdn_ref, out_hbm, hv, acc, wgu_st, wdn_st, wgu_bf, wdn_bf, xg, wsem, iosem, *,
                    layer, tok0):
    i = pl.program_id(0)
    n_used = nused_ref[0]

    @pl.when(i < n_used)
    def _():
        _experts_step(i, n_used, blk_e_ref, nxt_e_ref, src_ref, gate_ref, h_hbm, wgu_hbm, bgu_ref, wdn_hbm,
                      bdn_ref, out_hbm, hv, acc, wgu_st, wdn_st, wgu_bf, wdn_bf, xg, wsem, iosem, layer,
                      tok0)


def _experts_step(i, n_used, blk_e_ref, nxt_e_ref, src_ref, gate_ref, h_hbm, wgu_hbm, bgu_ref, wdn_hbm,
                  bdn_ref, out_hbm, hv, acc, wgu_st, wdn_st, wgu_bf, wdn_bf, xg, wsem, iosem, layer, tok0):
    t_str = hv.shape[0] // SUB - 8

    def weight_copies(e):
        return (pltpu.make_async_copy(wgu_hbm.at[layer, e], wgu_st, wsem.at[0]),
                pltpu.make_async_copy(wdn_hbm.at[layer, e], wdn_st, wsem.at[1]))

    e = blk_e_ref[i]

    @pl.when(i == 0)
    def _():
        for cp in weight_copies(e):
            cp.start()
        cp = pltpu.make_async_copy(h_hbm.at[pl.ds(tok0 * SUB, t_str * SUB)], hv.at[pl.ds(0, t_str * SUB)],
                                   iosem.at[0])
        cp.start()
        acc[...] = jnp.zeros_like(acc)
        hv[t_str * SUB:, :] = jnp.zeros((8 * SUB, LANES), f32)
        cp.wait()

    @pl.when((i == 0) | (e != blk_e_ref[jnp.maximum(i - 1, 0)]))
    def _():
        for cp in weight_copies(e):
            cp.wait()
        wgu_bf[...] = wgu_st[...].astype(bf16)
        wdn_bf[...] = wdn_st[...].astype(bf16)
        en = nxt_e_ref[e]

        @pl.when(en >= 0)
        def _():
            for cp in weight_copies(en):
                cp.start()

    for r in range(MOE_ROWS):
        o = pl.multiple_of(src_ref[0, 0, r] * SUB, SUB)
        xg[r * SUB:(r + 1) * SUB, :] = hv[pl.ds(o, SUB), :]
    subs = range(0, MOE_ROWS, MOE_SUB_ROWS)
    xbs = [jnp.concatenate([xg[pl.ds(r_lo * SUB + s, MOE_SUB_ROWS, stride=SUB), :] for s in range(SUB)],
                           axis=1).astype(bf16) for r_lo in subs]
    gus = [_dot(xb, wgu_bf[...]) + bgu_ref[...] for xb in xbs]
    acts = []
    for gu in gus:
        gt = jnp.minimum(gu[:, :MOE_FF], SWIGLU_LIMIT)
        up = jnp.clip(gu[:, MOE_FF:], -SWIGLU_LIMIT, SWIGLU_LIMIT)
        acts.append(((up + 1.0) * gt * jax.nn.sigmoid(SWIGLU_ALPHA * gt)).astype(bf16))
    ys = [_dot(act, wdn_bf[...]) + bdn_ref[...] for act in acts]
    yb = xg
    for r_lo, y in zip(subs, ys):
        for s in range(SUB):
            yb[pl.ds(r_lo * SUB + s, MOE_SUB_ROWS, stride=SUB), :] = y[:, s * LANES:(s + 1) * LANES]
    group = 16
    for r0 in range(0, MOE_ROWS, group):
        offs = [pl.multiple_of(src_ref[0, 0, r] * SUB, SUB) for r in range(r0, r0 + group)]
        olds = [acc[pl.ds(o, SUB), :] for o in offs]
        for j, o in enumerate(offs):
            r = r0 + j
            acc[pl.ds(o, SUB), :] = olds[j] + yb[r * SUB:(r + 1) * SUB, :] * gate_ref[0, 0, r]

    @pl.when(i == n_used - 1)
    def _():
        cp = pltpu.make_async_copy(acc.at[pl.ds(0, t_str * SUB)], out_hbm, iosem.at[0])
        cp.start()
        cp.wait()


def _experts(blk_e, nxt_e, n_used, row_src, row_gate, ht, wgu, bgu, wdn, bdn, layer, tok0, t_str):
    n_blocks = row_src.shape[0] // MOE_ROWS
    src3 = row_src.reshape(n_blocks, 1, MOE_ROWS)
    gate3 = row_gate.reshape(n_blocks, 1, MOE_ROWS)
    per_block = pl.BlockSpec((1, 1, MOE_ROWS), lambda i, be, ne, nu: (i, 0, 0), memory_space=pltpu.SMEM)
    grid_spec = pltpu.PrefetchScalarGridSpec(
        num_scalar_prefetch=3,
        grid=(n_blocks,),
        in_specs=[
            per_block,
            per_block,
            pl.BlockSpec(memory_space=pl.ANY),
            pl.BlockSpec(memory_space=pl.ANY),
            pl.BlockSpec((None, None, 1, 2 * MOE_FF), lambda i, be, ne, nu: (layer, be[i], 0, 0)),
            pl.BlockSpec(memory_space=pl.ANY),
            pl.BlockSpec((None, None, 1, D), lambda i, be, ne, nu: (layer, be[i], 0, 0)),
        ],
        out_specs=pl.BlockSpec(memory_space=pl.ANY),
        scratch_shapes=[
            pltpu.VMEM(((t_str + 8) * SUB, LANES), f32),
            pltpu.VMEM(((t_str + 8) * SUB, LANES), f32),
            pltpu.VMEM((D, 2 * MOE_FF), f32),
            pltpu.VMEM((MOE_FF, D), f32),
            pltpu.VMEM((D, 2 * MOE_FF), bf16),
            pltpu.VMEM((MOE_FF, D), bf16),
            pltpu.VMEM((MOE_ROWS * SUB, LANES), f32),
            pltpu.SemaphoreType.DMA((2,)),
            pltpu.SemaphoreType.DMA((1,)),
        ],
    )
    return pl.pallas_call(
        functools.partial(_experts_kernel, layer=layer, tok0=tok0),
        grid_spec=grid_spec,
        out_shape=jax.ShapeDtypeStruct((t_str * SUB, LANES), f32),
        compiler_params=_cparams(("arbitrary",), _EXPERTS_VMEM_LIMIT),
        name="experts",
    )(blk_e, nxt_e, n_used, src3, gate3, ht, wgu, bgu.reshape(DEPTH, N_EXPERTS, 1, 2 * MOE_FF),
      wdn, bdn.reshape(DEPTH, N_EXPERTS, 1, D))


def _combine_kernel(x1_ref, moep_ref, moes_ref, mod_ref, fg_ref, x2_ref, yn_ref):
    i = pl.program_id(0)
    rows = lambda ref: jnp.concatenate([ref[pl.ds(s, TM, stride=SUB), :] for s in range(SUB)], axis=1)
    moe = jnp.where(i < P_TILES, rows(moep_ref), rows(moes_ref))
    m = mod_ref[pl.ds(_mod_row(i), 1), :]
    x2 = x1_ref[...] + m[:, 5 * D:6 * D] * moe
    x2_ref[...] = x2
    yn_ref[...] = x2 * lax.rsqrt(jnp.mean(x2 * x2, axis=-1, keepdims=True) + EPS) * fg_ref[...]


def _combine(x1, moep, moes, mod, fg):
    tile = pl.BlockSpec((TM, D), lambda i: (i, 0))
    res = lambda shape: pl.BlockSpec(shape, lambda i: (0, 0), pipeline_mode=pl.Buffered(1))
    ctx = pl.BlockSpec((TM * SUB, LANES), lambda i: (jnp.minimum(i, P_TILES - 1), 0))
    lat = pl.BlockSpec((TM * SUB, LANES), lambda i: (jnp.maximum(i - P_TILES, 0), 0))
    return pl.pallas_call(
        _combine_kernel,
        grid=(N_TILES,),
        in_specs=[tile, ctx, lat, res((8, 6 * D)), res((1, D))],
        out_specs=[tile, tile],
        out_shape=[jax.ShapeDtypeStruct((T, D), f32), jax.ShapeDtypeStruct((T, D), f32)],
        compiler_params=_cparams(("arbitrary",)),
        name="combine",
    )(x1, moep, moes, mod, fg)


def _rope_tables():
    nf = HD_A // 4
    inv = ROPE_BASE ** (-jnp.arange(nf, dtype=f32) / nf)
    n = jnp.arange(DEC_SEQ)
    row = (n // GRID_W).astype(f32)[:, None] * inv
    col = (n % GRID_W).astype(f32)[:, None] * inv
    cos64 = jnp.concatenate([jnp.cos(row), jnp.cos(row), jnp.cos(col), jnp.cos(col)], axis=-1)
    sin64 = jnp.concatenate([-jnp.sin(row), jnp.sin(row), -jnp.sin(col), jnp.sin(col)], axis=-1)
    cos_s = jnp.tile(jnp.tile(cos64, (1, 2)), (DEC_BATCH, 1))
    sin_s = jnp.tile(jnp.tile(sin64, (1, 2)), (DEC_BATCH, 1))
    cos = jnp.concatenate([jnp.ones((T_P, LANES), f32), cos_s], axis=0)
    sin = jnp.concatenate([jnp.zeros((T_P, LANES), f32), sin_s], axis=0)
    return cos, sin


def _routing_tables(idx, rank, gate, counts):
    t_str = idx.shape[1]
    n_blocks = (t_str * TOP_K + N_EXPERTS * (MOE_ROWS - 1)) // MOE_ROWS + 1
    padded = (counts + MOE_ROWS - 1) // MOE_ROWS * MOE_ROWS
    pad_end = jnp.cumsum(padded)
    pad_start = pad_end - padded
    eids = jnp.arange(N_EXPERTS, dtype=i32)
    start_of = jnp.sum(jnp.where(idx[..., None] == eids, pad_start, 0), axis=-1)
    dest = (start_of + rank).reshape(-1)
    inv = jnp.full((n_blocks * MOE_ROWS,), -1, i32).at[dest].set(jnp.arange(TOP_K * t_str, dtype=i32))
    valid = inv >= 0
    safe = jnp.maximum(inv, 0)
    row_src = jnp.where(valid, safe % t_str, t_str).astype(i32)
    row_gate = jnp.where(valid, gate.reshape(-1)[safe], 0.0)
    starts = jnp.arange(n_blocks, dtype=i32) * MOE_ROWS
    blk_e = jnp.sum((pad_end[None, :] <= starts[:, None]).astype(i32), axis=1)
    present = counts > 0
    blk_e = jnp.minimum(blk_e, jnp.max(jnp.where(present, eids, 0))).astype(i32)
    later = (eids[None, :] > eids[:, None]) & present[None, :]
    nxt = jnp.min(jnp.where(later, eids[None, :], N_EXPERTS), axis=1)
    nxt_e = jnp.where(nxt < N_EXPERTS, nxt, -1).astype(i32)
    n_used = (pad_end[-1:] // MOE_ROWS).astype(i32)
    return blk_e, nxt_e, n_used, row_src, row_gate


def kernel(x_prompt, x_sample, cache_k, cache_v, state_fwd, state_bwd, c, c_ctx, ada_w, ada_b, norm_mix_g,
           norm_ffn_g, w_in, diff_lambda, diff_subln_g, gla_gk_w, gla_gk_b, gla_norm_g, w_branch_a,
           w_branch_b, w_out, router_w, router_b, expert_w_gu, expert_b_gu, expert_w_down, expert_b_down,
           final_norm_g):
    x = jnp.concatenate([x_prompt.reshape(T_P, D), x_sample.reshape(T_S, D)], axis=0)
    cvec = jnp.concatenate([c_ctx[None], c, jnp.zeros((3, D), f32)], axis=0)
    mods = _ada_mod(cvec, ada_w, ada_b)
    cos, sin = _rope_tables()
    ck = cache_k.reshape(DEC_BATCH, DEPTH, PAST_LEN, D)
    cv = cache_v.reshape(DEC_BATCH, DEPTH, PAST_LEN, D)
    main_w = 6 * D
    lr0 = main_w
    g0 = main_w + 2 * GK_RANK

    ks, vs, sfs, sbs = [], [], [], []
    yn = None
    for l in range(DEPTH):
        lam_init = 0.8 - 0.6 * math.exp(-0.3 * l)
        mod = mods[l]
        wm = w_in[l, :, :main_w].astype(bf16)
        wl = jnp.pad(w_in[l, :, lr0:g0], ((0, 0), (0, LANES - 2 * GK_RANK))).astype(bf16)
        wg = w_in[l, :, g0:].astype(bf16)
        q, k, v, bq, bk, bv, bg, lr, ga, gb = _inproj(x, mod, norm_mix_g[l][None], wm, wg, wl, cos, sin)
        ks.append(k[:T_P].reshape(BATCH, SEQ, D))
        vs.append(v[:T_P].reshape(BATCH, SEQ, D))

        lq = jnp.pad(diff_lambda[l], ((0, 4), (0, LANES - HD_A)))
        sg = diff_subln_g[l][None]
        oap = _attn_ctx(q, k, v, lq, sg, lam_init)
        oas = _attn_lat(q, k, v, ck, cv, lq, sg, lam_init, l)

        wf = jnp.pad(gla_gk_w[l, 0], ((0, LANES - GK_RANK), (0, 0)))
        wb = jnp.pad(gla_gk_w[l, 1], ((GK_RANK, LANES - 2 * GK_RANK), (0, 0)))
        bfv = gla_gk_b[l, 0][None]
        bbv = gla_gk_b[l, 1][None]
        gn = gla_norm_g[l][None]
        ogp, sf, sb = _gla(bq, bk, bv, bg, lr, wf, wb, bfv, bbv, gn, SEQ, 0, BATCH)
        ogs = _gla(bq, bk, bv, bg, lr, wf, wb, bfv, bbv, gn, DEC_SEQ, T_P // DEC_SEQ, DEC_BATCH,
                   states=(state_fwd, state_bwd), layer=l)
        sfs.append(sf)
        sbs.append(sb)

        rb = jnp.broadcast_to(router_b[l][:, None], (N_EXPERTS, LANES))
        x1, ht, idx8, gate8, rank8, cnt = _merge(
            x, oap, oas, ogp, ogs, ga, gb, mod, w_branch_a[l].astype(bf16), w_branch_b[l].astype(bf16),
            w_out[l].astype(bf16), norm_ffn_g[l][None], router_w[l].T, rb)
        moes = []
        for tok0, t_str, cnt_s in ((0, T_P, cnt[:N_EXPERTS, 0]), (T_P, T_S, cnt[N_EXPERTS:, 0])):
            sl = slice(tok0, tok0 + t_str)
            tables = _routing_tables(idx8[:TOP_K, sl], rank8[:TOP_K, sl], gate8[:TOP_K, sl], cnt_s)
            moes.append(_experts(*tables, ht, expert_w_gu, expert_b_gu, expert_w_down, expert_b_down, l,
                                 tok0, t_str))
        x, yn = _combine(x1, moes[0], moes[1], mod, final_norm_g[None])

    y_prompt = yn[:T_P].reshape(BATCH, SEQ, D)
    y_sample = yn[T_P:].reshape(DEC_BATCH, DEC_SEQ, D)
    new_k = jnp.stack(ks, axis=1).reshape(BATCH, DEPTH, SEQ, N_HEADS_A, 2, HD_A)
    new_v = jnp.stack(vs, axis=1).reshape(BATCH, DEPTH, SEQ, N_HEADS_A, 2 * HD_A)
    new_sf = jnp.stack(sfs, axis=1)
    new_sb = jnp.stack(sbs, axis=1)
    return (y_prompt, y_sample, new_k, new_v, new_sf, new_sb)
```

```python
import functools
import math

import jax
import jax.numpy as jnp
from jax import lax
from jax.experimental import pallas as pl
from jax.experimental.pallas import tpu as pltpu

f32 = jnp.float32
bf16 = jnp.bfloat16
i32 = jnp.int32

D = 1024
BATCH = 16
SEQ = 256
DEPTH = 2
DEC_BATCH = 4
DEC_SEQ = 1024
PAST_LEN = 512
GRID_W = 64
HD_A = 64
N_HEADS_A = 8
ROPE_BASE = 10000.0
N_HEADS_B = 4
DK_B = 128
DV_B = 256
GK_RANK = 16
GATE_NORM = 16.0
CHUNK = 64
N_EXPERTS = 32
TOP_K = 4
MOE_FF = 1024
SWIGLU_LIMIT = 7.0
SWIGLU_ALPHA = 1.702
EPS = 1e-6

T_P = BATCH * SEQ
T_S = DEC_BATCH * DEC_SEQ
T = T_P + T_S
TM = 256
N_TILES = T // TM
P_TILES = T_P // TM
S_TILES_PER_BATCH = DEC_SEQ // TM

MOE_ROWS = 512
MOE_SUB_ROWS = 128
SUB = 8
LANES = 128
FAST_DECAY_LIMIT = -60.0

_VMEM_LIMIT = 56 * 1024 * 1024
_EXPERTS_VMEM_LIMIT = 60 * 1024 * 1024


def _cparams(sem, vmem=_VMEM_LIMIT):
    return pltpu.CompilerParams(dimension_semantics=sem, vmem_limit_bytes=vmem)


def _dot(a, b):
    return jnp.dot(a, b, preferred_element_type=f32)


def _dot_nt(a, b):
    return lax.dot_general(a, b, (((1,), (1,)), ((), ())), preferred_element_type=f32)


def _split(x):
    hi = x.astype(bf16)
    lo = (x - hi.astype(f32)).astype(bf16)
    return hi, lo


def _dot3(a, b):
    ah, al = _split(a)
    bh, bl = _split(b)
    return _dot(ah, bh) + _dot(ah, bl) + _dot(al, bh)


def _dot3_nt(a, b):
    ah, al = _split(a)
    bh, bl = _split(b)
    return _dot_nt(ah, bh) + _dot_nt(ah, bl) + _dot_nt(al, bh)


def _silu(x):
    return x * jax.nn.sigmoid(x)


def _mod_row(i):
    return jnp.where(i < P_TILES, 0, 1 + lax.shift_right_logical(jnp.maximum(i - P_TILES, 0), 2))


def _ada_kernel(c_ref, w_ref, b_ref, o_ref):
    o_ref[0] = _dot3(_silu(c_ref[...]), w_ref[0]) + b_ref[0]


def _ada_mod(cvec, ada_w, ada_b):
    tn = 1536
    return pl.pallas_call(
        _ada_kernel,
        grid=(DEPTH, 6 * D // tn),
        in_specs=[
            pl.BlockSpec((8, D), lambda l, j: (0, 0)),
            pl.BlockSpec((1, D, tn), lambda l, j: (l, 0, j)),
            pl.BlockSpec((1, 1, tn), lambda l, j: (l, 0, j)),
        ],
        out_specs=pl.BlockSpec((1, 8, tn), lambda l, j: (l, 0, j)),
        out_shape=jax.ShapeDtypeStruct((DEPTH, 8, 6 * D), f32),
        compiler_params=_cparams(("arbitrary", "arbitrary")),
        name="ada_mod",
    )(cvec, ada_w, ada_b.reshape(DEPTH, 1, 6 * D))


def _inproj_kernel(x_ref, mod_ref, g_ref, wm_ref, wg_ref, wl_ref, cos_ref, sin_ref,
                   q_ref, k_ref, v_ref, bq_ref, bk_ref, bv_ref, bg_ref, lr_ref, ga_ref, gb_ref):
    i = pl.program_id(0)
    m = mod_ref[pl.ds(_mod_row(i), 1), :]
    shift = m[:, 0:D]
    scale = m[:, D:2 * D]
    x = x_ref[...]
    h = x * lax.rsqrt(jnp.mean(x * x, axis=-1, keepdims=True) + EPS) * g_ref[...]
    hb = (h * (1.0 + scale) + shift).astype(bf16)

    cos = cos_ref[...]
    sin = sin_ref[...]
    lane = lax.broadcasted_iota(i32, (TM, LANES), 1)
    lo_mask = (lane & 31) < 16

    def rope(a):
        partner = jnp.where(lo_mask, pltpu.roll(a, LANES - 16, 1), pltpu.roll(a, 16, 1))
        return a * cos + partner * sin

    aq = _dot(hb, wm_ref[:, 0:D])
    for c in range(D // LANES):
        q_ref[:, c * LANES:(c + 1) * LANES] = rope(aq[:, c * LANES:(c + 1) * LANES]).astype(bf16)
    ak = _dot(hb, wm_ref[:, D:2 * D])
    for c in range(D // LANES):
        k_ref[:, c * LANES:(c + 1) * LANES] = rope(ak[:, c * LANES:(c + 1) * LANES])
    v_ref[...] = _dot(hb, wm_ref[:, 2 * D:3 * D])
    bq_ref[...] = _dot(hb, wm_ref[:, 3 * D:3 * D + 512]).astype(bf16)
    bk_ref[...] = _dot(hb, wm_ref[:, 3 * D + 512:4 * D]).astype(bf16)
    bv_ref[...] = _dot(hb, wm_ref[:, 4 * D:5 * D]).astype(bf16)
    bg_ref[...] = _dot(hb, wm_ref[:, 5 * D:6 * D]).astype(bf16)
    lr_ref[...] = _dot(hb, wl_ref[...])
    ga_ref[...] = _dot(hb, wg_ref[:, 0:D]).astype(bf16)
    gb_ref[...] = _dot(hb, wg_ref[:, D:2 * D]).astype(bf16)


def _inproj(x, mod, g, wm, wg, wl, cos, sin):
    tile = lambda w: pl.BlockSpec((TM, w), lambda i: (i, 0))
    res = lambda shape: pl.BlockSpec(shape, lambda i: (0, 0), pipeline_mode=pl.Buffered(1))
    widths = (D, D, D, 512, 512, D, D, LANES, D, D)
    dtypes = (bf16, f32, f32, bf16, bf16, bf16, bf16, f32, bf16, bf16)
    return pl.pallas_call(
        _inproj_kernel,
        grid=(N_TILES,),
        in_specs=[tile(D), res((8, 6 * D)), res((1, D)), res((D, 6 * D)), res((D, 2 * D)),
                  res((D, LANES)), tile(LANES), tile(LANES)],
        out_specs=[tile(w) for w in widths],
        out_shape=[jax.ShapeDtypeStruct((T, w), dt) for w, dt in zip(widths, dtypes)],
        compiler_params=_cparams(("arbitrary",)),
        name="inproj",
    )(x, mod, g, wm, wg, wl, cos, sin)


def _lam_from(lq_ref, lam_init):
    lq = lq_ref[...]
    s01 = jnp.sum(lq[0:1] * lq[1:2], axis=-1, keepdims=True)
    s23 = jnp.sum(lq[2:3] * lq[3:4], axis=-1, keepdims=True)
    return jnp.exp(s01) - jnp.exp(s23) + lam_init


def _subln(o, g, lam_init):
    o = o * lax.rsqrt(jnp.mean(o * o, axis=-1, keepdims=True) + EPS)
    return o * g * (1.0 - lam_init)


def _diff_heads(q_ref, segs, lq_ref, g_ref, o_ref, lam_init):
    lam = _lam_from(lq_ref, lam_init)
    g = g_ref[...]
    lo = lax.broadcasted_iota(i32, (q_ref.shape[0], LANES), 1) < HD_A
    for h in range(N_HEADS_A):
        cols = slice(h * LANES, (h + 1) * LANES)
        q = q_ref[:, cols] * (HD_A ** -0.5)
        kv = [(k_ref[:, cols], v_ref[:, cols]) for k_ref, v_ref in segs]
        outs = []
        for qm in (jnp.where(lo, q, 0), jnp.where(lo, 0, q)):
            ss = [_dot_nt(qm, kb) for kb, _ in kv]
            mx = functools.reduce(jnp.maximum, [jnp.max(s, axis=-1, keepdims=True) for s in ss])
            ps = [jnp.exp(s - mx) for s in ss]
            l = functools.reduce(jnp.add, [jnp.sum(p, axis=-1, keepdims=True) for p in ps])
            pv = functools.reduce(jnp.add, [_dot(p.astype(bf16), vb) for p, (_, vb) in zip(ps, kv)])
            outs.append(pv / l)
        o = outs[0] - lam * outs[1]
        o_ref[:, cols] = _subln(o, g, lam_init).astype(bf16)


def _attn_ctx_kernel(q_ref, k_ref, v_ref, lq_ref, g_ref, o_ref, kb_ref, vb_ref, *, lam_init):
    kb_ref[...] = k_ref[...].astype(bf16)
    vb_ref[...] = v_ref[...].astype(bf16)
    _diff_heads(q_ref, [(kb_ref, vb_ref)], lq_ref, g_ref, o_ref, lam_init)


def _attn_lat_kernel(q_ref, k_ref, v_ref, ck_ref, cv_ref, lq_ref, g_ref, o_ref, kb_ref, vb_ref, ckb_ref,
                     cvb_ref, *, lam_init):
    @pl.when(pl.program_id(1) == 0)
    def _():
        kb_ref[...] = k_ref[...].astype(bf16)
        vb_ref[...] = v_ref[...].astype(bf16)
        ckb_ref[...] = ck_ref[...].astype(bf16)
        cvb_ref[...] = cv_ref[...].astype(bf16)

    _diff_heads(q_ref, [(ckb_ref, cvb_ref), (kb_ref, vb_ref)], lq_ref, g_ref, o_ref, lam_init)


def _attn_ctx(q, k, v, lq, g, lam_init):
    blk = pl.BlockSpec((SEQ, D), lambda b: (b, 0))
    small = lambda s: pl.BlockSpec(s, lambda b: (0, 0))
    return pl.pallas_call(
        functools.partial(_attn_ctx_kernel, lam_init=lam_init),
        grid=(BATCH,),
        in_specs=[blk, blk, blk, small((8, LANES)), small((1, LANES))],
        out_specs=blk,
        out_shape=jax.ShapeDtypeStruct((T_P, D), bf16),
        scratch_shapes=[pltpu.VMEM((SEQ, D), bf16), pltpu.VMEM((SEQ, D), bf16)],
        compiler_params=_cparams(("arbitrary",)),
        name="attn_ctx",
    )(q, k, v, lq, g)


def _attn_lat(q, k, v, ck, cv, lq, g, lam_init, layer):
    nq = DEC_SEQ // TM
    qblk = pl.BlockSpec((TM, D), lambda b, t: (P_TILES + b * nq + t, 0))
    kblk = pl.BlockSpec((DEC_SEQ, D), lambda b, t: (T_P // DEC_SEQ + b, 0))
    cblk = pl.BlockSpec((None, None, PAST_LEN, D), lambda b, t: (b, layer, 0, 0))
    small = lambda s: pl.BlockSpec(s, lambda b, t: (0, 0))
    return pl.pallas_call(
        functools.partial(_attn_lat_kernel, lam_init=lam_init),
        grid=(DEC_BATCH, nq),
        in_specs=[qblk, kblk, kblk, cblk, cblk, small((8, LANES)), small((1, LANES))],
        out_specs=pl.BlockSpec((TM, D), lambda b, t: (b * nq + t, 0)),
        out_shape=jax.ShapeDtypeStruct((T_S, D), bf16),
        scratch_shapes=[pltpu.VMEM((DEC_SEQ, D), bf16), pltpu.VMEM((DEC_SEQ, D), bf16),
                        pltpu.VMEM((PAST_LEN, D), bf16), pltpu.VMEM((PAST_LEN, D), bf16)],
        compiler_params=_cparams(("arbitrary", "arbitrary")),
        name="attn_lat",
    )(q, k, v, ck, cv, lq, g)


def _log_decay(lr, w_ref, b_ref):
    z = _dot3(lr, w_ref[...]) + b_ref[...]
    return -(jnp.maximum(-z, 0.0) + jnp.log(1.0 + jnp.exp(-jnp.abs(z)))) / GATE_NORM


def _gla_direction(q_ref, k_ref, v3b, g, st0, reverse, b_ref, att_ref, sall_ref):
    tb = g.shape[0]
    n = tb // CHUNK
    scale = DK_B ** -0.5
    ri = lax.broadcasted_iota(i32, (CHUNK, CHUNK), 0)
    ci = lax.broadcasted_iota(i32, (CHUNK, CHUNK), 1)
    keep = (ci >= ri) if reverse else (ci <= ri)
    tri = jnp.broadcast_to(jnp.where(keep, 1.0, 0.0).astype(bf16)[None], (n, CHUNK, CHUNK))

    g3 = g.reshape(n, CHUNK, DK_B)
    ghi, glo = _split(g3)
    bdot = lambda a, b: lax.dot_general(a, b, (((2,), (1,)), ((0,), (0,))), preferred_element_type=f32)
    b = bdot(tri, ghi) + bdot(tri, glo)
    tot = b[:, 0:1, :] if reverse else b[:, CHUNK - 1:CHUNK, :]
    q3 = q_ref[...].astype(f32).reshape(n, CHUNK, DK_B) * scale
    k3 = k_ref[...].astype(f32).reshape(n, CHUNK, DK_B)

    qs = (q3 * jnp.exp(b)).astype(bf16)
    kl = (k3 * jnp.exp(tot - b)).astype(bf16)
    kn = (k3 * jnp.exp(jnp.minimum(-b, 80.0))).astype(bf16)
    att = lax.dot_general(qs, kn, (((2,), (2,)), ((0,), (0,))), preferred_element_type=f32)
    att_ref[...] = jnp.where(keep[None], att, 0.0)

    b_ref[...] = b
    need_fix = jnp.min(tot) < FAST_DECAY_LIMIT

    @pl.when(need_fix)
    def _():
        lane = lax.broadcasted_iota(i32, (CHUNK, CHUNK), 1)

        def chunk_body(c, carry):
            bc = b_ref[c]
            row0 = pl.multiple_of(c * CHUNK, CHUNK)
            tot_c = bc[0:1] if reverse else bc[CHUNK - 1:CHUNK]

            @pl.when(jnp.min(tot_c) < FAST_DECAY_LIMIT)
            def _():
                qc = q_ref[pl.ds(row0, CHUNK), :].astype(f32) * scale
                kc = k_ref[pl.ds(row0, CHUNK), :].astype(f32)
                row = lax.broadcasted_iota(i32, (CHUNK, DK_B), 0)

                def jbody(j, acc):
                    bj = b_ref[c, pl.ds(j, 1), :]
                    kj = jnp.sum(jnp.where(row == j, kc, 0.0), axis=0, keepdims=True)
                    e = jnp.exp(jnp.minimum(bc - bj, 0.0))
                    col = jnp.sum(qc * e * kj, axis=-1, keepdims=True)
                    return acc + jnp.where(lane == j, col, 0.0)

                acc = lax.fori_loop(0, CHUNK, jbody, jnp.zeros((CHUNK, CHUNK), f32))
                att_ref[c] = jnp.where(keep, acc, 0.0)

            return carry

        lax.fori_loop(0, n, chunk_body, 0)

    u = lax.dot_general(v3b, kl, (((1,), (1,)), ((0,), (0,))), preferred_element_type=f32)
    dec = jnp.exp(tot)
    st = st0
    order = range(n - 1, -1, -1) if reverse else range(n)
    for c in order:
        sall_ref[c] = st
        st = st * dec[c] + u[c]
    inter = lax.dot_general(qs, sall_ref[...].astype(bf16), (((2,), (2,)), ((0,), (0,))),
                            preferred_element_type=f32)
    intra = lax.dot_general(att_ref[...].astype(bf16), v3b, (((2,), (1,)), ((0,), (0,))),
                            preferred_element_type=f32)
    return (inter + intra).reshape(tb, DV_B), st


def _gla_kernel(*refs, has_init, hp):
    if has_init:
        (q_ref, k_ref, v_ref, bg_ref, lr_ref, wf_ref, wb_ref, bf_ref, bb_ref, gn_ref, sf0_ref, sb0_ref,
         o_ref, b_ref, att_ref, sall_ref) = refs
    else:
        (q_ref, k_ref, v_ref, bg_ref, lr_ref, wf_ref, wb_ref, bf_ref, bb_ref, gn_ref,
         o_ref, sf_ref, sb_ref, b_ref, att_ref, sall_ref) = refs
    tb = q_ref.shape[0]
    n = tb // CHUNK
    lr = lr_ref[...]
    gn = gn_ref[...]
    for h in range(hp):
        kc = slice(h * DK_B, (h + 1) * DK_B)
        vc = slice(h * DV_B, (h + 1) * DV_B)
        if has_init:
            stf0 = sf0_ref[...].T
            stb0 = sb0_ref[...].T
        else:
            stf0 = jnp.zeros((DV_B, DK_B), f32)
            stb0 = stf0
        qh = q_ref.at[:, kc]
        kh = k_ref.at[:, kc]
        v3b = v_ref[:, vc].reshape(n, CHUNK, DV_B).astype(bf16)
        gf = _log_decay(lr, wf_ref.at[:, kc], bf_ref.at[:, kc])
        of, stf = _gla_direction(qh, kh, v3b, gf, stf0, False, b_ref.at[h, 0], att_ref.at[h, 0],
                                 sall_ref.at[h, 0])
        gb = _log_decay(lr, wb_ref.at[:, kc], bb_ref.at[:, kc])
        ob, stb = _gla_direction(qh, kh, v3b, gb, stb0, True, b_ref.at[h, 1], att_ref.at[h, 1],
                                 sall_ref.at[h, 1])
        o = of + ob
        o = o * lax.rsqrt(jnp.mean(o * o, axis=-1, keepdims=True) + EPS) * gn
        o_ref[:, vc] = (o * _silu(bg_ref[:, vc].astype(f32))).astype(bf16)
        if not has_init:
            sf_ref[h] = stf.T
            sb_ref[h] = stb.T


def _gla(bq, bk, bv, bg, lr, wf, wb, bfv, bbv, gn, tb, row_blk0, nb, states=None, layer=0):
    has_init = states is not None
    hp = 1 if has_init else N_HEADS_B
    n = tb // CHUNK
    rows = lambda w: pl.BlockSpec((tb, hp * w), lambda b, h: (row_blk0 + b, h))
    lrs = pl.BlockSpec((tb, LANES), lambda b, h: (row_blk0 + b, 0))
    wsp = pl.BlockSpec((LANES, hp * DK_B), lambda b, h: (0, h))
    bsp = pl.BlockSpec((1, hp * DK_B), lambda b, h: (0, h))
    gsp = pl.BlockSpec((1, DV_B), lambda b, h: (0, 0))
    in_specs = [rows(DK_B), rows(DK_B), rows(DV_B), rows(DV_B), lrs, wsp, wsp, bsp, bsp, gsp]
    args = [bq, bk, bv, bg, lr, wf, wb, bfv, bbv, gn]
    osp = pl.BlockSpec((tb, hp * DV_B), lambda b, h: (b, h))
    oshape = jax.ShapeDtypeStruct((nb * tb, D), bf16)
    if has_init:
        ssp = pl.BlockSpec((None, None, None, DK_B, DV_B), lambda b, h: (b, layer, h, 0, 0))
        in_specs += [ssp, ssp]
        args += list(states)
        out_specs, out_shape = osp, oshape
    else:
        ssp = pl.BlockSpec((None, hp, DK_B, DV_B), lambda b, h: (b, h, 0, 0))
        sshape = jax.ShapeDtypeStruct((nb, N_HEADS_B, DK_B, DV_B), f32)
        out_specs, out_shape = [osp, ssp, ssp], [oshape, sshape, sshape]
    return pl.pallas_call(
        functools.partial(_gla_kernel, has_init=has_init, hp=hp),
        grid=(nb, N_HEADS_B // hp),
        in_specs=in_specs,
        out_specs=out_specs,
        out_shape=out_shape,
        scratch_shapes=[pltpu.VMEM((hp, 2, n, CHUNK, DK_B), f32), pltpu.VMEM((hp, 2, n, CHUNK, CHUNK), f32),
                        pltpu.VMEM((hp, 2, n, DV_B, DK_B), f32)],
        compiler_params=_cparams(("arbitrary", "arbitrary")),
        name="gla_lat" if has_init else "gla_ctx",
    )(*args)


def _merge_kernel(x_ref, oap_ref, oas_ref, ogp_ref, ogs_ref, ga_ref, gb_ref, mod_ref, wa_ref, wb_ref,
                  wo_ref, g_ref, rw_ref, rb_ref,
                  x1_ref, ht_ref, idx_ref, gate_ref, rank_ref, cnt_ref, base_ref):
    i = pl.program_id(0)

    @pl.when((i == 0) | (i == P_TILES))
    def _():
        base_ref[...] = jnp.zeros_like(base_ref)

    is_ctx = i < P_TILES
    oa = jnp.where(is_ctx, oap_ref[...], oas_ref[...])
    og = jnp.where(is_ctx, ogp_ref[...], ogs_ref[...])
    ya = _dot(oa, wa_ref[...])
    yb = _dot(og, wb_ref[...])
    merged = jax.nn.sigmoid(ga_ref[...].astype(f32)) * ya + jax.nn.sigmoid(gb_ref[...].astype(f32)) * yb
    mix = _dot(merged.astype(bf16), wo_ref[...])
    m = mod_ref[pl.ds(_mod_row(i), 1), :]
    x1 = x_ref[...] + m[:, 2 * D:3 * D] * mix
    x1_ref[...] = x1
    h = x1 * lax.rsqrt(jnp.mean(x1 * x1, axis=-1, keepdims=True) + EPS) * g_ref[...]
    h = h * (1.0 + m[:, 4 * D:5 * D]) + m[:, 3 * D:4 * D]
    for s in range(SUB):
        ht_ref[pl.ds(s, TM, stride=SUB), :] = h[:, s * LANES:(s + 1) * LANES]

    lg = _dot3_nt(rw_ref[...], h) + rb_ref[...][:, 0:1]
    eid = lax.broadcasted_iota(i32, (N_EXPERTS, TM), 0)
    vals, sels, idxs = [], [], []
    for _ in range(TOP_K):
        mx = jnp.max(lg, axis=0, keepdims=True)
        ik = jnp.min(jnp.where(lg == mx, eid, N_EXPERTS), axis=0, keepdims=True)
        sel = eid == ik
        vals.append(mx)
        idxs.append(ik)
        sels.append(sel)
        lg = jnp.where(sel, -jnp.inf, lg)
    es = [jnp.exp(v - vals[0]) for v in vals]
    den = es[0] + es[1] + es[2] + es[3]
    onehot = jnp.where(sels[0] | sels[1] | sels[2] | sels[3], 1.0, 0.0)
    tr = lax.broadcasted_iota(i32, (TM, TM), 0)
    tc = lax.broadcasted_iota(i32, (TM, TM), 1)
    before = jnp.where(tr < tc, 1.0, 0.0).astype(bf16)
    pos = base_ref[:, 0:1] + _dot(onehot.astype(bf16), before)
    zrow_i = jnp.zeros((8 - TOP_K, TM), i32)
    ranks = [jnp.sum(jnp.where(s, pos, 0.0), axis=0, keepdims=True).astype(i32) for s in sels]
    idx_ref[...] = jnp.concatenate(idxs + [zrow_i], axis=0)
    rank_ref[...] = jnp.concatenate(ranks + [zrow_i], axis=0)
    gate_ref[...] = jnp.concatenate([e / den for e in es] + [jnp.zeros((8 - TOP_K, TM), f32)], axis=0)
    base = base_ref[...] + jnp.sum(onehot, axis=1, keepdims=True)
    base_ref[...] = base
    cnt_ref[...] = base.astype(i32)


def _merge(x, oap, oas, ogp, ogs, ga, gb, mod, wa, wb, wo, g, rw, rb):
    tile = lambda w: pl.BlockSpec((TM, w), lambda i: (i, 0))
    ctx = pl.BlockSpec((TM, D), lambda i: (jnp.minimum(i, P_TILES - 1), 0))
    lat = pl.BlockSpec((TM, D), lambda i: (jnp.maximum(i - P_TILES, 0), 0))
    res = lambda shape: pl.BlockSpec(shape, lambda i: (0, 0), pipeline_mode=pl.Buffered(1))
    meta = pl.BlockSpec((8, TM), lambda i: (0, i))
    return pl.pallas_call(
        _merge_kernel,
        grid=(N_TILES,),
        in_specs=[tile(D), ctx, lat, ctx, lat, tile(D), tile(D), res((8, 6 * D)), res((D, D)), res((D, D)),
                  res((D, D)), res((1, D)), res((N_EXPERTS, D)), res((N_EXPERTS, LANES))],
        out_specs=[tile(D), pl.BlockSpec((TM * SUB, LANES), lambda i: (i, 0)), meta, meta, meta,
                   pl.BlockSpec((N_EXPERTS, LANES), lambda i: (jnp.where(i < P_TILES, 0, 1), 0))],
        out_shape=[jax.ShapeDtypeStruct((T, D), f32), jax.ShapeDtypeStruct((T * SUB, LANES), f32),
                   jax.ShapeDtypeStruct((8, T), i32), jax.ShapeDtypeStruct((8, T), f32),
                   jax.ShapeDtypeStruct((8, T), i32), jax.ShapeDtypeStruct((2 * N_EXPERTS, LANES), i32)],
        scratch_shapes=[pltpu.VMEM((N_EXPERTS, LANES), f32)],
        compiler_params=_cparams(("arbitrary",)),
        name="merge_route",
    )(x, oap, oas, ogp, ogs, ga, gb, mod, wa, wb, wo, g, rw, rb)


def _experts_kernel(blk_e_ref, nxt_e_ref, blk_nv_ref, nused_ref, src_ref, gate_ref, h_hbm, wgu_hbm, bgu_ref, wdn_hbm,
                    bdn_ref, out_hbm, hv, acc, wgu_st, wdn_st, wgu_bf, wdn_bf, xg, wsem, iosem, *,
                    layer, tok0):
    i = pl.program_id(0)
    n_used = nused_ref[0]

    @pl.when(i < n_used)
    def _():
        _experts_step(i, n_used, blk_e_ref, nxt_e_ref, blk_nv_ref, src_ref, gate_ref, h_hbm, wgu_hbm, bgu_ref, wdn_hbm,
                      bdn_ref, out_hbm, hv, acc, wgu_st, wdn_st, wgu_bf, wdn_bf, xg, wsem, iosem, layer,
                      tok0)


def _experts_step(i, n_used, blk_e_ref, nxt_e_ref, blk_nv_ref, src_ref, gate_ref, h_hbm, wgu_hbm, bgu_ref, wdn_hbm,
                  bdn_ref, out_hbm, hv, acc, wgu_st, wdn_st, wgu_bf, wdn_bf, xg, wsem, iosem, layer, tok0):
    t_str = hv.shape[0] // SUB - 8

    def weight_copies(e):
        return (pltpu.make_async_copy(wgu_hbm.at[layer, e], wgu_st, wsem.at[0]),
                pltpu.make_async_copy(wdn_hbm.at[layer, e], wdn_st, wsem.at[1]))

    e = blk_e_ref[i]

    @pl.when(i == 0)
    def _():
        for cp in weight_copies(e):
            cp.start()
        cp = pltpu.make_async_copy(h_hbm.at[pl.ds(tok0 * SUB, t_str * SUB)], hv.at[pl.ds(0, t_str * SUB)],
                                   iosem.at[0])
        cp.start()
        acc[...] = jnp.zeros_like(acc)
        hv[t_str * SUB:, :] = jnp.zeros((8 * SUB, LANES), f32)
        cp.wait()

    @pl.when((i == 0) | (e != blk_e_ref[jnp.maximum(i - 1, 0)]))
    def _():
        for cp in weight_copies(e):
            cp.wait()
        wgu_bf[...] = wgu_st[...].astype(bf16)
        wdn_bf[...] = wdn_st[...].astype(bf16)
        en = nxt_e_ref[e]

        @pl.when(en >= 0)
        def _():
            for cp in weight_copies(en):
                cp.start()

    def sub_block(r_lo):
        for r in range(MOE_SUB_ROWS):
            o = pl.multiple_of(src_ref[0, 0, r_lo + r] * SUB, SUB)
            xg[r * SUB:(r + 1) * SUB, :] = hv[pl.ds(o, SUB), :]
        xb = jnp.concatenate([xg[pl.ds(s, MOE_SUB_ROWS, stride=SUB), :] for s in range(SUB)],
                             axis=1).astype(bf16)
        gu = _dot(xb, wgu_bf[...]) + bgu_ref[...]
        gt = jnp.minimum(gu[:, :MOE_FF], SWIGLU_LIMIT)
        up = jnp.clip(gu[:, MOE_FF:], -SWIGLU_LIMIT, SWIGLU_LIMIT)
        act = (up + 1.0) * gt * jax.nn.sigmoid(SWIGLU_ALPHA * gt)
        y = _dot(act.astype(bf16), wdn_bf[...]) + bdn_ref[...]
        for s in range(SUB):
            xg[pl.ds(s, MOE_SUB_ROWS, stride=SUB), :] = y[:, s * LANES:(s + 1) * LANES]
        group = 16
        for r0 in range(0, MOE_SUB_ROWS, group):
            offs = [pl.multiple_of(src_ref[0, 0, r_lo + r] * SUB, SUB) for r in range(r0, r0 + group)]
            olds = [acc[pl.ds(o, SUB), :] for o in offs]
            for j, o in enumerate(offs):
                r = r0 + j
                acc[pl.ds(o, SUB), :] = olds[j] + xg[r * SUB:(r + 1) * SUB, :] * gate_ref[0, 0, r_lo + r]

    sub_block(0)
    for r_lo in range(MOE_SUB_ROWS, MOE_ROWS, MOE_SUB_ROWS):
        pl.when(blk_nv_ref[i] > r_lo)(functools.partial(sub_block, r_lo))

    @pl.when(i == n_used - 1)
    def _():
        cp = pltpu.make_async_copy(acc.at[pl.ds(0, t_str * SUB)], out_hbm, iosem.at[0])
        cp.start()
        cp.wait()


def _experts(blk_e, nxt_e, blk_nv, n_used, row_src, row_gate, ht, wgu, bgu, wdn, bdn, layer, tok0, t_str):
    n_blocks = row_src.shape[0] // MOE_ROWS
    src3 = row_src.reshape(n_blocks, 1, MOE_ROWS)
    gate3 = row_gate.reshape(n_blocks, 1, MOE_ROWS)
    per_block = pl.BlockSpec((1, 1, MOE_ROWS), lambda i, be, ne, nv, nu: (i, 0, 0), memory_space=pltpu.SMEM)
    grid_spec = pltpu.PrefetchScalarGridSpec(
        num_scalar_prefetch=4,
        grid=(n_blocks,),
        in_specs=[
            per_block,
            per_block,
            pl.BlockSpec(memory_space=pl.ANY),
            pl.BlockSpec(memory_space=pl.ANY),
            pl.BlockSpec((None, None, 1, 2 * MOE_FF), lambda i, be, ne, nv, nu: (layer, be[i], 0, 0)),
            pl.BlockSpec(memory_space=pl.ANY),
            pl.BlockSpec((None, None, 1, D), lambda i, be, ne, nv, nu: (layer, be[i], 0, 0)),
        ],
        out_specs=pl.BlockSpec(memory_space=pl.ANY),
        scratch_shapes=[
            pltpu.VMEM(((t_str + 8) * SUB, LANES), f32),
            pltpu.VMEM(((t_str + 8) * SUB, LANES), f32),
            pltpu.VMEM((D, 2 * MOE_FF), f32),
            pltpu.VMEM((MOE_FF, D), f32),
            pltpu.VMEM((D, 2 * MOE_FF), bf16),
            pltpu.VMEM((MOE_FF, D), bf16),
            pltpu.VMEM((MOE_SUB_ROWS * SUB, LANES), f32),
            pltpu.SemaphoreType.DMA((2,)),
            pltpu.SemaphoreType.DMA((1,)),
        ],
    )
    return pl.pallas_call(
        functools.partial(_experts_kernel, layer=layer, tok0=tok0),
        grid_spec=grid_spec,
        out_shape=jax.ShapeDtypeStruct((t_str * SUB, LANES), f32),
        compiler_params=_cparams(("arbitrary",), _EXPERTS_VMEM_LIMIT),
        name="experts",
    )(blk_e, nxt_e, blk_nv, n_used, src3, gate3, ht, wgu, bgu.reshape(DEPTH, N_EXPERTS, 1, 2 * MOE_FF),
      wdn, bdn.reshape(DEPTH, N_EXPERTS, 1, D))


def _combine_kernel(x1_ref, moep_ref, moes_ref, mod_ref, fg_ref, o_ref, *, final):
    i = pl.program_id(0)
    rows = lambda ref: jnp.concatenate([ref[pl.ds(s, TM, stride=SUB), :] for s in range(SUB)], axis=1)
    moe = jnp.where(i < P_TILES, rows(moep_ref), rows(moes_ref))
    m = mod_ref[pl.ds(_mod_row(i), 1), :]
    x2 = x1_ref[...] + m[:, 5 * D:6 * D] * moe
    if final:
        x2 = x2 * lax.rsqrt(jnp.mean(x2 * x2, axis=-1, keepdims=True) + EPS) * fg_ref[...]
    o_ref[...] = x2


def _combine(x1, moep, moes, mod, fg, final):
    tile = pl.BlockSpec((TM, D), lambda i: (i, 0))
    res = lambda shape: pl.BlockSpec(shape, lambda i: (0, 0), pipeline_mode=pl.Buffered(1))
    ctx = pl.BlockSpec((TM * SUB, LANES), lambda i: (jnp.minimum(i, P_TILES - 1), 0))
    lat = pl.BlockSpec((TM * SUB, LANES), lambda i: (jnp.maximum(i - P_TILES, 0), 0))
    return pl.pallas_call(
        functools.partial(_combine_kernel, final=final),
        grid=(N_TILES,),
        in_specs=[tile, ctx, lat, res((8, 6 * D)), res((1, D))],
        out_specs=tile,
        out_shape=jax.ShapeDtypeStruct((T, D), f32),
        compiler_params=_cparams(("arbitrary",)),
        name="combine",
    )(x1, moep, moes, mod, fg)


def _rope_tables():
    nf = HD_A // 4
    inv = ROPE_BASE ** (-jnp.arange(nf, dtype=f32) / nf)
    n = jnp.arange(DEC_SEQ)
    row = (n // GRID_W).astype(f32)[:, None] * inv
    col = (n % GRID_W).astype(f32)[:, None] * inv
    cos64 = jnp.concatenate([jnp.cos(row), jnp.cos(row), jnp.cos(col), jnp.cos(col)], axis=-1)
    sin64 = jnp.concatenate([-jnp.sin(row), jnp.sin(row), -jnp.sin(col), jnp.sin(col)], axis=-1)
    cos_s = jnp.tile(jnp.tile(cos64, (1, 2)), (DEC_BATCH, 1))
    sin_s = jnp.tile(jnp.tile(sin64, (1, 2)), (DEC_BATCH, 1))
    cos = jnp.concatenate([jnp.ones((T_P, LANES), f32), cos_s], axis=0)
    sin = jnp.concatenate([jnp.zeros((T_P, LANES), f32), sin_s], axis=0)
    return cos, sin


def _routing_tables(idx, rank, gate, counts):
    t_str = idx.shape[1]
    n_blocks = (t_str * TOP_K + N_EXPERTS * (MOE_ROWS - 1)) // MOE_ROWS + 1
    padded = (counts + MOE_ROWS - 1) // MOE_ROWS * MOE_ROWS
    pad_end = jnp.cumsum(padded)
    pad_start = pad_end - padded
    eids = jnp.arange(N_EXPERTS, dtype=i32)
    start_of = jnp.sum(jnp.where(idx[..., None] == eids, pad_start, 0), axis=-1)
    dest = (start_of + rank).reshape(-1)
    inv = jnp.full((n_blocks * MOE_ROWS,), -1, i32).at[dest].set(jnp.arange(TOP_K * t_str, dtype=i32))
    valid = inv >= 0
    safe = jnp.maximum(inv, 0)
    row_src = jnp.where(valid, safe % t_str, t_str).astype(i32)
    row_gate = jnp.where(valid, gate.reshape(-1)[safe], 0.0)
    starts = jnp.arange(n_blocks, dtype=i32) * MOE_ROWS
    blk_e = jnp.sum((pad_end[None, :] <= starts[:, None]).astype(i32), axis=1)
    present = counts > 0
    blk_e = jnp.minimum(blk_e, jnp.max(jnp.where(present, eids, 0))).astype(i32)
    later = (eids[None, :] > eids[:, None]) & present[None, :]
    nxt = jnp.min(jnp.where(later, eids[None, :], N_EXPERTS), axis=1)
    nxt_e = jnp.where(nxt < N_EXPERTS, nxt, -1).astype(i32)
    n_used = (pad_end[-1:] // MOE_ROWS).astype(i32)
    cnt_of = jnp.sum(jnp.where(blk_e[:, None] == eids, (counts + pad_start)[None, :], 0), axis=1)
    blk_nv = jnp.clip(cnt_of - starts, 0, MOE_ROWS).astype(i32)
    return blk_e, nxt_e, blk_nv, n_used, row_src, row_gate


def kernel(x_prompt, x_sample, cache_k, cache_v, state_fwd, state_bwd, c, c_ctx, ada_w, ada_b, norm_mix_g,
           norm_ffn_g, w_in, diff_lambda, diff_subln_g, gla_gk_w, gla_gk_b, gla_norm_g, w_branch_a,
           w_branch_b, w_out, router_w, router_b, expert_w_gu, expert_b_gu, expert_w_down, expert_b_down,
           final_norm_g):
    x = jnp.concatenate([x_prompt.reshape(T_P, D), x_sample.reshape(T_S, D)], axis=0)
    cvec = jnp.concatenate([c_ctx[None], c, jnp.zeros((3, D), f32)], axis=0)
    mods = _ada_mod(cvec, ada_w, ada_b)
    cos, sin = _rope_tables()
    ck = cache_k.reshape(DEC_BATCH, DEPTH, PAST_LEN, D)
    cv = cache_v.reshape(DEC_BATCH, DEPTH, PAST_LEN, D)
    main_w = 6 * D
    lr0 = main_w
    g0 = main_w + 2 * GK_RANK

    ks, vs, sfs, sbs = [], [], [], []
    yn = None
    for l in range(DEPTH):
        lam_init = 0.8 - 0.6 * math.exp(-0.3 * l)
        mod = mods[l]
        wm = w_in[l, :, :main_w].astype(bf16)
        wl = jnp.pad(w_in[l, :, lr0:g0], ((0, 0), (0, LANES - 2 * GK_RANK))).astype(bf16)
        wg = w_in[l, :, g0:].astype(bf16)
        q, k, v, bq, bk, bv, bg, lr, ga, gb = _inproj(x, mod, norm_mix_g[l][None], wm, wg, wl, cos, sin)
        ks.append(k[:T_P].reshape(BATCH, SEQ, D))
        vs.append(v[:T_P].reshape(BATCH, SEQ, D))

        lq = jnp.pad(diff_lambda[l], ((0, 4), (0, LANES - HD_A)))
        sg = diff_subln_g[l][None]
        oap = _attn_ctx(q, k, v, lq, sg, lam_init)
        oas = _attn_lat(q, k, v, ck, cv, lq, sg, lam_init, l)

        wf = jnp.pad(gla_gk_w[l, 0], ((0, LANES - GK_RANK), (0, 0)))
        wb = jnp.pad(gla_gk_w[l, 1], ((GK_RANK, LANES - 2 * GK_RANK), (0, 0)))
        bfv = gla_gk_b[l, 0][None]
        bbv = gla_gk_b[l, 1][None]
        gn = gla_norm_g[l][None]
        ogp, sf, sb = _gla(bq, bk, bv, bg, lr, wf, wb, bfv, bbv, gn, SEQ, 0, BATCH)
        ogs = _gla(bq, bk, bv, bg, lr, wf, wb, bfv, bbv, gn, DEC_SEQ, T_P // DEC_SEQ, DEC_BATCH,
                   states=(state_fwd, state_bwd), layer=l)
        sfs.append(sf)
        sbs.append(sb)

        rb = jnp.broadcast_to(router_b[l][:, None], (N_EXPERTS, LANES))
        x1, ht, idx8, gate8, rank8, cnt = _merge(
            x, oap, oas, ogp, ogs, ga, gb, mod, w_branch_a[l].astype(bf16), w_branch_b[l].astype(bf16),
            w_out[l].astype(bf16), norm_ffn_g[l][None], router_w[l].T, rb)
        moes = []
        for tok0, t_str, cnt_s in ((0, T_P, cnt[:N_EXPERTS, 0]), (T_P, T_S, cnt[N_EXPERTS:, 0])):
            sl = slice(tok0, tok0 + t_str)
            tables = _routing_tables(idx8[:TOP_K, sl], rank8[:TOP_K, sl], gate8[:TOP_K, sl], cnt_s)
            moes.append(_experts(*tables, ht, expert_w_gu, expert_b_gu, expert_w_down, expert_b_down, l,
                                 tok0, t_str))
        x = _combine(x1, moes[0], moes[1], mod, final_norm_g[None], final=(l == DEPTH - 1))
    yn = x

    y_prompt = yn[:T_P].reshape(BATCH, SEQ, D)
    y_sample = yn[T_P:].reshape(DEC_BATCH, DEC_SEQ, D)
    new_k = jnp.stack(ks, axis=1).reshape(BATCH, DEPTH, SEQ, N_HEADS_A, 2, HD_A)
    new_v = jnp.stack(vs, axis=1).reshape(BATCH, DEPTH, SEQ, N_HEADS_A, 2 * HD_A)
    new_sf = jnp.stack(sfs, axis=1)
    new_sb = jnp.stack(sbs, axis=1)
    return (y_prompt, y_sample, new_k, new_v, new_sf, new_sb)
```

```python
import functools
import math

import jax
import jax.numpy as jnp
from jax import lax
from jax.experimental import pallas as pl
from jax.experimental.pallas import tpu as pltpu

f32 = jnp.float32
bf16 = jnp.bfloat16
i32 = jnp.int32

D = 1024
BATCH = 16
SEQ = 256
DEPTH = 2
DEC_BATCH = 4
DEC_SEQ = 1024
PAST_LEN = 512
GRID_W = 64
HD_A = 64
N_HEADS_A = 8
ROPE_BASE = 10000.0
N_HEADS_B = 4
DK_B = 128
DV_B = 256
GK_RANK = 16
GATE_NORM = 16.0
CHUNK = 64
N_EXPERTS = 32
TOP_K = 4
MOE_FF = 1024
SWIGLU_LIMIT = 7.0
SWIGLU_ALPHA = 1.702
EPS = 1e-6

T_P = BATCH * SEQ
T_S = DEC_BATCH * DEC_SEQ
T = T_P + T_S
TM = 512
TQ = 256
N_TILES = T // TM
P_TILES = T_P // TM
S_TILES_PER_BATCH = DEC_SEQ // TM

MOE_ROWS = 512
MOE_SUB_ROWS = 128
SUB = 8
LANES = 128
FAST_DECAY_LIMIT = -60.0

_VMEM_LIMIT = 56 * 1024 * 1024
_EXPERTS_VMEM_LIMIT = 60 * 1024 * 1024


def _cparams(sem, vmem=_VMEM_LIMIT):
    return pltpu.CompilerParams(dimension_semantics=sem, vmem_limit_bytes=vmem)


def _dot(a, b):
    return jnp.dot(a, b, preferred_element_type=f32)


def _dot_nt(a, b):
    return lax.dot_general(a, b, (((1,), (1,)), ((), ())), preferred_element_type=f32)


def _split(x):
    hi = x.astype(bf16)
    lo = (x - hi.astype(f32)).astype(bf16)
    return hi, lo


def _dot3(a, b):
    ah, al = _split(a)
    bh, bl = _split(b)
    return _dot(ah, bh) + _dot(ah, bl) + _dot(al, bh)


def _dot3_nt(a, b):
    ah, al = _split(a)
    bh, bl = _split(b)
    return _dot_nt(ah, bh) + _dot_nt(ah, bl) + _dot_nt(al, bh)


def _silu(x):
    return x * jax.nn.sigmoid(x)


def _mod_row(i):
    shift = S_TILES_PER_BATCH.bit_length() - 1
    return jnp.where(i < P_TILES, 0, 1 + lax.shift_right_logical(jnp.maximum(i - P_TILES, 0), shift))


def _ada_kernel(c_ref, w_ref, b_ref, o_ref):
    o_ref[0] = _dot3(_silu(c_ref[...]), w_ref[0]) + b_ref[0]


def _ada_mod(cvec, ada_w, ada_b):
    tn = 1536
    return pl.pallas_call(
        _ada_kernel,
        grid=(DEPTH, 6 * D // tn),
        in_specs=[
            pl.BlockSpec((8, D), lambda l, j: (0, 0)),
            pl.BlockSpec((1, D, tn), lambda l, j: (l, 0, j)),
            pl.BlockSpec((1, 1, tn), lambda l, j: (l, 0, j)),
        ],
        out_specs=pl.BlockSpec((1, 8, tn), lambda l, j: (l, 0, j)),
        out_shape=jax.ShapeDtypeStruct((DEPTH, 8, 6 * D), f32),
        compiler_params=_cparams(("arbitrary", "arbitrary")),
        name="ada_mod",
    )(cvec, ada_w, ada_b.reshape(DEPTH, 1, 6 * D))


def _inproj_kernel(x_ref, mod_ref, g_ref, wm_ref, wg_ref, wl_ref, cos_ref, sin_ref,
                   q_ref, k_ref, v_ref, bq_ref, bk_ref, bv_ref, bg_ref, lr_ref, ga_ref, gb_ref):
    i = pl.program_id(0)
    m = mod_ref[pl.ds(_mod_row(i), 1), :]
    shift = m[:, 0:D]
    scale = m[:, D:2 * D]
    x = x_ref[...]
    h = x * lax.rsqrt(jnp.mean(x * x, axis=-1, keepdims=True) + EPS) * g_ref[...]
    hb = (h * (1.0 + scale) + shift).astype(bf16)

    cos = cos_ref[...]
    sin = sin_ref[...]
    lane = lax.broadcasted_iota(i32, (TM, LANES), 1)
    lo_mask = (lane & 31) < 16

    def rope(a):
        partner = jnp.where(lo_mask, pltpu.roll(a, LANES - 16, 1), pltpu.roll(a, 16, 1))
        return a * cos + partner * sin

    aq = _dot(hb, wm_ref[:, 0:D])
    for c in range(D // LANES):
        q_ref[:, c * LANES:(c + 1) * LANES] = rope(aq[:, c * LANES:(c + 1) * LANES]).astype(bf16)
    ak = _dot(hb, wm_ref[:, D:2 * D])
    for c in range(D // LANES):
        k_ref[:, c * LANES:(c + 1) * LANES] = rope(ak[:, c * LANES:(c + 1) * LANES])
    v_ref[...] = _dot(hb, wm_ref[:, 2 * D:3 * D])
    bq_ref[...] = _dot(hb, wm_ref[:, 3 * D:3 * D + 512]).astype(bf16)
    bk_ref[...] = _dot(hb, wm_ref[:, 3 * D + 512:4 * D]).astype(bf16)
    bv_ref[...] = _dot(hb, wm_ref[:, 4 * D:5 * D]).astype(bf16)
    bg_ref[...] = _dot(hb, wm_ref[:, 5 * D:6 * D]).astype(bf16)
    lr_ref[...] = _dot(hb, wl_ref[...])
    ga_ref[...] = _dot(hb, wg_ref[:, 0:D]).astype(bf16)
    gb_ref[...] = _dot(hb, wg_ref[:, D:2 * D]).astype(bf16)


def _inproj(x, mod, g, wm, wg, wl, cos, sin):
    tile = lambda w: pl.BlockSpec((TM, w), lambda i: (i, 0))
    res = lambda shape: pl.BlockSpec(shape, lambda i: (0, 0), pipeline_mode=pl.Buffered(1))
    widths = (D, D, D, 512, 512, D, D, LANES, D, D)
    dtypes = (bf16, f32, f32, bf16, bf16, bf16, bf16, f32, bf16, bf16)
    return pl.pallas_call(
        _inproj_kernel,
        grid=(N_TILES,),
        in_specs=[tile(D), res((8, 6 * D)), res((1, D)), res((D, 6 * D)), res((D, 2 * D)),
                  res((D, LANES)), tile(LANES), tile(LANES)],
        out_specs=[tile(w) for w in widths],
        out_shape=[jax.ShapeDtypeStruct((T, w), dt) for w, dt in zip(widths, dtypes)],
        compiler_params=_cparams(("arbitrary",)),
        name="inproj",
    )(x, mod, g, wm, wg, wl, cos, sin)


def _lam_from(lq_ref, lam_init):
    lq = lq_ref[...]
    s01 = jnp.sum(lq[0:1] * lq[1:2], axis=-1, keepdims=True)
    s23 = jnp.sum(lq[2:3] * lq[3:4], axis=-1, keepdims=True)
    return jnp.exp(s01) - jnp.exp(s23) + lam_init


def _subln(o, g, lam_init):
    o = o * lax.rsqrt(jnp.mean(o * o, axis=-1, keepdims=True) + EPS)
    return o * g * (1.0 - lam_init)


def _diff_heads(q_ref, segs, lq_ref, g_ref, o_ref, lam_init):
    lam = _lam_from(lq_ref, lam_init)
    g = g_ref[...]
    lo = lax.broadcasted_iota(i32, (q_ref.shape[0], LANES), 1) < HD_A
    for h in range(N_HEADS_A):
        cols = slice(h * LANES, (h + 1) * LANES)
        q = q_ref[:, cols] * (HD_A ** -0.5)
        kv = [(k_ref[:, cols], v_ref[:, cols]) for k_ref, v_ref in segs]
        outs = []
        for qm in (jnp.where(lo, q, 0), jnp.where(lo, 0, q)):
            ss = [_dot_nt(qm, kb) for kb, _ in kv]
            mx = functools.reduce(jnp.maximum, [jnp.max(s, axis=-1, keepdims=True) for s in ss])
            ps = [jnp.exp(s - mx) for s in ss]
            l = functools.reduce(jnp.add, [jnp.sum(p, axis=-1, keepdims=True) for p in ps])
            pv = functools.reduce(jnp.add, [_dot(p.astype(bf16), vb) for p, (_, vb) in zip(ps, kv)])
            outs.append(pv / l)
        o = outs[0] - lam * outs[1]
        o_ref[:, cols] = _subln(o, g, lam_init).astype(bf16)


def _attn_ctx_kernel(q_ref, k_ref, v_ref, lq_ref, g_ref, o_ref, kb_ref, vb_ref, *, lam_init):
    kb_ref[...] = k_ref[...].astype(bf16)
    vb_ref[...] = v_ref[...].astype(bf16)
    _diff_heads(q_ref, [(kb_ref, vb_ref)], lq_ref, g_ref, o_ref, lam_init)


def _attn_lat_kernel(q_ref, k_ref, v_ref, ck_ref, cv_ref, lq_ref, g_ref, o_ref, kb_ref, vb_ref, ckb_ref,
                     cvb_ref, *, lam_init):
    @pl.when(pl.program_id(1) == 0)
    def _():
        kb_ref[...] = k_ref[...].astype(bf16)
        vb_ref[...] = v_ref[...].astype(bf16)
        ckb_ref[...] = ck_ref[...].astype(bf16)
        cvb_ref[...] = cv_ref[...].astype(bf16)

    _diff_heads(q_ref, [(ckb_ref, cvb_ref), (kb_ref, vb_ref)], lq_ref, g_ref, o_ref, lam_init)


def _attn_ctx(q, k, v, lq, g, lam_init):
    blk = pl.BlockSpec((SEQ, D), lambda b: (b, 0))
    small = lambda s: pl.BlockSpec(s, lambda b: (0, 0))
    return pl.pallas_call(
        functools.partial(_attn_ctx_kernel, lam_init=lam_init),
        grid=(BATCH,),
        in_specs=[blk, blk, blk, small((8, LANES)), small((1, LANES))],
        out_specs=blk,
        out_shape=jax.ShapeDtypeStruct((T_P, D), bf16),
        scratch_shapes=[pltpu.VMEM((SEQ, D), bf16), pltpu.VMEM((SEQ, D), bf16)],
        compiler_params=_cparams(("arbitrary",)),
        name="attn_ctx",
    )(q, k, v, lq, g)


def _attn_lat(q, k, v, ck, cv, lq, g, lam_init, layer):
    nq = DEC_SEQ // TQ
    qblk = pl.BlockSpec((TQ, D), lambda b, t: (T_P // TQ + b * nq + t, 0))
    kblk = pl.BlockSpec((DEC_SEQ, D), lambda b, t: (T_P // DEC_SEQ + b, 0))
    cblk = pl.BlockSpec((None, None, PAST_LEN, D), lambda b, t: (b, layer, 0, 0))
    small = lambda s: pl.BlockSpec(s, lambda b, t: (0, 0))
    return pl.pallas_call(
        functools.partial(_attn_lat_kernel, lam_init=lam_init),
        grid=(DEC_BATCH, nq),
        in_specs=[qblk, kblk, kblk, cblk, cblk, small((8, LANES)), small((1, LANES))],
        out_specs=pl.BlockSpec((TQ, D), lambda b, t: (b * nq + t, 0)),
        out_shape=jax.ShapeDtypeStruct((T_S, D), bf16),
        scratch_shapes=[pltpu.VMEM((DEC_SEQ, D), bf16), pltpu.VMEM((DEC_SEQ, D), bf16),
                        pltpu.VMEM((PAST_LEN, D), bf16), pltpu.VMEM((PAST_LEN, D), bf16)],
        compiler_params=_cparams(("arbitrary", "arbitrary")),
        name="attn_lat",
    )(q, k, v, ck, cv, lq, g)


def _log_decay(lr, w_ref, b_ref):
    z = _dot3(lr, w_ref[...]) + b_ref[...]
    return -(jnp.maximum(-z, 0.0) + jnp.log(1.0 + jnp.exp(-jnp.abs(z)))) / GATE_NORM


def _gla_direction(q_ref, k_ref, v3b, g, st0, reverse, b_ref, att_ref, sall_ref):
    tb = g.shape[0]
    n = tb // CHUNK
    scale = DK_B ** -0.5
    ri = lax.broadcasted_iota(i32, (CHUNK, CHUNK), 0)
    ci = lax.broadcasted_iota(i32, (CHUNK, CHUNK), 1)
    keep = (ci >= ri) if reverse else (ci <= ri)
    tri = jnp.broadcast_to(jnp.where(keep, 1.0, 0.0).astype(bf16)[None], (n, CHUNK, CHUNK))

    g3 = g.reshape(n, CHUNK, DK_B)
    ghi, glo = _split(g3)
    bdot = lambda a, b: lax.dot_general(a, b, (((2,), (1,)), ((0,), (0,))), preferred_element_type=f32)
    b = bdot(tri, ghi) + bdot(tri, glo)
    tot = b[:, 0:1, :] if reverse else b[:, CHUNK - 1:CHUNK, :]
    q3 = q_ref[...].astype(f32).reshape(n, CHUNK, DK_B) * scale
    k3 = k_ref[...].astype(f32).reshape(n, CHUNK, DK_B)

    qs = (q3 * jnp.exp(b)).astype(bf16)
    kl = (k3 * jnp.exp(tot - b)).astype(bf16)
    kn = (k3 * jnp.exp(jnp.minimum(-b, 80.0))).astype(bf16)
    att = lax.dot_general(qs, kn, (((2,), (2,)), ((0,), (0,))), preferred_element_type=f32)
    att_ref[...] = jnp.where(keep[None], att, 0.0)

    b_ref[...] = b
    need_fix = jnp.min(tot) < FAST_DECAY_LIMIT

    @pl.when(need_fix)
    def _():
        lane = lax.broadcasted_iota(i32, (CHUNK, CHUNK), 1)

        def chunk_body(c, carry):
            bc = b_ref[c]
            row0 = pl.multiple_of(c * CHUNK, CHUNK)
            tot_c = bc[0:1] if reverse else bc[CHUNK - 1:CHUNK]

            @pl.when(jnp.min(tot_c) < FAST_DECAY_LIMIT)
            def _():
                qc = q_ref[pl.ds(row0, CHUNK), :].astype(f32) * scale
                kc = k_ref[pl.ds(row0, CHUNK), :].astype(f32)
                row = lax.broadcasted_iota(i32, (CHUNK, DK_B), 0)

                def jbody(j, acc):
                    bj = b_ref[c, pl.ds(j, 1), :]
                    kj = jnp.sum(jnp.where(row == j, kc, 0.0), axis=0, keepdims=True)
                    e = jnp.exp(jnp.minimum(bc - bj, 0.0))
                    col = jnp.sum(qc * e * kj, axis=-1, keepdims=True)
                    return acc + jnp.where(lane == j, col, 0.0)

                acc = lax.fori_loop(0, CHUNK, jbody, jnp.zeros((CHUNK, CHUNK), f32))
                att_ref[c] = jnp.where(keep, acc, 0.0)

            return carry

        lax.fori_loop(0, n, chunk_body, 0)

    u = lax.dot_general(v3b, kl, (((1,), (1,)), ((0,), (0,))), preferred_element_type=f32)
    dec = jnp.exp(tot)
    st = st0
    order = range(n - 1, -1, -1) if reverse else range(n)
    for c in order:
        sall_ref[c] = st
        st = st * dec[c] + u[c]
    inter = lax.dot_general(qs, sall_ref[...].astype(bf16), (((2,), (2,)), ((0,), (0,))),
                            preferred_element_type=f32)
    intra = lax.dot_general(att_ref[...].astype(bf16), v3b, (((2,), (1,)), ((0,), (0,))),
                            preferred_element_type=f32)
    return (inter + intra).reshape(tb, DV_B), st


def _gla_kernel(*refs, has_init, hp):
    if has_init:
        (q_ref, k_ref, v_ref, bg_ref, lr_ref, wf_ref, wb_ref, bf_ref, bb_ref, gn_ref, sf0_ref, sb0_ref,
         o_ref, b_ref, att_ref, sall_ref) = refs
    else:
        (q_ref, k_ref, v_ref, bg_ref, lr_ref, wf_ref, wb_ref, bf_ref, bb_ref, gn_ref,
         o_ref, sf_ref, sb_ref, b_ref, att_ref, sall_ref) = refs
    tb = q_ref.shape[0]
    n = tb // CHUNK
    lr = lr_ref[...]
    gn = gn_ref[...]
    for h in range(hp):
        kc = slice(h * DK_B, (h + 1) * DK_B)
        vc = slice(h * DV_B, (h + 1) * DV_B)
        if has_init:
            stf0 = sf0_ref[...].T
            stb0 = sb0_ref[...].T
        else:
            stf0 = jnp.zeros((DV_B, DK_B), f32)
            stb0 = stf0
        qh = q_ref.at[:, kc]
        kh = k_ref.at[:, kc]
        v3b = v_ref[:, vc].reshape(n, CHUNK, DV_B).astype(bf16)
        gf = _log_decay(lr, wf_ref.at[:, kc], bf_ref.at[:, kc])
        of, stf = _gla_direction(qh, kh, v3b, gf, stf0, False, b_ref.at[h, 0], att_ref.at[h, 0],
                                 sall_ref.at[h, 0])
        gb = _log_decay(lr, wb_ref.at[:, kc], bb_ref.at[:, kc])
        ob, stb = _gla_direction(qh, kh, v3b, gb, stb0, True, b_ref.at[h, 1], att_ref.at[h, 1],
                                 sall_ref.at[h, 1])
        o = of + ob
        o = o * lax.rsqrt(jnp.mean(o * o, axis=-1, keepdims=True) + EPS) * gn
        o_ref[:, vc] = (o * _silu(bg_ref[:, vc].astype(f32))).astype(bf16)
        if not has_init:
            sf_ref[h] = stf.T
            sb_ref[h] = stb.T


def _gla(bq, bk, bv, bg, lr, wf, wb, bfv, bbv, gn, tb, row_blk0, nb, states=None, layer=0):
    has_init = states is not None
    hp = 1 if has_init else N_HEADS_B
    n = tb // CHUNK
    rows = lambda w: pl.BlockSpec((tb, hp * w), lambda b, h: (row_blk0 + b, h))
    lrs = pl.BlockSpec((tb, LANES), lambda b, h: (row_blk0 + b, 0))
    wsp = pl.BlockSpec((LANES, hp * DK_B), lambda b, h: (0, h))
    bsp = pl.BlockSpec((1, hp * DK_B), lambda b, h: (0, h))
    gsp = pl.BlockSpec((1, DV_B), lambda b, h: (0, 0))
    in_specs = [rows(DK_B), rows(DK_B), rows(DV_B), rows(DV_B), lrs, wsp, wsp, bsp, bsp, gsp]
    args = [bq, bk, bv, bg, lr, wf, wb, bfv, bbv, gn]
    osp = pl.BlockSpec((tb, hp * DV_B), lambda b, h: (b, h))
    oshape = jax.ShapeDtypeStruct((nb * tb, D), bf16)
    if has_init:
        ssp = pl.BlockSpec((None, None, None, DK_B, DV_B), lambda b, h: (b, layer, h, 0, 0))
        in_specs += [ssp, ssp]
        args += list(states)
        out_specs, out_shape = osp, oshape
    else:
        ssp = pl.BlockSpec((None, hp, DK_B, DV_B), lambda b, h: (b, h, 0, 0))
        sshape = jax.ShapeDtypeStruct((nb, N_HEADS_B, DK_B, DV_B), f32)
        out_specs, out_shape = [osp, ssp, ssp], [oshape, sshape, sshape]
    return pl.pallas_call(
        functools.partial(_gla_kernel, has_init=has_init, hp=hp),
        grid=(nb, N_HEADS_B // hp),
        in_specs=in_specs,
        out_specs=out_specs,
        out_shape=out_shape,
        scratch_shapes=[pltpu.VMEM((hp, 2, n, CHUNK, DK_B), f32), pltpu.VMEM((hp, 2, n, CHUNK, CHUNK), f32),
                        pltpu.VMEM((hp, 2, n, DV_B, DK_B), f32)],
        compiler_params=_cparams(("arbitrary", "arbitrary")),
        name="gla_lat" if has_init else "gla_ctx",
    )(*args)


def _merge_kernel(x_ref, oap_ref, oas_ref, ogp_ref, ogs_ref, ga_ref, gb_ref, mod_ref, wa_ref, wb_ref,
                  wo_ref, g_ref, rw_ref, rb_ref,
                  x1_ref, ht_ref, idx_ref, gate_ref, rank_ref, cnt_ref, base_ref):
    i = pl.program_id(0)

    @pl.when((i == 0) | (i == P_TILES))
    def _():
        base_ref[...] = jnp.zeros_like(base_ref)

    is_ctx = i < P_TILES
    oa = jnp.where(is_ctx, oap_ref[...], oas_ref[...])
    og = jnp.where(is_ctx, ogp_ref[...], ogs_ref[...])
    ya = _dot(oa, wa_ref[...])
    yb = _dot(og, wb_ref[...])
    merged = jax.nn.sigmoid(ga_ref[...].astype(f32)) * ya + jax.nn.sigmoid(gb_ref[...].astype(f32)) * yb
    mix = _dot(merged.astype(bf16), wo_ref[...])
    m = mod_ref[pl.ds(_mod_row(i), 1), :]
    x1 = x_ref[...] + m[:, 2 * D:3 * D] * mix
    x1_ref[...] = x1
    h = x1 * lax.rsqrt(jnp.mean(x1 * x1, axis=-1, keepdims=True) + EPS) * g_ref[...]
    h = h * (1.0 + m[:, 4 * D:5 * D]) + m[:, 3 * D:4 * D]
    for s in range(SUB):
        ht_ref[pl.ds(s, TM, stride=SUB), :] = h[:, s * LANES:(s + 1) * LANES]

    lg = _dot3_nt(rw_ref[...], h) + rb_ref[...][:, 0:1]
    eid = lax.broadcasted_iota(i32, (N_EXPERTS, TM), 0)
    vals, sels, idxs = [], [], []
    for _ in range(TOP_K):
        mx = jnp.max(lg, axis=0, keepdims=True)
        ik = jnp.min(jnp.where(lg == mx, eid, N_EXPERTS), axis=0, keepdims=True)
        sel = eid == ik
        vals.append(mx)
        idxs.append(ik)
        sels.append(sel)
        lg = jnp.where(sel, -jnp.inf, lg)
    es = [jnp.exp(v - vals[0]) for v in vals]
    den = es[0] + es[1] + es[2] + es[3]
    onehot = jnp.where(sels[0] | sels[1] | sels[2] | sels[3], 1.0, 0.0)
    tr = lax.broadcasted_iota(i32, (TM, TM), 0)
    tc = lax.broadcasted_iota(i32, (TM, TM), 1)
    before = jnp.where(tr < tc, 1.0, 0.0).astype(bf16)
    pos = base_ref[:, 0:1] + _dot(onehot.astype(bf16), before)
    zrow_i = jnp.zeros((8 - TOP_K, TM), i32)
    ranks = [jnp.sum(jnp.where(s, pos, 0.0), axis=0, keepdims=True).astype(i32) for s in sels]
    idx_ref[...] = jnp.concatenate(idxs + [zrow_i], axis=0)
    rank_ref[...] = jnp.concatenate(ranks + [zrow_i], axis=0)
    gate_ref[...] = jnp.concatenate([e / den for e in es] + [jnp.zeros((8 - TOP_K, TM), f32)], axis=0)
    base = base_ref[...] + jnp.sum(onehot, axis=1, keepdims=True)
    base_ref[...] = base
    cnt_ref[...] = base.astype(i32)


def _merge(x, oap, oas, ogp, ogs, ga, gb, mod, wa, wb, wo, g, rw, rb):
    tile = lambda w: pl.BlockSpec((TM, w), lambda i: (i, 0))
    ctx = pl.BlockSpec((TM, D), lambda i: (jnp.minimum(i, P_TILES - 1), 0))
    lat = pl.BlockSpec((TM, D), lambda i: (jnp.maximum(i - P_TILES, 0), 0))
    res = lambda shape: pl.BlockSpec(shape, lambda i: (0, 0), pipeline_mode=pl.Buffered(1))
    meta = pl.BlockSpec((8, TM), lambda i: (0, i))
    return pl.pallas_call(
        _merge_kernel,
        grid=(N_TILES,),
        in_specs=[tile(D), ctx, lat, ctx, lat, tile(D), tile(D), res((8, 6 * D)), res((D, D)), res((D, D)),
                  res((D, D)), res((1, D)), res((N_EXPERTS, D)), res((N_EXPERTS, LANES))],
        out_specs=[tile(D), pl.BlockSpec((TM * SUB, LANES), lambda i: (i, 0)), meta, meta, meta,
                   pl.BlockSpec((N_EXPERTS, LANES), lambda i: (jnp.where(i < P_TILES, 0, 1), 0))],
        out_shape=[jax.ShapeDtypeStruct((T, D), f32), jax.ShapeDtypeStruct((T * SUB, LANES), f32),
                   jax.ShapeDtypeStruct((8, T), i32), jax.ShapeDtypeStruct((8, T), f32),
                   jax.ShapeDtypeStruct((8, T), i32), jax.ShapeDtypeStruct((2 * N_EXPERTS, LANES), i32)],
        scratch_shapes=[pltpu.VMEM((N_EXPERTS, LANES), f32)],
        compiler_params=_cparams(("arbitrary",)),
        name="merge_route",
    )(x, oap, oas, ogp, ogs, ga, gb, mod, wa, wb, wo, g, rw, rb)


def _experts_kernel(blk_e_ref, nxt_e_ref, blk_nv_ref, nused_ref, src_ref, slot_ref, gates_ref, h_hbm, wgu_hbm, bgu_ref, wdn_hbm,
                    bdn_ref, out_hbm, hv, acc, wgu_st, wdn_st, wgu_bf, wdn_bf, xg, wsem, iosem, *,
                    layer, tok0):
    i = pl.program_id(0)
    n_used = nused_ref[0]

    @pl.when(i < n_used)
    def _():
        _experts_step(i, n_used, blk_e_ref, nxt_e_ref, blk_nv_ref, src_ref, slot_ref, gates_ref, h_hbm, wgu_hbm, bgu_ref, wdn_hbm,
                      bdn_ref, out_hbm, hv, acc, wgu_st, wdn_st, wgu_bf, wdn_bf, xg, wsem, iosem, layer,
                      tok0)


def _experts_step(i, n_used, blk_e_ref, nxt_e_ref, blk_nv_ref, src_ref, slot_ref, gates_ref, h_hbm, wgu_hbm, bgu_ref, wdn_hbm,
                  bdn_ref, out_hbm, hv, acc, wgu_st, wdn_st, wgu_bf, wdn_bf, xg, wsem, iosem, layer, tok0):
    t_str = hv.shape[0] // SUB - 8

    def weight_copies(e):
        return (pltpu.make_async_copy(wgu_hbm.at[layer, e], wgu_st, wsem.at[0]),
                pltpu.make_async_copy(wdn_hbm.at[layer, e], wdn_st, wsem.at[1]))

    e = blk_e_ref[i]

    @pl.when(i == 0)
    def _():
        for cp in weight_copies(e):
            cp.start()
        cp = pltpu.make_async_copy(h_hbm.at[pl.ds(tok0 * SUB, t_str * SUB)], hv.at[pl.ds(0, t_str * SUB)],
                                   iosem.at[0])
        cp.start()
        acc[...] = jnp.zeros_like(acc)
        hv[t_str * SUB:, :] = jnp.zeros((8 * SUB, LANES), f32)
        cp.wait()

    @pl.when((i == 0) | (e != blk_e_ref[jnp.maximum(i - 1, 0)]))
    def _():
        for cp in weight_copies(e):
            cp.wait()
        wgu_bf[...] = wgu_st[...].astype(bf16)
        wdn_bf[...] = wdn_st[...].astype(bf16)
        en = nxt_e_ref[e]

        @pl.when(en >= 0)
        def _():
            for cp in weight_copies(en):
                cp.start()

    def sub_block(r_lo):
        for r in range(MOE_SUB_ROWS):
            o = pl.multiple_of(src_ref[0, 0, r_lo + r] * SUB, SUB)
            xg[r * SUB:(r + 1) * SUB, :] = hv[pl.ds(o, SUB), :]
        xb = jnp.concatenate([xg[pl.ds(s, MOE_SUB_ROWS, stride=SUB), :] for s in range(SUB)],
                             axis=1).astype(bf16)
        gu = _dot(xb, wgu_bf[...]) + bgu_ref[...]
        gt = jnp.minimum(gu[:, :MOE_FF], SWIGLU_LIMIT)
        up = jnp.clip(gu[:, MOE_FF:], -SWIGLU_LIMIT, SWIGLU_LIMIT)
        act = (up + 1.0) * gt * jax.nn.sigmoid(SWIGLU_ALPHA * gt)
        y = _dot(act.astype(bf16), wdn_bf[...]) + bdn_ref[...]
        for s in range(SUB):
            xg[pl.ds(s, MOE_SUB_ROWS, stride=SUB), :] = y[:, s * LANES:(s + 1) * LANES]
        group = 16
        for r0 in range(0, MOE_SUB_ROWS, group):
            offs = [pl.multiple_of(src_ref[0, 0, r_lo + r] * SUB, SUB) for r in range(r0, r0 + group)]
            olds = [acc[pl.ds(o, SUB), :] for o in offs]
            for j, o in enumerate(offs):
                r = r0 + j
                gate = gates_ref[slot_ref[0, 0, r_lo + r]]
                acc[pl.ds(o, SUB), :] = olds[j] + xg[r * SUB:(r + 1) * SUB, :] * gate

    sub_block(0)
    for r_lo in range(MOE_SUB_ROWS, MOE_ROWS, MOE_SUB_ROWS):
        pl.when(blk_nv_ref[i] > r_lo)(functools.partial(sub_block, r_lo))

    @pl.when(i == n_used - 1)
    def _():
        cp = pltpu.make_async_copy(acc.at[pl.ds(0, t_str * SUB)], out_hbm, iosem.at[0])
        cp.start()
        cp.wait()


def _experts(blk_e, nxt_e, blk_nv, n_used, row_src, row_slot, gates, ht, wgu, bgu, wdn, bdn, layer, tok0,
             t_str):
    n_blocks = row_src.shape[0] // MOE_ROWS
    src3 = row_src.reshape(n_blocks, 1, MOE_ROWS)
    slot3 = row_slot.reshape(n_blocks, 1, MOE_ROWS)
    per_block = pl.BlockSpec((1, 1, MOE_ROWS), lambda i, be, ne, nv, nu: (i, 0, 0), memory_space=pltpu.SMEM)
    grid_spec = pltpu.PrefetchScalarGridSpec(
        num_scalar_prefetch=4,
        grid=(n_blocks,),
        in_specs=[
            per_block,
            per_block,
            pl.BlockSpec(memory_space=pltpu.SMEM),
            pl.BlockSpec(memory_space=pl.ANY),
            pl.BlockSpec(memory_space=pl.ANY),
            pl.BlockSpec((None, None, 1, 2 * MOE_FF), lambda i, be, ne, nv, nu: (layer, be[i], 0, 0)),
            pl.BlockSpec(memory_space=pl.ANY),
            pl.BlockSpec((None, None, 1, D), lambda i, be, ne, nv, nu: (layer, be[i], 0, 0)),
        ],
        out_specs=pl.BlockSpec(memory_space=pl.ANY),
        scratch_shapes=[
            pltpu.VMEM(((t_str + 8) * SUB, LANES), f32),
            pltpu.VMEM(((t_str + 8) * SUB, LANES), f32),
            pltpu.VMEM((D, 2 * MOE_FF), f32),
            pltpu.VMEM((MOE_FF, D), f32),
            pltpu.VMEM((D, 2 * MOE_FF), bf16),
            pltpu.VMEM((MOE_FF, D), bf16),
            pltpu.VMEM((MOE_SUB_ROWS * SUB, LANES), f32),
            pltpu.SemaphoreType.DMA((2,)),
            pltpu.SemaphoreType.DMA((1,)),
        ],
    )
    return pl.pallas_call(
        functools.partial(_experts_kernel, layer=layer, tok0=tok0),
        grid_spec=grid_spec,
        out_shape=jax.ShapeDtypeStruct((t_str * SUB, LANES), f32),
        compiler_params=_cparams(("arbitrary",), _EXPERTS_VMEM_LIMIT),
        name="experts",
    )(blk_e, nxt_e, blk_nv, n_used, src3, slot3, gates, ht, wgu, bgu.reshape(DEPTH, N_EXPERTS, 1, 2 * MOE_FF),
      wdn, bdn.reshape(DEPTH, N_EXPERTS, 1, D))


def _combine_kernel(x1_ref, moep_ref, moes_ref, mod_ref, fg_ref, o_ref, *, final):
    i = pl.program_id(0)
    rows = lambda ref: jnp.concatenate([ref[pl.ds(s, TM, stride=SUB), :] for s in range(SUB)], axis=1)
    moe = jnp.where(i < P_TILES, rows(moep_ref), rows(moes_ref))
    m = mod_ref[pl.ds(_mod_row(i), 1), :]
    x2 = x1_ref[...] + m[:, 5 * D:6 * D] * moe
    if final:
        x2 = x2 * lax.rsqrt(jnp.mean(x2 * x2, axis=-1, keepdims=True) + EPS) * fg_ref[...]
    o_ref[...] = x2


def _combine(x1, moep, moes, mod, fg, final):
    tile = pl.BlockSpec((TM, D), lambda i: (i, 0))
    res = lambda shape: pl.BlockSpec(shape, lambda i: (0, 0), pipeline_mode=pl.Buffered(1))
    ctx = pl.BlockSpec((TM * SUB, LANES), lambda i: (jnp.minimum(i, P_TILES - 1), 0))
    lat = pl.BlockSpec((TM * SUB, LANES), lambda i: (jnp.maximum(i - P_TILES, 0), 0))
    return pl.pallas_call(
        functools.partial(_combine_kernel, final=final),
        grid=(N_TILES,),
        in_specs=[tile, ctx, lat, res((8, 6 * D)), res((1, D))],
        out_specs=tile,
        out_shape=jax.ShapeDtypeStruct((T, D), f32),
        compiler_params=_cparams(("arbitrary",)),
        name="combine",
    )(x1, moep, moes, mod, fg)


def _rope_tables():
    nf = HD_A // 4
    inv = ROPE_BASE ** (-jnp.arange(nf, dtype=f32) / nf)
    n = jnp.arange(DEC_SEQ)
    row = (n // GRID_W).astype(f32)[:, None] * inv
    col = (n % GRID_W).astype(f32)[:, None] * inv
    cos64 = jnp.concatenate([jnp.cos(row), jnp.cos(row), jnp.cos(col), jnp.cos(col)], axis=-1)
    sin64 = jnp.concatenate([-jnp.sin(row), jnp.sin(row), -jnp.sin(col), jnp.sin(col)], axis=-1)
    cos_s = jnp.tile(jnp.tile(cos64, (1, 2)), (DEC_BATCH, 1))
    sin_s = jnp.tile(jnp.tile(sin64, (1, 2)), (DEC_BATCH, 1))
    cos = jnp.concatenate([jnp.ones((T_P, LANES), f32), cos_s], axis=0)
    sin = jnp.concatenate([jnp.zeros((T_P, LANES), f32), sin_s], axis=0)
    return cos, sin


def _routing_tables(idx, rank, gate, counts):
    t_str = idx.shape[1]
    n_blocks = (t_str * TOP_K + N_EXPERTS * (MOE_ROWS - 1)) // MOE_ROWS + 1
    padded = (counts + MOE_ROWS - 1) // MOE_ROWS * MOE_ROWS
    pad_end = jnp.cumsum(padded)
    pad_start = pad_end - padded
    eids = jnp.arange(N_EXPERTS, dtype=i32)
    start_of = jnp.sum(jnp.where(idx[..., None] == eids, pad_start, 0), axis=-1)
    dest = (start_of + rank).reshape(-1)
    inv = jnp.full((n_blocks * MOE_ROWS,), -1, i32).at[dest].set(jnp.arange(TOP_K * t_str, dtype=i32))
    valid = inv >= 0
    safe = jnp.maximum(inv, 0)
    row_src = jnp.where(valid, safe % t_str, t_str).astype(i32)
    row_slot = jnp.where(valid, inv, TOP_K * t_str).astype(i32)
    gates = jnp.concatenate([gate.reshape(-1), jnp.zeros((8,), f32)])
    starts = jnp.arange(n_blocks, dtype=i32) * MOE_ROWS
    blk_e = jnp.sum((pad_end[None, :] <= starts[:, None]).astype(i32), axis=1)
    present = counts > 0
    blk_e = jnp.minimum(blk_e, jnp.max(jnp.where(present, eids, 0))).astype(i32)
    later = (eids[None, :] > eids[:, None]) & present[None, :]
    nxt = jnp.min(jnp.where(later, eids[None, :], N_EXPERTS), axis=1)
    nxt_e = jnp.where(nxt < N_EXPERTS, nxt, -1).astype(i32)
    n_used = (pad_end[-1:] // MOE_ROWS).astype(i32)
    cnt_of = jnp.sum(jnp.where(blk_e[:, None] == eids, (counts + pad_start)[None, :], 0), axis=1)
    blk_nv = jnp.clip(cnt_of - starts, 0, MOE_ROWS).astype(i32)
    return blk_e, nxt_e, blk_nv, n_used, row_src, row_slot, gates


def kernel(x_prompt, x_sample, cache_k, cache_v, state_fwd, state_bwd, c, c_ctx, ada_w, ada_b, norm_mix_g,
           norm_ffn_g, w_in, diff_lambda, diff_subln_g, gla_gk_w, gla_gk_b, gla_norm_g, w_branch_a,
           w_branch_b, w_out, router_w, router_b, expert_w_gu, expert_b_gu, expert_w_down, expert_b_down,
           final_norm_g):
    x = jnp.concatenate([x_prompt.reshape(T_P, D), x_sample.reshape(T_S, D)], axis=0)
    cvec = jnp.concatenate([c_ctx[None], c, jnp.zeros((3, D), f32)], axis=0)
    mods = _ada_mod(cvec, ada_w, ada_b)
    cos, sin = _rope_tables()
    ck = cache_k.reshape(DEC_BATCH, DEPTH, PAST_LEN, D)
    cv = cache_v.reshape(DEC_BATCH, DEPTH, PAST_LEN, D)
    main_w = 6 * D
    lr0 = main_w
    g0 = main_w + 2 * GK_RANK

    ks, vs, sfs, sbs = [], [], [], []
    yn = None
    for l in range(DEPTH):
        lam_init = 0.8 - 0.6 * math.exp(-0.3 * l)
        mod = mods[l]
        wm = w_in[l, :, :main_w].astype(bf16)
        wl = jnp.pad(w_in[l, :, lr0:g0], ((0, 0), (0, LANES - 2 * GK_RANK))).astype(bf16)
        wg = w_in[l, :, g0:].astype(bf16)
        q, k, v, bq, bk, bv, bg, lr, ga, gb = _inproj(x, mod, norm_mix_g[l][None], wm, wg, wl, cos, sin)
        ks.append(k[:T_P].reshape(BATCH, SEQ, D))
        vs.append(v[:T_P].reshape(BATCH, SEQ, D))

        lq = jnp.pad(diff_lambda[l], ((0, 4), (0, LANES - HD_A)))
        sg = diff_subln_g[l][None]
        oap = _attn_ctx(q, k, v, lq, sg, lam_init)
        oas = _attn_lat(q, k, v, ck, cv, lq, sg, lam_init, l)

        wf = jnp.pad(gla_gk_w[l, 0], ((0, LANES - GK_RANK), (0, 0)))
        wb = jnp.pad(gla_gk_w[l, 1], ((GK_RANK, LANES - 2 * GK_RANK), (0, 0)))
        bfv = gla_gk_b[l, 0][None]
        bbv = gla_gk_b[l, 1][None]
        gn = gla_norm_g[l][None]
        ogp, sf, sb = _gla(bq, bk, bv, bg, lr, wf, wb, bfv, bbv, gn, SEQ, 0, BATCH)
        ogs = _gla(bq, bk, bv, bg, lr, wf, wb, bfv, bbv, gn, DEC_SEQ, T_P // DEC_SEQ, DEC_BATCH,
                   states=(state_fwd, state_bwd), layer=l)
        sfs.append(sf)
        sbs.append(sb)

        rb = jnp.broadcast_to(router_b[l][:, None], (N_EXPERTS, LANES))
        x1, ht, idx8, gate8, rank8, cnt = _merge(
            x, oap, oas, ogp, ogs, ga, gb, mod, w_branch_a[l].astype(bf16), w_branch_b[l].astype(bf16),
            w_out[l].astype(bf16), norm_ffn_g[l][None], router_w[l].T, rb)
        moes = []
        for tok0, t_str, cnt_s in ((0, T_P, cnt[:N_EXPERTS, 0]), (T_P, T_S, cnt[N_EXPERTS:, 0])):
            sl = slice(tok0, tok0 + t_str)
            tables = _routing_tables(idx8[:TOP_K, sl], rank8[:TOP_K, sl], gate8[:TOP_K, sl], cnt_s)
            moes.append(_experts(*tables, ht, expert_w_gu, expert_b_gu, expert_w_down, expert_b_down, l,
                                 tok0, t_str))
        x = _combine(x1, moes[0], moes[1], mod, final_norm_g[None], final=(l == DEPTH - 1))
    yn = x

    y_prompt = yn[:T_P].reshape(BATCH, SEQ, D)
    y_sample = yn[T_P:].reshape(DEC_BATCH, DEC_SEQ, D)
    new_k = jnp.stack(ks, axis=1).reshape(BATCH, DEPTH, SEQ, N_HEADS_A, 2, HD_A)
    new_v = jnp.stack(vs, axis=1).reshape(BATCH, DEPTH, SEQ, N_HEADS_A, 2 * HD_A)
    new_sf = jnp.stack(sfs, axis=1)
    new_sb = jnp.stack(sbs, axis=1)
    return (y_prompt, y_sample, new_k, new_v, new_sf, new_sb)
```

```python
import functools
import math

import jax
import jax.numpy as jnp
from jax import lax
from jax.experimental import pallas as pl
from jax.experimental.pallas import tpu as pltpu

f32 = jnp.float32
bf16 = jnp.bfloat16
i32 = jnp.int32

D = 1024
BATCH = 16
SEQ = 256
DEPTH = 2
DEC_BATCH = 4
DEC_SEQ = 1024
PAST_LEN = 512
GRID_W = 64
HD_A = 64
N_HEADS_A = 8
ROPE_BASE = 10000.0
N_HEADS_B = 4
DK_B = 128
DV_B = 256
GK_RANK = 16
GATE_NORM = 16.0
CHUNK = 64
N_EXPERTS = 32
TOP_K = 4
MOE_FF = 1024
SWIGLU_LIMIT = 7.0
SWIGLU_ALPHA = 1.702
EPS = 1e-6

T_P = BATCH * SEQ
T_S = DEC_BATCH * DEC_SEQ
T = T_P + T_S
TM = 512
TQ = 256
N_TILES = T // TM
P_TILES = T_P // TM
S_TILES_PER_BATCH = DEC_SEQ // TM

MOE_ROWS = 512
MOE_SUB_ROWS = 128
SUB = 8
LANES = 128
FAST_DECAY_LIMIT = -60.0

_VMEM_LIMIT = 56 * 1024 * 1024
_EXPERTS_VMEM_LIMIT = 60 * 1024 * 1024


def _cparams(sem, vmem=_VMEM_LIMIT):
    return pltpu.CompilerParams(dimension_semantics=sem, vmem_limit_bytes=vmem)


def _dot(a, b):
    return jnp.dot(a, b, preferred_element_type=f32)


def _dot_nt(a, b):
    return lax.dot_general(a, b, (((1,), (1,)), ((), ())), preferred_element_type=f32)


def _split(x):
    hi = x.astype(bf16)
    lo = (x - hi.astype(f32)).astype(bf16)
    return hi, lo


def _dot3(a, b):
    ah, al = _split(a)
    bh, bl = _split(b)
    return _dot(ah, bh) + _dot(ah, bl) + _dot(al, bh)


def _dot3_nt(a, b):
    ah, al = _split(a)
    bh, bl = _split(b)
    return _dot_nt(ah, bh) + _dot_nt(ah, bl) + _dot_nt(al, bh)


def _silu(x):
    return x * jax.nn.sigmoid(x)


def _mod_row(i):
    shift = S_TILES_PER_BATCH.bit_length() - 1
    return jnp.where(i < P_TILES, 0, 1 + lax.shift_right_logical(jnp.maximum(i - P_TILES, 0), shift))


def _ada_kernel(c_ref, w_ref, b_ref, o_ref):
    o_ref[0] = _dot3(_silu(c_ref[...]), w_ref[0]) + b_ref[0]


def _ada_mod(cvec, ada_w, ada_b):
    tn = 1536
    return pl.pallas_call(
        _ada_kernel,
        grid=(DEPTH, 6 * D // tn),
        in_specs=[
            pl.BlockSpec((8, D), lambda l, j: (0, 0)),
            pl.BlockSpec((1, D, tn), lambda l, j: (l, 0, j)),
            pl.BlockSpec((1, 1, tn), lambda l, j: (l, 0, j)),
        ],
        out_specs=pl.BlockSpec((1, 8, tn), lambda l, j: (l, 0, j)),
        out_shape=jax.ShapeDtypeStruct((DEPTH, 8, 6 * D), f32),
        compiler_params=_cparams(("arbitrary", "arbitrary")),
        name="ada_mod",
    )(cvec, ada_w, ada_b.reshape(DEPTH, 1, 6 * D))


def _inproj_kernel(x_ref, mod_ref, g_ref, wm_ref, wg_ref, wl_ref, cos_ref, sin_ref,
                   q_ref, k_ref, v_ref, bq_ref, bk_ref, bv_ref, bg_ref, lr_ref, ga_ref, gb_ref):
    i = pl.program_id(0)
    m = mod_ref[pl.ds(_mod_row(i), 1), :]
    shift = m[:, 0:D]
    scale = m[:, D:2 * D]
    x = x_ref[...]
    h = x * lax.rsqrt(jnp.mean(x * x, axis=-1, keepdims=True) + EPS) * g_ref[...]
    hb = (h * (1.0 + scale) + shift).astype(bf16)

    cos = cos_ref[...]
    sin = sin_ref[...]
    lane = lax.broadcasted_iota(i32, (TM, LANES), 1)
    lo_mask = (lane & 31) < 16

    def rope(a):
        partner = jnp.where(lo_mask, pltpu.roll(a, LANES - 16, 1), pltpu.roll(a, 16, 1))
        return a * cos + partner * sin

    aq = _dot(hb, wm_ref[:, 0:D])
    for c in range(D // LANES):
        q_ref[:, c * LANES:(c + 1) * LANES] = rope(aq[:, c * LANES:(c + 1) * LANES]).astype(bf16)
    ak = _dot(hb, wm_ref[:, D:2 * D])
    for c in range(D // LANES):
        k_ref[:, c * LANES:(c + 1) * LANES] = rope(ak[:, c * LANES:(c + 1) * LANES])
    v_ref[...] = _dot(hb, wm_ref[:, 2 * D:3 * D])
    bq_ref[...] = _dot(hb, wm_ref[:, 3 * D:3 * D + 512]).astype(bf16)
    bk_ref[...] = _dot(hb, wm_ref[:, 3 * D + 512:4 * D]).astype(bf16)
    bv_ref[...] = _dot(hb, wm_ref[:, 4 * D:5 * D]).astype(bf16)
    bg_ref[...] = _dot(hb, wm_ref[:, 5 * D:6 * D]).astype(bf16)
    lr_ref[...] = _dot(hb, wl_ref[...])
    ga_ref[...] = _dot(hb, wg_ref[:, 0:D]).astype(bf16)
    gb_ref[...] = _dot(hb, wg_ref[:, D:2 * D]).astype(bf16)


def _inproj(x, mod, g, wm, wg, wl, cos, sin):
    tile = lambda w: pl.BlockSpec((TM, w), lambda i: (i, 0))
    res = lambda shape: pl.BlockSpec(shape, lambda i: (0, 0), pipeline_mode=pl.Buffered(1))
    widths = (D, D, D, 512, 512, D, D, LANES, D, D)
    dtypes = (bf16, f32, f32, bf16, bf16, bf16, bf16, f32, bf16, bf16)
    return pl.pallas_call(
        _inproj_kernel,
        grid=(N_TILES,),
        in_specs=[tile(D), res((8, 6 * D)), res((1, D)), res((D, 6 * D)), res((D, 2 * D)),
                  res((D, LANES)), tile(LANES), tile(LANES)],
        out_specs=[tile(w) for w in widths],
        out_shape=[jax.ShapeDtypeStruct((T, w), dt) for w, dt in zip(widths, dtypes)],
        compiler_params=_cparams(("arbitrary",)),
        name="inproj",
    )(x, mod, g, wm, wg, wl, cos, sin)


def _lam_from(lq_ref, lam_init):
    lq = lq_ref[...]
    s01 = jnp.sum(lq[0:1] * lq[1:2], axis=-1, keepdims=True)
    s23 = jnp.sum(lq[2:3] * lq[3:4], axis=-1, keepdims=True)
    return jnp.exp(s01) - jnp.exp(s23) + lam_init


def _subln(o, g, lam_init):
    o = o * lax.rsqrt(jnp.mean(o * o, axis=-1, keepdims=True) + EPS)
    return o * g * (1.0 - lam_init)


def _diff_heads(q_ref, segs, lq_ref, g_ref, o_ref, lam_init):
    lam = _lam_from(lq_ref, lam_init)
    g = g_ref[...]
    tq = q_ref.shape[0]
    lo = lax.broadcasted_iota(i32, (tq, LANES), 1) < HD_A
    for h in range(N_HEADS_A):
        cols = slice(h * LANES, (h + 1) * LANES)
        q = q_ref[:, cols] * (HD_A ** -0.5)
        kv = [(k_ref[:, cols], v_ref[:, cols]) for k_ref, v_ref in segs]
        qm = jnp.concatenate([jnp.where(lo, q, 0), jnp.where(lo, 0, q)], axis=0)
        ss = [_dot_nt(qm, kb) for kb, _ in kv]
        mx = functools.reduce(jnp.maximum, [jnp.max(s, axis=-1, keepdims=True) for s in ss])
        ps = [jnp.exp(s - mx) for s in ss]
        l = functools.reduce(jnp.add, [jnp.sum(p, axis=-1, keepdims=True) for p in ps])
        pv = functools.reduce(jnp.add, [_dot(p.astype(bf16), vb) for p, (_, vb) in zip(ps, kv)]) / l
        o = pv[:tq] - lam * pv[tq:]
        o_ref[:, cols] = _subln(o, g, lam_init).astype(bf16)


def _attn_ctx_kernel(q_ref, k_ref, v_ref, lq_ref, g_ref, o_ref, kb_ref, vb_ref, *, lam_init):
    kb_ref[...] = k_ref[...].astype(bf16)
    vb_ref[...] = v_ref[...].astype(bf16)
    _diff_heads(q_ref, [(kb_ref, vb_ref)], lq_ref, g_ref, o_ref, lam_init)


def _attn_lat_kernel(q_ref, k_ref, v_ref, ck_ref, cv_ref, lq_ref, g_ref, o_ref, kb_ref, vb_ref, ckb_ref,
                     cvb_ref, *, lam_init):
    @pl.when(pl.program_id(1) == 0)
    def _():
        kb_ref[...] = k_ref[...].astype(bf16)
        vb_ref[...] = v_ref[...].astype(bf16)
        ckb_ref[...] = ck_ref[...].astype(bf16)
        cvb_ref[...] = cv_ref[...].astype(bf16)

    _diff_heads(q_ref, [(ckb_ref, cvb_ref), (kb_ref, vb_ref)], lq_ref, g_ref, o_ref, lam_init)


def _attn_ctx(q, k, v, lq, g, lam_init):
    blk = pl.BlockSpec((SEQ, D), lambda b: (b, 0))
    small = lambda s: pl.BlockSpec(s, lambda b: (0, 0))
    return pl.pallas_call(
        functools.partial(_attn_ctx_kernel, lam_init=lam_init),
        grid=(BATCH,),
        in_specs=[blk, blk, blk, small((8, LANES)), small((1, LANES))],
        out_specs=blk,
        out_shape=jax.ShapeDtypeStruct((T_P, D), bf16),
        scratch_shapes=[pltpu.VMEM((SEQ, D), bf16), pltpu.VMEM((SEQ, D), bf16)],
        compiler_params=_cparams(("arbitrary",)),
        name="attn_ctx",
    )(q, k, v, lq, g)


def _attn_lat(q, k, v, ck, cv, lq, g, lam_init, layer):
    nq = DEC_SEQ // TQ
    qblk = pl.BlockSpec((TQ, D), lambda b, t: (T_P // TQ + b * nq + t, 0))
    kblk = pl.BlockSpec((DEC_SEQ, D), lambda b, t: (T_P // DEC_SEQ + b, 0))
    cblk = pl.BlockSpec((None, None, PAST_LEN, D), lambda b, t: (b, layer, 0, 0))
    small = lambda s: pl.BlockSpec(s, lambda b, t: (0, 0))
    return pl.pallas_call(
        functools.partial(_attn_lat_kernel, lam_init=lam_init),
        grid=(DEC_BATCH, nq),
        in_specs=[qblk, kblk, kblk, cblk, cblk, small((8, LANES)), small((1, LANES))],
        out_specs=pl.BlockSpec((TQ, D), lambda b, t: (b * nq + t, 0)),
        out_shape=jax.ShapeDtypeStruct((T_S, D), bf16),
        scratch_shapes=[pltpu.VMEM((DEC_SEQ, D), bf16), pltpu.VMEM((DEC_SEQ, D), bf16),
                        pltpu.VMEM((PAST_LEN, D), bf16), pltpu.VMEM((PAST_LEN, D), bf16)],
        compiler_params=_cparams(("arbitrary", "arbitrary")),
        name="attn_lat",
    )(q, k, v, ck, cv, lq, g)


def _log_decay(lr, w_ref, b_ref):
    z = _dot3(lr, w_ref[...]) + b_ref[...]
    return -(jnp.maximum(-z, 0.0) + jnp.log(1.0 + jnp.exp(-jnp.abs(z)))) / GATE_NORM


def _gla_direction(q_ref, k_ref, v3b, g, st0, reverse, b_ref, att_ref, sall_ref):
    tb = g.shape[0]
    n = tb // CHUNK
    scale = DK_B ** -0.5
    ri = lax.broadcasted_iota(i32, (CHUNK, CHUNK), 0)
    ci = lax.broadcasted_iota(i32, (CHUNK, CHUNK), 1)
    keep = (ci >= ri) if reverse else (ci <= ri)
    tri = jnp.broadcast_to(jnp.where(keep, 1.0, 0.0).astype(bf16)[None], (n, CHUNK, CHUNK))

    g3 = g.reshape(n, CHUNK, DK_B)
    ghi, glo = _split(g3)
    bdot = lambda a, b: lax.dot_general(a, b, (((2,), (1,)), ((0,), (0,))), preferred_element_type=f32)
    b = bdot(tri, ghi) + bdot(tri, glo)
    tot = b[:, 0:1, :] if reverse else b[:, CHUNK - 1:CHUNK, :]
    q3 = q_ref[...].astype(f32).reshape(n, CHUNK, DK_B) * scale
    k3 = k_ref[...].astype(f32).reshape(n, CHUNK, DK_B)

    qs = (q3 * jnp.exp(b)).astype(bf16)
    kl = (k3 * jnp.exp(tot - b)).astype(bf16)
    kn = (k3 * jnp.exp(jnp.minimum(-b, 80.0))).astype(bf16)
    att = lax.dot_general(qs, kn, (((2,), (2,)), ((0,), (0,))), preferred_element_type=f32)
    att_ref[...] = jnp.where(keep[None], att, 0.0)

    b_ref[...] = b
    need_fix = jnp.min(tot) < FAST_DECAY_LIMIT

    @pl.when(need_fix)
    def _():
        lane = lax.broadcasted_iota(i32, (CHUNK, CHUNK), 1)

        def chunk_body(c, carry):
            bc = b_ref[c]
            row0 = pl.multiple_of(c * CHUNK, CHUNK)
            tot_c = bc[0:1] if reverse else bc[CHUNK - 1:CHUNK]

            @pl.when(jnp.min(tot_c) < FAST_DECAY_LIMIT)
            def _():
                qc = q_ref[pl.ds(row0, CHUNK), :].astype(f32) * scale
                kc = k_ref[pl.ds(row0, CHUNK), :].astype(f32)
                row = lax.broadcasted_iota(i32, (CHUNK, DK_B), 0)

                def jbody(j, acc):
                    bj = b_ref[c, pl.ds(j, 1), :]
                    kj = jnp.sum(jnp.where(row == j, kc, 0.0), axis=0, keepdims=True)
                    e = jnp.exp(jnp.minimum(bc - bj, 0.0))
                    col = jnp.sum(qc * e * kj, axis=-1, keepdims=True)
                    return acc + jnp.where(lane == j, col, 0.0)

                acc = lax.fori_loop(0, CHUNK, jbody, jnp.zeros((CHUNK, CHUNK), f32))
                att_ref[c] = jnp.where(keep, acc, 0.0)

            return carry

        lax.fori_loop(0, n, chunk_body, 0)

    u = lax.dot_general(v3b, kl, (((1,), (1,)), ((0,), (0,))), preferred_element_type=f32)
    dec = jnp.exp(tot)
    st = st0
    order = range(n - 1, -1, -1) if reverse else range(n)
    for c in order:
        sall_ref[c] = st
        st = st * dec[c] + u[c]
    inter = lax.dot_general(qs, sall_ref[...].astype(bf16), (((2,), (2,)), ((0,), (0,))),
                            preferred_element_type=f32)
    intra = lax.dot_general(att_ref[...].astype(bf16), v3b, (((2,), (1,)), ((0,), (0,))),
                            preferred_element_type=f32)
    return (inter + intra).reshape(tb, DV_B), st


def _gla_kernel(*refs, has_init, hp):
    if has_init:
        (q_ref, k_ref, v_ref, bg_ref, lr_ref, wf_ref, wb_ref, bf_ref, bb_ref, gn_ref, sf0_ref, sb0_ref,
         o_ref, b_ref, att_ref, sall_ref) = refs
    else:
        (q_ref, k_ref, v_ref, bg_ref, lr_ref, wf_ref, wb_ref, bf_ref, bb_ref, gn_ref,
         o_ref, sf_ref, sb_ref, b_ref, att_ref, sall_ref) = refs
    tb = q_ref.shape[0]
    n = tb // CHUNK
    lr = lr_ref[...]
    gn = gn_ref[...]
    gf_all = _log_decay(lr, wf_ref, bf_ref)
    gb_all = _log_decay(lr, wb_ref, bb_ref)
    for h in range(hp):
        kc = slice(h * DK_B, (h + 1) * DK_B)
        vc = slice(h * DV_B, (h + 1) * DV_B)
        if has_init:
            stf0 = sf0_ref[...].T
            stb0 = sb0_ref[...].T
        else:
            stf0 = jnp.zeros((DV_B, DK_B), f32)
            stb0 = stf0
        qh = q_ref.at[:, kc]
        kh = k_ref.at[:, kc]
        v3b = v_ref[:, vc].reshape(n, CHUNK, DV_B).astype(bf16)
        of, stf = _gla_direction(qh, kh, v3b, gf_all[:, kc], stf0, False, b_ref.at[h, 0], att_ref.at[h, 0],
                                 sall_ref.at[h, 0])
        ob, stb = _gla_direction(qh, kh, v3b, gb_all[:, kc], stb0, True, b_ref.at[h, 1], att_ref.at[h, 1],
                                 sall_ref.at[h, 1])
        o = of + ob
        o = o * lax.rsqrt(jnp.mean(o * o, axis=-1, keepdims=True) + EPS) * gn
        o_ref[:, vc] = (o * _silu(bg_ref[:, vc].astype(f32))).astype(bf16)
        if not has_init:
            sf_ref[h] = stf.T
            sb_ref[h] = stb.T


def _gla(bq, bk, bv, bg, lr, wf, wb, bfv, bbv, gn, tb, row_blk0, nb, states=None, layer=0):
    has_init = states is not None
    hp = 1 if has_init else N_HEADS_B
    n = tb // CHUNK
    rows = lambda w: pl.BlockSpec((tb, hp * w), lambda b, h: (row_blk0 + b, h))
    lrs = pl.BlockSpec((tb, LANES), lambda b, h: (row_blk0 + b, 0))
    wsp = pl.BlockSpec((LANES, hp * DK_B), lambda b, h: (0, h))
    bsp = pl.BlockSpec((1, hp * DK_B), lambda b, h: (0, h))
    gsp = pl.BlockSpec((1, DV_B), lambda b, h: (0, 0))
    in_specs = [rows(DK_B), rows(DK_B), rows(DV_B), rows(DV_B), lrs, wsp, wsp, bsp, bsp, gsp]
    args = [bq, bk, bv, bg, lr, wf, wb, bfv, bbv, gn]
    osp = pl.BlockSpec((tb, hp * DV_B), lambda b, h: (b, h))
    oshape = jax.ShapeDtypeStruct((nb * tb, D), bf16)
    if has_init:
        ssp = pl.BlockSpec((None, None, None, DK_B, DV_B), lambda b, h: (b, layer, h, 0, 0))
        in_specs += [ssp, ssp]
        args += list(states)
        out_specs, out_shape = osp, oshape
    else:
        ssp = pl.BlockSpec((None, hp, DK_B, DV_B), lambda b, h: (b, h, 0, 0))
        sshape = jax.ShapeDtypeStruct((nb, N_HEADS_B, DK_B, DV_B), f32)
        out_specs, out_shape = [osp, ssp, ssp], [oshape, sshape, sshape]
    return pl.pallas_call(
        functools.partial(_gla_kernel, has_init=has_init, hp=hp),
        grid=(nb, N_HEADS_B // hp),
        in_specs=in_specs,
        out_specs=out_specs,
        out_shape=out_shape,
        scratch_shapes=[pltpu.VMEM((hp, 2, n, CHUNK, DK_B), f32), pltpu.VMEM((hp, 2, n, CHUNK, CHUNK), f32),
                        pltpu.VMEM((hp, 2, n, DV_B, DK_B), f32)],
        compiler_params=_cparams(("arbitrary", "arbitrary")),
        name="gla_lat" if has_init else "gla_ctx",
    )(*args)


def _merge_kernel(x_ref, oap_ref, oas_ref, ogp_ref, ogs_ref, ga_ref, gb_ref, mod_ref, wa_ref, wb_ref,
                  wo_ref, g_ref, rw_ref, rb_ref,
                  x1_ref, ht_ref, idx_ref, gate_ref, rank_ref, cnt_ref, base_ref):
    i = pl.program_id(0)

    @pl.when((i == 0) | (i == P_TILES))
    def _():
        base_ref[...] = jnp.zeros_like(base_ref)

    is_ctx = i < P_TILES
    oa = jnp.where(is_ctx, oap_ref[...], oas_ref[...])
    og = jnp.where(is_ctx, ogp_ref[...], ogs_ref[...])
    ya = _dot(oa, wa_ref[...])
    yb = _dot(og, wb_ref[...])
    merged = jax.nn.sigmoid(ga_ref[...].astype(f32)) * ya + jax.nn.sigmoid(gb_ref[...].astype(f32)) * yb
    mix = _dot(merged.astype(bf16), wo_ref[...])
    m = mod_ref[pl.ds(_mod_row(i), 1), :]
    x1 = x_ref[...] + m[:, 2 * D:3 * D] * mix
    x1_ref[...] = x1
    h = x1 * lax.rsqrt(jnp.mean(x1 * x1, axis=-1, keepdims=True) + EPS) * g_ref[...]
    h = h * (1.0 + m[:, 4 * D:5 * D]) + m[:, 3 * D:4 * D]
    for s in range(SUB):
        ht_ref[pl.ds(s, TM, stride=SUB), :] = h[:, s * LANES:(s + 1) * LANES]

    lg = _dot3_nt(rw_ref[...], h) + rb_ref[...][:, 0:1]
    eid = lax.broadcasted_iota(i32, (N_EXPERTS, TM), 0)
    vals, sels, idxs = [], [], []
    for _ in range(TOP_K):
        mx = jnp.max(lg, axis=0, keepdims=True)
        ik = jnp.min(jnp.where(lg == mx, eid, N_EXPERTS), axis=0, keepdims=True)
        sel = eid == ik
        vals.append(mx)
        idxs.append(ik)
        sels.append(sel)
        lg = jnp.where(sel, -jnp.inf, lg)
    es = [jnp.exp(v - vals[0]) for v in vals]
    den = es[0] + es[1] + es[2] + es[3]
    onehot = jnp.where(sels[0] | sels[1] | sels[2] | sels[3], 1.0, 0.0)
    tr = lax.broadcasted_iota(i32, (TM, TM), 0)
    tc = lax.broadcasted_iota(i32, (TM, TM), 1)
    before = jnp.where(tr < tc, 1.0, 0.0).astype(bf16)
    pos = base_ref[:, 0:1] + _dot(onehot.astype(bf16), before)
    zrow_i = jnp.zeros((8 - TOP_K, TM), i32)
    ranks = [jnp.sum(jnp.where(s, pos, 0.0), axis=0, keepdims=True).astype(i32) for s in sels]
    idx_ref[...] = jnp.concatenate(idxs + [zrow_i], axis=0)
    rank_ref[...] = jnp.concatenate(ranks + [zrow_i], axis=0)
    gate_ref[...] = jnp.concatenate([e / den for e in es] + [jnp.zeros((8 - TOP_K, TM), f32)], axis=0)
    base = base_ref[...] + jnp.sum(onehot, axis=1, keepdims=True)
    base_ref[...] = base
    cnt_ref[...] = base.astype(i32)


def _merge(x, oap, oas, ogp, ogs, ga, gb, mod, wa, wb, wo, g, rw, rb):
    tile = lambda w: pl.BlockSpec((TM, w), lambda i: (i, 0))
    ctx = pl.BlockSpec((TM, D), lambda i: (jnp.minimum(i, P_TILES - 1), 0))
    lat = pl.BlockSpec((TM, D), lambda i: (jnp.maximum(i - P_TILES, 0), 0))
    res = lambda shape: pl.BlockSpec(shape, lambda i: (0, 0), pipeline_mode=pl.Buffered(1))
    meta = pl.BlockSpec((8, TM), lambda i: (0, i))
    return pl.pallas_call(
        _merge_kernel,
        grid=(N_TILES,),
        in_specs=[tile(D), ctx, lat, ctx, lat, tile(D), tile(D), res((8, 6 * D)), res((D, D)), res((D, D)),
                  res((D, D)), res((1, D)), res((N_EXPERTS, D)), res((N_EXPERTS, LANES))],
        out_specs=[tile(D), pl.BlockSpec((TM * SUB, LANES), lambda i: (i, 0)), meta, meta, meta,
                   pl.BlockSpec((N_EXPERTS, LANES), lambda i: (jnp.where(i < P_TILES, 0, 1), 0))],
        out_shape=[jax.ShapeDtypeStruct((T, D), f32), jax.ShapeDtypeStruct((T * SUB, LANES), f32),
                   jax.ShapeDtypeStruct((8, T), i32), jax.ShapeDtypeStruct((8, T), f32),
                   jax.ShapeDtypeStruct((8, T), i32), jax.ShapeDtypeStruct((2 * N_EXPERTS, LANES), i32)],
        scratch_shapes=[pltpu.VMEM((N_EXPERTS, LANES), f32)],
        compiler_params=_cparams(("arbitrary",)),
        name="merge_route",
    )(x, oap, oas, ogp, ogs, ga, gb, mod, wa, wb, wo, g, rw, rb)


def _experts_kernel(blk_e_ref, nxt_e_ref, blk_nv_ref, nused_ref, src_ref, slot_ref, gates_ref, h_hbm, wgu_hbm, bgu_ref, wdn_hbm,
                    bdn_ref, out_hbm, hv, acc, wgu_st, wdn_st, wgu_bf, wdn_bf, xg, wsem, iosem, *,
                    layer, tok0):
    i = pl.program_id(0)
    n_used = nused_ref[0]

    @pl.when(i < n_used)
    def _():
        _experts_step(i, n_used, blk_e_ref, nxt_e_ref, blk_nv_ref, src_ref, slot_ref, gates_ref, h_hbm, wgu_hbm, bgu_ref, wdn_hbm,
                      bdn_ref, out_hbm, hv, acc, wgu_st, wdn_st, wgu_bf, wdn_bf, xg, wsem, iosem, layer,
                      tok0)


def _experts_step(i, n_used, blk_e_ref, nxt_e_ref, blk_nv_ref, src_ref, slot_ref, gates_ref, h_hbm, wgu_hbm, bgu_ref, wdn_hbm,
                  bdn_ref, out_hbm, hv, acc, wgu_st, wdn_st, wgu_bf, wdn_bf, xg, wsem, iosem, layer, tok0):
    t_str = hv.shape[0] // SUB - 8

    def weight_copies(e):
        return (pltpu.make_async_copy(wgu_hbm.at[layer, e], wgu_st, wsem.at[0]),
                pltpu.make_async_copy(wdn_hbm.at[layer, e], wdn_st, wsem.at[1]))

    e = blk_e_ref[i]

    @pl.when(i == 0)
    def _():
        for cp in weight_copies(e):
            cp.start()
        cp = pltpu.make_async_copy(h_hbm.at[pl.ds(tok0 * SUB, t_str * SUB)], hv.at[pl.ds(0, t_str * SUB)],
                                   iosem.at[0])
        cp.start()
        acc[...] = jnp.zeros_like(acc)
        hv[t_str * SUB:, :] = jnp.zeros((8 * SUB, LANES), f32)
        cp.wait()

    @pl.when((i == 0) | (e != blk_e_ref[jnp.maximum(i - 1, 0)]))
    def _():
        for cp in weight_copies(e):
            cp.wait()
        wgu_bf[...] = wgu_st[...].astype(bf16)
        wdn_bf[...] = wdn_st[...].astype(bf16)
        en = nxt_e_ref[e]

        @pl.when(en >= 0)
        def _():
            for cp in weight_copies(en):
                cp.start()

    def sub_block(r_lo):
        for r in range(MOE_SUB_ROWS):
            o = pl.multiple_of(src_ref[0, 0, r_lo + r] * SUB, SUB)
            xg[r * SUB:(r + 1) * SUB, :] = hv[pl.ds(o, SUB), :]
        xb = jnp.concatenate([xg[pl.ds(s, MOE_SUB_ROWS, stride=SUB), :] for s in range(SUB)],
                             axis=1).astype(bf16)
        gu = _dot(xb, wgu_bf[...]) + bgu_ref[...]
        gt = jnp.minimum(gu[:, :MOE_FF], SWIGLU_LIMIT)
        up = jnp.clip(gu[:, MOE_FF:], -SWIGLU_LIMIT, SWIGLU_LIMIT)
        act = (up + 1.0) * gt * jax.nn.sigmoid(SWIGLU_ALPHA * gt)
        y = _dot(act.astype(bf16), wdn_bf[...]) + bdn_ref[...]
        for s in range(SUB):
            xg[pl.ds(s, MOE_SUB_ROWS, stride=SUB), :] = y[:, s * LANES:(s + 1) * LANES]
        group = 16
        for r0 in range(0, MOE_SUB_ROWS, group):
            offs = [pl.multiple_of(src_ref[0, 0, r_lo + r] * SUB, SUB) for r in range(r0, r0 + group)]
            olds = [acc[pl.ds(o, SUB), :] for o in offs]
            for j, o in enumerate(offs):
                r = r0 + j
                gate = gates_ref[slot_ref[0, 0, r_lo + r]]
                acc[pl.ds(o, SUB), :] = olds[j] + xg[r * SUB:(r + 1) * SUB, :] * gate

    sub_block(0)
    for r_lo in range(MOE_SUB_ROWS, MOE_ROWS, MOE_SUB_ROWS):
        pl.when(blk_nv_ref[i] > r_lo)(functools.partial(sub_block, r_lo))

    @pl.when(i == n_used - 1)
    def _():
        cp = pltpu.make_async_copy(acc.at[pl.ds(0, t_str * SUB)], out_hbm, iosem.at[0])
        cp.start()
        cp.wait()


def _experts(blk_e, nxt_e, blk_nv, n_used, row_src, row_slot, gates, ht, wgu, bgu, wdn, bdn, layer, tok0,
             t_str):
    n_blocks = row_src.shape[0] // MOE_ROWS
    src3 = row_src.reshape(n_blocks, 1, MOE_ROWS)
    slot3 = row_slot.reshape(n_blocks, 1, MOE_ROWS)
    per_block = pl.BlockSpec((1, 1, MOE_ROWS), lambda i, be, ne, nv, nu: (i, 0, 0), memory_space=pltpu.SMEM)
    grid_spec = pltpu.PrefetchScalarGridSpec(
        num_scalar_prefetch=4,
        grid=(n_blocks,),
        in_specs=[
            per_block,
            per_block,
            pl.BlockSpec(memory_space=pltpu.SMEM),
            pl.BlockSpec(memory_space=pl.ANY),
            pl.BlockSpec(memory_space=pl.ANY),
            pl.BlockSpec((None, None, 1, 2 * MOE_FF), lambda i, be, ne, nv, nu: (layer, be[i], 0, 0)),
            pl.BlockSpec(memory_space=pl.ANY),
            pl.BlockSpec((None, None, 1, D), lambda i, be, ne, nv, nu: (layer, be[i], 0, 0)),
        ],
        out_specs=pl.BlockSpec(memory_space=pl.ANY),
        scratch_shapes=[
            pltpu.VMEM(((t_str + 8) * SUB, LANES), f32),
            pltpu.VMEM(((t_str + 8) * SUB, LANES), f32),
            pltpu.VMEM((D, 2 * MOE_FF), f32),
            pltpu.VMEM((MOE_FF, D), f32),
            pltpu.VMEM((D, 2 * MOE_FF), bf16),
            pltpu.VMEM((MOE_FF, D), bf16),
            pltpu.VMEM((MOE_SUB_ROWS * SUB, LANES), f32),
            pltpu.SemaphoreType.DMA((2,)),
            pltpu.SemaphoreType.DMA((1,)),
        ],
    )
    return pl.pallas_call(
        functools.partial(_experts_kernel, layer=layer, tok0=tok0),
        grid_spec=grid_spec,
        out_shape=jax.ShapeDtypeStruct((t_str * SUB, LANES), f32),
        compiler_params=_cparams(("arbitrary",), _EXPERTS_VMEM_LIMIT),
        name="experts",
    )(blk_e, nxt_e, blk_nv, n_used, src3, slot3, gates, ht, wgu, bgu.reshape(DEPTH, N_EXPERTS, 1, 2 * MOE_FF),
      wdn, bdn.reshape(DEPTH, N_EXPERTS, 1, D))


def _combine_kernel(x1_ref, moep_ref, moes_ref, mod_ref, fg_ref, o_ref, *, final):
    i = pl.program_id(0)
    rows = lambda ref: jnp.concatenate([ref[pl.ds(s, TM, stride=SUB), :] for s in range(SUB)], axis=1)
    moe = jnp.where(i < P_TILES, rows(moep_ref), rows(moes_ref))
    m = mod_ref[pl.ds(_mod_row(i), 1), :]
    x2 = x1_ref[...] + m[:, 5 * D:6 * D] * moe
    if final:
        x2 = x2 * lax.rsqrt(jnp.mean(x2 * x2, axis=-1, keepdims=True) + EPS) * fg_ref[...]
    o_ref[...] = x2


def _combine(x1, moep, moes, mod, fg, final):
    tile = pl.BlockSpec((TM, D), lambda i: (i, 0))
    res = lambda shape: pl.BlockSpec(shape, lambda i: (0, 0), pipeline_mode=pl.Buffered(1))
    ctx = pl.BlockSpec((TM * SUB, LANES), lambda i: (jnp.minimum(i, P_TILES - 1), 0))
    lat = pl.BlockSpec((TM * SUB, LANES), lambda i: (jnp.maximum(i - P_TILES, 0), 0))
    return pl.pallas_call(
        functools.partial(_combine_kernel, final=final),
        grid=(N_TILES,),
        in_specs=[tile, ctx, lat, res((8, 6 * D)), res((1, D))],
        out_specs=tile,
        out_shape=jax.ShapeDtypeStruct((T, D), f32),
        compiler_params=_cparams(("arbitrary",)),
        name="combine",
    )(x1, moep, moes, mod, fg)


def _rope_tables():
    nf = HD_A // 4
    inv = ROPE_BASE ** (-jnp.arange(nf, dtype=f32) / nf)
    n = jnp.arange(DEC_SEQ)
    row = (n // GRID_W).astype(f32)[:, None] * inv
    col = (n % GRID_W).astype(f32)[:, None] * inv
    cos64 = jnp.concatenate([jnp.cos(row), jnp.cos(row), jnp.cos(col), jnp.cos(col)], axis=-1)
    sin64 = jnp.concatenate([-jnp.sin(row), jnp.sin(row), -jnp.sin(col), jnp.sin(col)], axis=-1)
    cos_s = jnp.tile(jnp.tile(cos64, (1, 2)), (DEC_BATCH, 1))
    sin_s = jnp.tile(jnp.tile(sin64, (1, 2)), (DEC_BATCH, 1))
    cos = jnp.concatenate([jnp.ones((T_P, LANES), f32), cos_s], axis=0)
    sin = jnp.concatenate([jnp.zeros((T_P, LANES), f32), sin_s], axis=0)
    return cos, sin


def _routing_tables(idx, rank, gate, counts):
    t_str = idx.shape[1]
    n_blocks = (t_str * TOP_K + N_EXPERTS * (MOE_ROWS - 1)) // MOE_ROWS + 1
    padded = (counts + MOE_ROWS - 1) // MOE_ROWS * MOE_ROWS
    pad_end = jnp.cumsum(padded)
    pad_start = pad_end - padded
    eids = jnp.arange(N_EXPERTS, dtype=i32)
    start_of = jnp.sum(jnp.where(idx[..., None] == eids, pad_start, 0), axis=-1)
    dest = (start_of + rank).reshape(-1)
    inv = jnp.full((n_blocks * MOE_ROWS,), -1, i32).at[dest].set(jnp.arange(TOP_K * t_str, dtype=i32))
    valid = inv >= 0
    safe = jnp.maximum(inv, 0)
    row_src = jnp.where(valid, safe % t_str, t_str).astype(i32)
    row_slot = jnp.where(valid, inv, TOP_K * t_str).astype(i32)
    gates = jnp.concatenate([gate.reshape(-1), jnp.zeros((8,), f32)])
    starts = jnp.arange(n_blocks, dtype=i32) * MOE_ROWS
    blk_e = jnp.sum((pad_end[None, :] <= starts[:, None]).astype(i32), axis=1)
    present = counts > 0
    blk_e = jnp.minimum(blk_e, jnp.max(jnp.where(present, eids, 0))).astype(i32)
    later = (eids[None, :] > eids[:, None]) & present[None, :]
    nxt = jnp.min(jnp.where(later, eids[None, :], N_EXPERTS), axis=1)
    nxt_e = jnp.where(nxt < N_EXPERTS, nxt, -1).astype(i32)
    n_used = (pad_end[-1:] // MOE_ROWS).astype(i32)
    cnt_of = jnp.sum(jnp.where(blk_e[:, None] == eids, (counts + pad_start)[None, :], 0), axis=1)
    blk_nv = jnp.clip(cnt_of - starts, 0, MOE_ROWS).astype(i32)
    return blk_e, nxt_e, blk_nv, n_used, row_src, row_slot, gates


def kernel(x_prompt, x_sample, cache_k, cache_v, state_fwd, state_bwd, c, c_ctx, ada_w, ada_b, norm_mix_g,
           norm_ffn_g, w_in, diff_lambda, diff_subln_g, gla_gk_w, gla_gk_b, gla_norm_g, w_branch_a,
           w_branch_b, w_out, router_w, router_b, expert_w_gu, expert_b_gu, expert_w_down, expert_b_down,
           final_norm_g):
    x = jnp.concatenate([x_prompt.reshape(T_P, D), x_sample.reshape(T_S, D)], axis=0)
    cvec = jnp.concatenate([c_ctx[None], c, jnp.zeros((3, D), f32)], axis=0)
    mods = _ada_mod(cvec, ada_w, ada_b)
    cos, sin = _rope_tables()
    ck = cache_k.reshape(DEC_BATCH, DEPTH, PAST_LEN, D)
    cv = cache_v.reshape(DEC_BATCH, DEPTH, PAST_LEN, D)
    main_w = 6 * D
    lr0 = main_w
    g0 = main_w + 2 * GK_RANK

    ks, vs, sfs, sbs = [], [], [], []
    yn = None
    for l in range(DEPTH):
        lam_init = 0.8 - 0.6 * math.exp(-0.3 * l)
        mod = mods[l]
        wm = w_in[l, :, :main_w].astype(bf16)
        wl = jnp.pad(w_in[l, :, lr0:g0], ((0, 0), (0, LANES - 2 * GK_RANK))).astype(bf16)
        wg = w_in[l, :, g0:].astype(bf16)
        q, k, v, bq, bk, bv, bg, lr, ga, gb = _inproj(x, mod, norm_mix_g[l][None], wm, wg, wl, cos, sin)
        ks.append(k[:T_P].reshape(BATCH, SEQ, D))
        vs.append(v[:T_P].reshape(BATCH, SEQ, D))

        lq = jnp.pad(diff_lambda[l], ((0, 4), (0, LANES - HD_A)))
        sg = diff_subln_g[l][None]
        oap = _attn_ctx(q, k, v, lq, sg, lam_init)
        oas = _attn_lat(q, k, v, ck, cv, lq, sg, lam_init, l)

        wf = jnp.pad(gla_gk_w[l, 0], ((0, LANES - GK_RANK), (0, 0)))
        wb = jnp.pad(gla_gk_w[l, 1], ((GK_RANK, LANES - 2 * GK_RANK), (0, 0)))
        bfv = gla_gk_b[l, 0][None]
        bbv = gla_gk_b[l, 1][None]
        gn = gla_norm_g[l][None]
        ogp, sf, sb = _gla(bq, bk, bv, bg, lr, wf, wb, bfv, bbv, gn, SEQ, 0, BATCH)
        ogs = _gla(bq, bk, bv, bg, lr, wf, wb, bfv, bbv, gn, DEC_SEQ, T_P // DEC_SEQ, DEC_BATCH,
                   states=(state_fwd, state_bwd), layer=l)
        sfs.append(sf)
        sbs.append(sb)

        rb = jnp.broadcast_to(router_b[l][:, None], (N_EXPERTS, LANES))
        x1, ht, idx8, gate8, rank8, cnt = _merge(
            x, oap, oas, ogp, ogs, ga, gb, mod, w_branch_a[l].astype(bf16), w_branch_b[l].astype(bf16),
            w_out[l].astype(bf16), norm_ffn_g[l][None], router_w[l].T, rb)
        moes = []
        for tok0, t_str, cnt_s in ((0, T_P, cnt[:N_EXPERTS, 0]), (T_P, T_S, cnt[N_EXPERTS:, 0])):
            sl = slice(tok0, tok0 + t_str)
            tables = _routing_tables(idx8[:TOP_K, sl], rank8[:TOP_K, sl], gate8[:TOP_K, sl], cnt_s)
            moes.append(_experts(*tables, ht, expert_w_gu, expert_b_gu, expert_w_down, expert_b_down, l,
                                 tok0, t_str))
        x = _combine(x1, moes[0], moes[1], mod, final_norm_g[None], final=(l == DEPTH - 1))
    yn = x

    y_prompt = yn[:T_P].reshape(BATCH, SEQ, D)
    y_sample = yn[T_P:].reshape(DEC_BATCH, DEC_SEQ, D)
    new_k = jnp.stack(ks, axis=1).reshape(BATCH, DEPTH, SEQ, N_HEADS_A, 2, HD_A)
    new_v = jnp.stack(vs, axis=1).reshape(BATCH, DEPTH, SEQ, N_HEADS_A, 2 * HD_A)
    new_sf = jnp.stack(sfs, axis=1)
    new_sb = jnp.stack(sbs, axis=1)
    return (y_prompt, y_sample, new_k, new_v, new_sf, new_sb)
```

```python
import functools
import math

import jax
import jax.numpy as jnp
from jax import lax
from jax.experimental import pallas as pl
from jax.experimental.pallas import tpu as pltpu

f32 = jnp.float32
bf16 = jnp.bfloat16
i32 = jnp.int32

D = 1024
BATCH = 16
SEQ = 256
DEPTH = 2
DEC_BATCH = 4
DEC_SEQ = 1024
PAST_LEN = 512
GRID_W = 64
HD_A = 64
N_HEADS_A = 8
ROPE_BASE = 10000.0
N_HEADS_B = 4
DK_B = 128
DV_B = 256
GK_RANK = 16
GATE_NORM = 16.0
CHUNK = 64
N_EXPERTS = 32
TOP_K = 4
MOE_FF = 1024
SWIGLU_LIMIT = 7.0
SWIGLU_ALPHA = 1.702
EPS = 1e-6

T_P = BATCH * SEQ
T_S = DEC_BATCH * DEC_SEQ
T = T_P + T_S
TM = 512
TQ = 256
N_TILES = T // TM
P_TILES = T_P // TM
S_TILES_PER_BATCH = DEC_SEQ // TM

MOE_ROWS = 512
MOE_SUB_ROWS = 256
SUB = 8
LANES = 128
FAST_DECAY_LIMIT = -60.0

_VMEM_LIMIT = 56 * 1024 * 1024
_EXPERTS_VMEM_LIMIT = 62 * 1024 * 1024


def _cparams(sem, vmem=_VMEM_LIMIT):
    return pltpu.CompilerParams(dimension_semantics=sem, vmem_limit_bytes=vmem)


def _dot(a, b):
    return jnp.dot(a, b, preferred_element_type=f32)


def _dot_nt(a, b):
    return lax.dot_general(a, b, (((1,), (1,)), ((), ())), preferred_element_type=f32)


def _split(x):
    hi = x.astype(bf16)
    lo = (x - hi.astype(f32)).astype(bf16)
    return hi, lo


def _dot3(a, b):
    ah, al = _split(a)
    bh, bl = _split(b)
    return _dot(ah, bh) + _dot(ah, bl) + _dot(al, bh)


def _dot3_nt(a, b):
    ah, al = _split(a)
    bh, bl = _split(b)
    return _dot_nt(ah, bh) + _dot_nt(ah, bl) + _dot_nt(al, bh)


def _silu(x):
    return x * jax.nn.sigmoid(x)


def _mod_row(i):
    shift = S_TILES_PER_BATCH.bit_length() - 1
    return jnp.where(i < P_TILES, 0, 1 + lax.shift_right_logical(jnp.maximum(i - P_TILES, 0), shift))


def _ada_kernel(c_ref, w_ref, b_ref, o_ref):
    o_ref[0] = _dot3(_silu(c_ref[...]), w_ref[0]) + b_ref[0]


def _ada_mod(cvec, ada_w, ada_b):
    tn = 1536
    return pl.pallas_call(
        _ada_kernel,
        grid=(DEPTH, 6 * D // tn),
        in_specs=[
            pl.BlockSpec((8, D), lambda l, j: (0, 0)),
            pl.BlockSpec((1, D, tn), lambda l, j: (l, 0, j)),
            pl.BlockSpec((1, 1, tn), lambda l, j: (l, 0, j)),
        ],
        out_specs=pl.BlockSpec((1, 8, tn), lambda l, j: (l, 0, j)),
        out_shape=jax.ShapeDtypeStruct((DEPTH, 8, 6 * D), f32),
        compiler_params=_cparams(("arbitrary", "arbitrary")),
        name="ada_mod",
    )(cvec, ada_w, ada_b.reshape(DEPTH, 1, 6 * D))


def _inproj_kernel(x_ref, mod_ref, g_ref, wm_ref, wg_ref, wl_ref, cos_ref, sin_ref,
                   q_ref, k_ref, v_ref, bq_ref, bk_ref, bv_ref, bg_ref, lr_ref, ga_ref, gb_ref):
    i = pl.program_id(0)
    m = mod_ref[pl.ds(_mod_row(i), 1), :]
    shift = m[:, 0:D]
    scale = m[:, D:2 * D]
    x = x_ref[...]
    h = x * lax.rsqrt(jnp.mean(x * x, axis=-1, keepdims=True) + EPS) * g_ref[...]
    hb = (h * (1.0 + scale) + shift).astype(bf16)

    cos = cos_ref[...]
    sin = sin_ref[...]
    lane = lax.broadcasted_iota(i32, (TM, LANES), 1)
    lo_mask = (lane & 31) < 16

    def rope(a):
        partner = jnp.where(lo_mask, pltpu.roll(a, LANES - 16, 1), pltpu.roll(a, 16, 1))
        return a * cos + partner * sin

    aq = _dot(hb, wm_ref[:, 0:D])
    for c in range(D // LANES):
        q_ref[:, c * LANES:(c + 1) * LANES] = rope(aq[:, c * LANES:(c + 1) * LANES]).astype(bf16)
    ak = _dot(hb, wm_ref[:, D:2 * D])
    for c in range(D // LANES):
        k_ref[:, c * LANES:(c + 1) * LANES] = rope(ak[:, c * LANES:(c + 1) * LANES])
    v_ref[...] = _dot(hb, wm_ref[:, 2 * D:3 * D])
    bq_ref[...] = _dot(hb, wm_ref[:, 3 * D:3 * D + 512]).astype(bf16)
    bk_ref[...] = _dot(hb, wm_ref[:, 3 * D + 512:4 * D]).astype(bf16)
    bv_ref[...] = _dot(hb, wm_ref[:, 4 * D:5 * D]).astype(bf16)
    bg_ref[...] = _dot(hb, wm_ref[:, 5 * D:6 * D]).astype(bf16)
    lr_ref[...] = _dot(hb, wl_ref[...])
    ga_ref[...] = _dot(hb, wg_ref[:, 0:D]).astype(bf16)
    gb_ref[...] = _dot(hb, wg_ref[:, D:2 * D]).astype(bf16)


def _inproj(x, mod, g, wm, wg, wl, cos, sin):
    tile = lambda w: pl.BlockSpec((TM, w), lambda i: (i, 0))
    res = lambda shape: pl.BlockSpec(shape, lambda i: (0, 0), pipeline_mode=pl.Buffered(1))
    widths = (D, D, D, 512, 512, D, D, LANES, D, D)
    dtypes = (bf16, f32, f32, bf16, bf16, bf16, bf16, f32, bf16, bf16)
    return pl.pallas_call(
        _inproj_kernel,
        grid=(N_TILES,),
        in_specs=[tile(D), res((8, 6 * D)), res((1, D)), res((D, 6 * D)), res((D, 2 * D)),
                  res((D, LANES)), tile(LANES), tile(LANES)],
        out_specs=[tile(w) for w in widths],
        out_shape=[jax.ShapeDtypeStruct((T, w), dt) for w, dt in zip(widths, dtypes)],
        compiler_params=_cparams(("arbitrary",)),
        name="inproj",
    )(x, mod, g, wm, wg, wl, cos, sin)


def _lam_from(lq_ref, lam_init):
    lq = lq_ref[...]
    s01 = jnp.sum(lq[0:1] * lq[1:2], axis=-1, keepdims=True)
    s23 = jnp.sum(lq[2:3] * lq[3:4], axis=-1, keepdims=True)
    return jnp.exp(s01) - jnp.exp(s23) + lam_init


def _subln(o, g, lam_init):
    o = o * lax.rsqrt(jnp.mean(o * o, axis=-1, keepdims=True) + EPS)
    return o * g * (1.0 - lam_init)


def _diff_heads(q_ref, segs, lq_ref, g_ref, o_ref, lam_init):
    lam = _lam_from(lq_ref, lam_init)
    g = g_ref[...]
    tq = q_ref.shape[0]
    lo = lax.broadcasted_iota(i32, (tq, LANES), 1) < HD_A
    for h in range(N_HEADS_A):
        cols = slice(h * LANES, (h + 1) * LANES)
        q = q_ref[:, cols] * (HD_A ** -0.5)
        kv = [(k_ref[:, cols], v_ref[:, cols]) for k_ref, v_ref in segs]
        qm = jnp.concatenate([jnp.where(lo, q, 0), jnp.where(lo, 0, q)], axis=0)
        ss = [_dot_nt(qm, kb) for kb, _ in kv]
        mx = functools.reduce(jnp.maximum, [jnp.max(s, axis=-1, keepdims=True) for s in ss])
        ps = [jnp.exp(s - mx) for s in ss]
        l = functools.reduce(jnp.add, [jnp.sum(p, axis=-1, keepdims=True) for p in ps])
        pv = functools.reduce(jnp.add, [_dot(p.astype(bf16), vb) for p, (_, vb) in zip(ps, kv)]) / l
        o = pv[:tq] - lam * pv[tq:]
        o_ref[:, cols] = _subln(o, g, lam_init).astype(bf16)


def _attn_ctx_kernel(q_ref, k_ref, v_ref, lq_ref, g_ref, o_ref, kb_ref, vb_ref, *, lam_init):
    kb_ref[...] = k_ref[...].astype(bf16)
    vb_ref[...] = v_ref[...].astype(bf16)
    _diff_heads(q_ref, [(kb_ref, vb_ref)], lq_ref, g_ref, o_ref, lam_init)


def _attn_lat_kernel(q_ref, k_ref, v_ref, ck_ref, cv_ref, lq_ref, g_ref, o_ref, kb_ref, vb_ref, *, lam_init):
    @pl.when(pl.program_id(1) == 0)
    def _():
        kb_ref[:PAST_LEN, :] = ck_ref[...].astype(bf16)
        vb_ref[:PAST_LEN, :] = cv_ref[...].astype(bf16)
        kb_ref[PAST_LEN:, :] = k_ref[...].astype(bf16)
        vb_ref[PAST_LEN:, :] = v_ref[...].astype(bf16)

    _diff_heads(q_ref, [(kb_ref, vb_ref)], lq_ref, g_ref, o_ref, lam_init)


def _attn_ctx(q, k, v, lq, g, lam_init):
    blk = pl.BlockSpec((SEQ, D), lambda b: (b, 0))
    small = lambda s: pl.BlockSpec(s, lambda b: (0, 0))
    return pl.pallas_call(
        functools.partial(_attn_ctx_kernel, lam_init=lam_init),
        grid=(BATCH,),
        in_specs=[blk, blk, blk, small((8, LANES)), small((1, LANES))],
        out_specs=blk,
        out_shape=jax.ShapeDtypeStruct((T_P, D), bf16),
        scratch_shapes=[pltpu.VMEM((SEQ, D), bf16), pltpu.VMEM((SEQ, D), bf16)],
        compiler_params=_cparams(("arbitrary",)),
        name="attn_ctx",
    )(q, k, v, lq, g)


def _attn_lat(q, k, v, ck, cv, lq, g, lam_init, layer):
    nq = DEC_SEQ // TQ
    qblk = pl.BlockSpec((TQ, D), lambda b, t: (T_P // TQ + b * nq + t, 0))
    kblk = pl.BlockSpec((DEC_SEQ, D), lambda b, t: (T_P // DEC_SEQ + b, 0))
    cblk = pl.BlockSpec((None, None, PAST_LEN, D), lambda b, t: (b, layer, 0, 0))
    small = lambda s: pl.BlockSpec(s, lambda b, t: (0, 0))
    return pl.pallas_call(
        functools.partial(_attn_lat_kernel, lam_init=lam_init),
        grid=(DEC_BATCH, nq),
        in_specs=[qblk, kblk, kblk, cblk, cblk, small((8, LANES)), small((1, LANES))],
        out_specs=pl.BlockSpec((TQ, D), lambda b, t: (b * nq + t, 0)),
        out_shape=jax.ShapeDtypeStruct((T_S, D), bf16),
        scratch_shapes=[pltpu.VMEM((PAST_LEN + DEC_SEQ, D), bf16), pltpu.VMEM((PAST_LEN + DEC_SEQ, D), bf16)],
        compiler_params=_cparams(("arbitrary", "arbitrary")),
        name="attn_lat",
    )(q, k, v, ck, cv, lq, g)


def _log_decay(lr, w_ref, b_ref):
    z = _dot3(lr, w_ref[...]) + b_ref[...]
    return -(jnp.maximum(-z, 0.0) + jnp.log(1.0 + jnp.exp(-jnp.abs(z)))) / GATE_NORM


def _gla_direction(q_ref, k_ref, v3b, g, st0, reverse, b_ref, att_ref, sall_ref):
    tb = g.shape[0]
    n = tb // CHUNK
    scale = DK_B ** -0.5
    ri = lax.broadcasted_iota(i32, (CHUNK, CHUNK), 0)
    ci = lax.broadcasted_iota(i32, (CHUNK, CHUNK), 1)
    keep = (ci >= ri) if reverse else (ci <= ri)
    tri = jnp.broadcast_to(jnp.where(keep, 1.0, 0.0).astype(bf16)[None], (n, CHUNK, CHUNK))

    g3 = g.reshape(n, CHUNK, DK_B)
    ghi, glo = _split(g3)
    bdot = lambda a, b: lax.dot_general(a, b, (((2,), (1,)), ((0,), (0,))), preferred_element_type=f32)
    b = bdot(tri, ghi) + bdot(tri, glo)
    tot = b[:, 0:1, :] if reverse else b[:, CHUNK - 1:CHUNK, :]
    q3 = q_ref[...].astype(f32).reshape(n, CHUNK, DK_B) * scale
    k3 = k_ref[...].astype(f32).reshape(n, CHUNK, DK_B)

    qs = (q3 * jnp.exp(b)).astype(bf16)
    kl = (k3 * jnp.exp(tot - b)).astype(bf16)
    kn = (k3 * jnp.exp(jnp.minimum(-b, 80.0))).astype(bf16)
    att = lax.dot_general(qs, kn, (((2,), (2,)), ((0,), (0,))), preferred_element_type=f32)
    att_ref[...] = jnp.where(keep[None], att, 0.0)

    b_ref[...] = b
    need_fix = jnp.min(tot) < FAST_DECAY_LIMIT

    @pl.when(need_fix)
    def _():
        lane = lax.broadcasted_iota(i32, (CHUNK, CHUNK), 1)

        def chunk_body(c, carry):
            bc = b_ref[c]
            row0 = pl.multiple_of(c * CHUNK, CHUNK)
            tot_c = bc[0:1] if reverse else bc[CHUNK - 1:CHUNK]

            @pl.when(jnp.min(tot_c) < FAST_DECAY_LIMIT)
            def _():
                qc = q_ref[pl.ds(row0, CHUNK), :].astype(f32) * scale
                kc = k_ref[pl.ds(row0, CHUNK), :].astype(f32)
                row = lax.broadcasted_iota(i32, (CHUNK, DK_B), 0)

                def jbody(j, acc):
                    bj = b_ref[c, pl.ds(j, 1), :]
                    kj = jnp.sum(jnp.where(row == j, kc, 0.0), axis=0, keepdims=True)
                    e = jnp.exp(jnp.minimum(bc - bj, 0.0))
                    col = jnp.sum(qc * e * kj, axis=-1, keepdims=True)
                    return acc + jnp.where(lane == j, col, 0.0)

                acc = lax.fori_loop(0, CHUNK, jbody, jnp.zeros((CHUNK, CHUNK), f32))
                att_ref[c] = jnp.where(keep, acc, 0.0)

            return carry

        lax.fori_loop(0, n, chunk_body, 0)

    u = lax.dot_general(v3b, kl, (((1,), (1,)), ((0,), (0,))), preferred_element_type=f32)
    dec = jnp.exp(tot)
    st = st0
    order = range(n - 1, -1, -1) if reverse else range(n)
    for c in order:
        sall_ref[c] = st
        st = st * dec[c] + u[c]
    inter = lax.dot_general(qs, sall_ref[...].astype(bf16), (((2,), (2,)), ((0,), (0,))),
                            preferred_element_type=f32)
    intra = lax.dot_general(att_ref[...].astype(bf16), v3b, (((2,), (1,)), ((0,), (0,))),
                            preferred_element_type=f32)
    return (inter + intra).reshape(tb, DV_B), st


def _gla_kernel(*refs, has_init, hp):
    if has_init:
        (q_ref, k_ref, v_ref, bg_ref, lr_ref, wf_ref, wb_ref, bf_ref, bb_ref, gn_ref, sf0_ref, sb0_ref,
         o_ref, b_ref, att_ref, sall_ref) = refs
    else:
        (q_ref, k_ref, v_ref, bg_ref, lr_ref, wf_ref, wb_ref, bf_ref, bb_ref, gn_ref,
         o_ref, sf_ref, sb_ref, b_ref, att_ref, sall_ref) = refs
    tb = q_ref.shape[0]
    n = tb // CHUNK
    lr = lr_ref[...]
    gn = gn_ref[...]
    gf_all = _log_decay(lr, wf_ref, bf_ref)
    gb_all = _log_decay(lr, wb_ref, bb_ref)
    for h in range(hp):
        kc = slice(h * DK_B, (h + 1) * DK_B)
        vc = slice(h * DV_B, (h + 1) * DV_B)
        if has_init:
            stf0 = sf0_ref[...].T
            stb0 = sb0_ref[...].T
        else:
            stf0 = jnp.zeros((DV_B, DK_B), f32)
            stb0 = stf0
        qh = q_ref.at[:, kc]
        kh = k_ref.at[:, kc]
        v3b = v_ref[:, vc].reshape(n, CHUNK, DV_B).astype(bf16)
        of, stf = _gla_direction(qh, kh, v3b, gf_all[:, kc], stf0, False, b_ref.at[h, 0], att_ref.at[h, 0],
                                 sall_ref.at[h, 0])
        ob, stb = _gla_direction(qh, kh, v3b, gb_all[:, kc], stb0, True, b_ref.at[h, 1], att_ref.at[h, 1],
                                 sall_ref.at[h, 1])
        o = of + ob
        o = o * lax.rsqrt(jnp.mean(o * o, axis=-1, keepdims=True) + EPS) * gn
        o_ref[:, vc] = (o * _silu(bg_ref[:, vc].astype(f32))).astype(bf16)
        if not has_init:
            sf_ref[h] = stf.T
            sb_ref[h] = stb.T


def _gla(bq, bk, bv, bg, lr, wf, wb, bfv, bbv, gn, tb, row_blk0, nb, states=None, layer=0):
    has_init = states is not None
    hp = 1 if has_init else N_HEADS_B
    n = tb // CHUNK
    rows = lambda w: pl.BlockSpec((tb, hp * w), lambda b, h: (row_blk0 + b, h))
    lrs = pl.BlockSpec((tb, LANES), lambda b, h: (row_blk0 + b, 0))
    wsp = pl.BlockSpec((LANES, hp * DK_B), lambda b, h: (0, h))
    bsp = pl.BlockSpec((1, hp * DK_B), lambda b, h: (0, h))
    gsp = pl.BlockSpec((1, DV_B), lambda b, h: (0, 0))
    in_specs = [rows(DK_B), rows(DK_B), rows(DV_B), rows(DV_B), lrs, wsp, wsp, bsp, bsp, gsp]
    args = [bq, bk, bv, bg, lr, wf, wb, bfv, bbv, gn]
    osp = pl.BlockSpec((tb, hp * DV_B), lambda b, h: (b, h))
    oshape = jax.ShapeDtypeStruct((nb * tb, D), bf16)
    if has_init:
        ssp = pl.BlockSpec((None, None, None, DK_B, DV_B), lambda b, h: (b, layer, h, 0, 0))
        in_specs += [ssp, ssp]
        args += list(states)
        out_specs, out_shape = osp, oshape
    else:
        ssp = pl.BlockSpec((None, hp, DK_B, DV_B), lambda b, h: (b, h, 0, 0))
        sshape = jax.ShapeDtypeStruct((nb, N_HEADS_B, DK_B, DV_B), f32)
        out_specs, out_shape = [osp, ssp, ssp], [oshape, sshape, sshape]
    return pl.pallas_call(
        functools.partial(_gla_kernel, has_init=has_init, hp=hp),
        grid=(nb, N_HEADS_B // hp),
        in_specs=in_specs,
        out_specs=out_specs,
        out_shape=out_shape,
        scratch_shapes=[pltpu.VMEM((hp, 2, n, CHUNK, DK_B), f32), pltpu.VMEM((hp, 2, n, CHUNK, CHUNK), f32),
                        pltpu.VMEM((hp, 2, n, DV_B, DK_B), f32)],
        compiler_params=_cparams(("arbitrary", "arbitrary")),
        name="gla_lat" if has_init else "gla_ctx",
    )(*args)


def _merge_kernel(x_ref, oap_ref, oas_ref, ogp_ref, ogs_ref, ga_ref, gb_ref, mod_ref, wa_ref, wb_ref,
                  wo_ref, g_ref, rw_ref, rb_ref,
                  x1_ref, ht_ref, idx_ref, gate_ref, rank_ref, cnt_ref, base_ref):
    i = pl.program_id(0)

    @pl.when((i == 0) | (i == P_TILES))
    def _():
        base_ref[...] = jnp.zeros_like(base_ref)

    is_ctx = i < P_TILES
    oa = jnp.where(is_ctx, oap_ref[...], oas_ref[...])
    og = jnp.where(is_ctx, ogp_ref[...], ogs_ref[...])
    ya = _dot(oa, wa_ref[...])
    yb = _dot(og, wb_ref[...])
    merged = jax.nn.sigmoid(ga_ref[...].astype(f32)) * ya + jax.nn.sigmoid(gb_ref[...].astype(f32)) * yb
    mix = _dot(merged.astype(bf16), wo_ref[...])
    m = mod_ref[pl.ds(_mod_row(i), 1), :]
    x1 = x_ref[...] + m[:, 2 * D:3 * D] * mix
    x1_ref[...] = x1
    h = x1 * lax.rsqrt(jnp.mean(x1 * x1, axis=-1, keepdims=True) + EPS) * g_ref[...]
    h = h * (1.0 + m[:, 4 * D:5 * D]) + m[:, 3 * D:4 * D]
    for s in range(SUB):
        ht_ref[pl.ds(s, TM, stride=SUB), :] = h[:, s * LANES:(s + 1) * LANES]

    lg = _dot3_nt(rw_ref[...], h) + rb_ref[...][:, 0:1]
    eid = lax.broadcasted_iota(i32, (N_EXPERTS, TM), 0)
    vals, sels, idxs = [], [], []
    for _ in range(TOP_K):
        mx = jnp.max(lg, axis=0, keepdims=True)
        ik = jnp.min(jnp.where(lg == mx, eid, N_EXPERTS), axis=0, keepdims=True)
        sel = eid == ik
        vals.append(mx)
        idxs.append(ik)
        sels.append(sel)
        lg = jnp.where(sel, -jnp.inf, lg)
    es = [jnp.exp(v - vals[0]) for v in vals]
    den = es[0] + es[1] + es[2] + es[3]
    onehot = jnp.where(sels[0] | sels[1] | sels[2] | sels[3], 1.0, 0.0)
    tr = lax.broadcasted_iota(i32, (TM, TM), 0)
    tc = lax.broadcasted_iota(i32, (TM, TM), 1)
    before = jnp.where(tr < tc, 1.0, 0.0).astype(bf16)
    pos = base_ref[:, 0:1] + _dot(onehot.astype(bf16), before)
    zrow_i = jnp.zeros((8 - TOP_K, TM), i32)
    ranks = [jnp.sum(jnp.where(s, pos, 0.0), axis=0, keepdims=True).astype(i32) for s in sels]
    idx_ref[...] = jnp.concatenate(idxs + [zrow_i], axis=0)
    rank_ref[...] = jnp.concatenate(ranks + [zrow_i], axis=0)
    gate_ref[...] = jnp.concatenate([e / den for e in es] + [jnp.zeros((8 - TOP_K, TM), f32)], axis=0)
    base = base_ref[...] + jnp.sum(onehot, axis=1, keepdims=True)
    base_ref[...] = base
    cnt_ref[...] = base.astype(i32)


def _merge(x, oap, oas, ogp, ogs, ga, gb, mod, wa, wb, wo, g, rw, rb):
    tile = lambda w: pl.BlockSpec((TM, w), lambda i: (i, 0))
    ctx = pl.BlockSpec((TM, D), lambda i: (jnp.minimum(i, P_TILES - 1), 0))
    lat = pl.BlockSpec((TM, D), lambda i: (jnp.maximum(i - P_TILES, 0), 0))
    res = lambda shape: pl.BlockSpec(shape, lambda i: (0, 0), pipeline_mode=pl.Buffered(1))
    meta = pl.BlockSpec((8, TM), lambda i: (0, i))
    return pl.pallas_call(
        _merge_kernel,
        grid=(N_TILES,),
        in_specs=[tile(D), ctx, lat, ctx, lat, tile(D), tile(D), res((8, 6 * D)), res((D, D)), res((D, D)),
                  res((D, D)), res((1, D)), res((N_EXPERTS, D)), res((N_EXPERTS, LANES))],
        out_specs=[tile(D), pl.BlockSpec((TM * SUB, LANES), lambda i: (i, 0)), meta, meta, meta,
                   pl.BlockSpec((N_EXPERTS, LANES), lambda i: (jnp.where(i < P_TILES, 0, 1), 0))],
        out_shape=[jax.ShapeDtypeStruct((T, D), f32), jax.ShapeDtypeStruct((T * SUB, LANES), f32),
                   jax.ShapeDtypeStruct((8, T), i32), jax.ShapeDtypeStruct((8, T), f32),
                   jax.ShapeDtypeStruct((8, T), i32), jax.ShapeDtypeStruct((2 * N_EXPERTS, LANES), i32)],
        scratch_shapes=[pltpu.VMEM((N_EXPERTS, LANES), f32)],
        compiler_params=_cparams(("arbitrary",)),
        name="merge_route",
    )(x, oap, oas, ogp, ogs, ga, gb, mod, wa, wb, wo, g, rw, rb)


def _experts_kernel(blk_e_ref, nxt_e_ref, blk_nv_ref, nused_ref, src_ref, slot_ref, gates_ref, h_hbm, wgu_hbm, bgu_ref, wdn_hbm,
                    bdn_ref, out_hbm, hv, acc, wgu_st, wdn_st, wgu_bf, wdn_bf, xg, wsem, iosem, *,
                    layer, tok0):
    i = pl.program_id(0)
    n_used = nused_ref[0]

    @pl.when(i < n_used)
    def _():
        _experts_step(i, n_used, blk_e_ref, nxt_e_ref, blk_nv_ref, src_ref, slot_ref, gates_ref, h_hbm, wgu_hbm, bgu_ref, wdn_hbm,
                      bdn_ref, out_hbm, hv, acc, wgu_st, wdn_st, wgu_bf, wdn_bf, xg, wsem, iosem, layer,
                      tok0)


def _experts_step(i, n_used, blk_e_ref, nxt_e_ref, blk_nv_ref, src_ref, slot_ref, gates_ref, h_hbm, wgu_hbm, bgu_ref, wdn_hbm,
                  bdn_ref, out_hbm, hv, acc, wgu_st, wdn_st, wgu_bf, wdn_bf, xg, wsem, iosem, layer, tok0):
    t_str = hv.shape[0] // SUB - 8

    def weight_copies(e):
        return (pltpu.make_async_copy(wgu_hbm.at[layer, e], wgu_st, wsem.at[0]),
                pltpu.make_async_copy(wdn_hbm.at[layer, e], wdn_st, wsem.at[1]))

    e = blk_e_ref[i]

    @pl.when(i == 0)
    def _():
        for cp in weight_copies(e):
            cp.start()
        cp = pltpu.make_async_copy(h_hbm.at[pl.ds(tok0 * SUB, t_str * SUB)], hv.at[pl.ds(0, t_str * SUB)],
                                   iosem.at[0])
        cp.start()
        acc[...] = jnp.zeros_like(acc)
        hv[t_str * SUB:, :] = jnp.zeros((8 * SUB, LANES), f32)
        cp.wait()

    @pl.when((i == 0) | (e != blk_e_ref[jnp.maximum(i - 1, 0)]))
    def _():
        for cp in weight_copies(e):
            cp.wait()
        wgu_bf[...] = wgu_st[...].astype(bf16)
        wdn_bf[...] = wdn_st[...].astype(bf16)
        en = nxt_e_ref[e]

        @pl.when(en >= 0)
        def _():
            for cp in weight_copies(en):
                cp.start()

    def sub_block(r_lo):
        for r in range(MOE_SUB_ROWS):
            o = pl.multiple_of(src_ref[0, 0, r_lo + r] * SUB, SUB)
            xg[r * SUB:(r + 1) * SUB, :] = hv[pl.ds(o, SUB), :]
        xb = jnp.concatenate([xg[pl.ds(s, MOE_SUB_ROWS, stride=SUB), :] for s in range(SUB)],
                             axis=1).astype(bf16)
        gu = _dot(xb, wgu_bf[...]) + bgu_ref[...]
        gt = jnp.minimum(gu[:, :MOE_FF], SWIGLU_LIMIT)
        up = jnp.clip(gu[:, MOE_FF:], -SWIGLU_LIMIT, SWIGLU_LIMIT)
        act = (up + 1.0) * gt * jax.nn.sigmoid(SWIGLU_ALPHA * gt)
        y = _dot(act.astype(bf16), wdn_bf[...]) + bdn_ref[...]
        for s in range(SUB):
            xg[pl.ds(s, MOE_SUB_ROWS, stride=SUB), :] = y[:, s * LANES:(s + 1) * LANES]
        group = 16
        for r0 in range(0, MOE_SUB_ROWS, group):
            offs = [pl.multiple_of(src_ref[0, 0, r_lo + r] * SUB, SUB) for r in range(r0, r0 + group)]
            olds = [acc[pl.ds(o, SUB), :] for o in offs]
            for j, o in enumerate(offs):
                r = r0 + j
                gate = gates_ref[slot_ref[0, 0, r_lo + r]]
                acc[pl.ds(o, SUB), :] = olds[j] + xg[r * SUB:(r + 1) * SUB, :] * gate

    sub_block(0)
    for r_lo in range(MOE_SUB_ROWS, MOE_ROWS, MOE_SUB_ROWS):
        pl.when(blk_nv_ref[i] > r_lo)(functools.partial(sub_block, r_lo))

    @pl.when(i == n_used - 1)
    def _():
        cp = pltpu.make_async_copy(acc.at[pl.ds(0, t_str * SUB)], out_hbm, iosem.at[0])
        cp.start()
        cp.wait()


def _experts(blk_e, nxt_e, blk_nv, n_used, row_src, row_slot, gates, ht, wgu, bgu, wdn, bdn, layer, tok0,
             t_str):
    n_blocks = row_src.shape[0] // MOE_ROWS
    src3 = row_src.reshape(n_blocks, 1, MOE_ROWS)
    slot3 = row_slot.reshape(n_blocks, 1, MOE_ROWS)
    per_block = pl.BlockSpec((1, 1, MOE_ROWS), lambda i, be, ne, nv, nu: (i, 0, 0), memory_space=pltpu.SMEM)
    grid_spec = pltpu.PrefetchScalarGridSpec(
        num_scalar_prefetch=4,
        grid=(n_blocks,),
        in_specs=[
            per_block,
            per_block,
            pl.BlockSpec(memory_space=pltpu.SMEM),
            pl.BlockSpec(memory_space=pl.ANY),
            pl.BlockSpec(memory_space=pl.ANY),
            pl.BlockSpec((None, None, 1, 2 * MOE_FF), lambda i, be, ne, nv, nu: (layer, be[i], 0, 0)),
            pl.BlockSpec(memory_space=pl.ANY),
            pl.BlockSpec((None, None, 1, D), lambda i, be, ne, nv, nu: (layer, be[i], 0, 0)),
        ],
        out_specs=pl.BlockSpec(memory_space=pl.ANY),
        scratch_shapes=[
            pltpu.VMEM(((t_str + 8) * SUB, LANES), f32),
            pltpu.VMEM(((t_str + 8) * SUB, LANES), f32),
            pltpu.VMEM((D, 2 * MOE_FF), f32),
            pltpu.VMEM((MOE_FF, D), f32),
            pltpu.VMEM((D, 2 * MOE_FF), bf16),
            pltpu.VMEM((MOE_FF, D), bf16),
            pltpu.VMEM((MOE_SUB_ROWS * SUB, LANES), f32),
            pltpu.SemaphoreType.DMA((2,)),
            pltpu.SemaphoreType.DMA((1,)),
        ],
    )
    return pl.pallas_call(
        functools.partial(_experts_kernel, layer=layer, tok0=tok0),
        grid_spec=grid_spec,
        out_shape=jax.ShapeDtypeStruct((t_str * SUB, LANES), f32),
        compiler_params=_cparams(("arbitrary",), _EXPERTS_VMEM_LIMIT),
        name="experts",
    )(blk_e, nxt_e, blk_nv, n_used, src3, slot3, gates, ht, wgu, bgu.reshape(DEPTH, N_EXPERTS, 1, 2 * MOE_FF),
      wdn, bdn.reshape(DEPTH, N_EXPERTS, 1, D))


def _combine_kernel(x1_ref, moep_ref, moes_ref, mod_ref, fg_ref, o_ref, *, final):
    i = pl.program_id(0)
    rows = lambda ref: jnp.concatenate([ref[pl.ds(s, TM, stride=SUB), :] for s in range(SUB)], axis=1)
    moe = jnp.where(i < P_TILES, rows(moep_ref), rows(moes_ref))
    m = mod_ref[pl.ds(_mod_row(i), 1), :]
    x2 = x1_ref[...] + m[:, 5 * D:6 * D] * moe
    if final:
        x2 = x2 * lax.rsqrt(jnp.mean(x2 * x2, axis=-1, keepdims=True) + EPS) * fg_ref[...]
    o_ref[...] = x2


def _combine(x1, moep, moes, mod, fg, final):
    tile = pl.BlockSpec((TM, D), lambda i: (i, 0))
    res = lambda shape: pl.BlockSpec(shape, lambda i: (0, 0), pipeline_mode=pl.Buffered(1))
    ctx = pl.BlockSpec((TM * SUB, LANES), lambda i: (jnp.minimum(i, P_TILES - 1), 0))
    lat = pl.BlockSpec((TM * SUB, LANES), lambda i: (jnp.maximum(i - P_TILES, 0), 0))
    return pl.pallas_call(
        functools.partial(_combine_kernel, final=final),
        grid=(N_TILES,),
        in_specs=[tile, ctx, lat, res((8, 6 * D)), res((1, D))],
        out_specs=tile,
        out_shape=jax.ShapeDtypeStruct((T, D), f32),
        compiler_params=_cparams(("arbitrary",)),
        name="combine",
    )(x1, moep, moes, mod, fg)


def _rope_tables():
    nf = HD_A // 4
    inv = ROPE_BASE ** (-jnp.arange(nf, dtype=f32) / nf)
    n = jnp.arange(DEC_SEQ)
    row = (n // GRID_W).astype(f32)[:, None] * inv
    col = (n % GRID_W).astype(f32)[:, None] * inv
    cos64 = jnp.concatenate([jnp.cos(row), jnp.cos(row), jnp.cos(col), jnp.cos(col)], axis=-1)
    sin64 = jnp.concatenate([-jnp.sin(row), jnp.sin(row), -jnp.sin(col), jnp.sin(col)], axis=-1)
    cos_s = jnp.tile(jnp.tile(cos64, (1, 2)), (DEC_BATCH, 1))
    sin_s = jnp.tile(jnp.tile(sin64, (1, 2)), (DEC_BATCH, 1))
    cos = jnp.concatenate([jnp.ones((T_P, LANES), f32), cos_s], axis=0)
    sin = jnp.concatenate([jnp.zeros((T_P, LANES), f32), sin_s], axis=0)
    return cos, sin


def _routing_tables(idx, rank, gate, counts):
    t_str = idx.shape[1]
    n_blocks = (t_str * TOP_K + N_EXPERTS * (MOE_ROWS - 1)) // MOE_ROWS + 1
    padded = (counts + MOE_ROWS - 1) // MOE_ROWS * MOE_ROWS
    pad_end = jnp.cumsum(padded)
    pad_start = pad_end - padded
    eids = jnp.arange(N_EXPERTS, dtype=i32)
    start_of = jnp.sum(jnp.where(idx[..., None] == eids, pad_start, 0), axis=-1)
    dest = (start_of + rank).reshape(-1)
    inv = jnp.full((n_blocks * MOE_ROWS,), -1, i32).at[dest].set(jnp.arange(TOP_K * t_str, dtype=i32))
    valid = inv >= 0
    safe = jnp.maximum(inv, 0)
    row_src = jnp.where(valid, safe % t_str, t_str).astype(i32)
    row_slot = jnp.where(valid, inv, TOP_K * t_str).astype(i32)
    gates = jnp.concatenate([gate.reshape(-1), jnp.zeros((8,), f32)])
    starts = jnp.arange(n_blocks, dtype=i32) * MOE_ROWS
    blk_e = jnp.sum((pad_end[None, :] <= starts[:, None]).astype(i32), axis=1)
    present = counts > 0
    blk_e = jnp.minimum(blk_e, jnp.max(jnp.where(present, eids, 0))).astype(i32)
    later = (eids[None, :] > eids[:, None]) & present[None, :]
    nxt = jnp.min(jnp.where(later, eids[None, :], N_EXPERTS), axis=1)
    nxt_e = jnp.where(nxt < N_EXPERTS, nxt, -1).astype(i32)
    n_used = (pad_end[-1:] // MOE_ROWS).astype(i32)
    cnt_of = jnp.sum(jnp.where(blk_e[:, None] == eids, (counts + pad_start)[None, :], 0), axis=1)
    blk_nv = jnp.clip(cnt_of - starts, 0, MOE_ROWS).astype(i32)
    return blk_e, nxt_e, blk_nv, n_used, row_src, row_slot, gates


def kernel(x_prompt, x_sample, cache_k, cache_v, state_fwd, state_bwd, c, c_ctx, ada_w, ada_b, norm_mix_g,
           norm_ffn_g, w_in, diff_lambda, diff_subln_g, gla_gk_w, gla_gk_b, gla_norm_g, w_branch_a,
           w_branch_b, w_out, router_w, router_b, expert_w_gu, expert_b_gu, expert_w_down, expert_b_down,
           final_norm_g):
    x = jnp.concatenate([x_prompt.reshape(T_P, D), x_sample.reshape(T_S, D)], axis=0)
    cvec = jnp.concatenate([c_ctx[None], c, jnp.zeros((3, D), f32)], axis=0)
    mods = _ada_mod(cvec, ada_w, ada_b)
    cos, sin = _rope_tables()
    ck = cache_k.reshape(DEC_BATCH, DEPTH, PAST_LEN, D)
    cv = cache_v.reshape(DEC_BATCH, DEPTH, PAST_LEN, D)
    main_w = 6 * D
    lr0 = main_w
    g0 = main_w + 2 * GK_RANK

    ks, vs, sfs, sbs = [], [], [], []
    yn = None
    for l in range(DEPTH):
        lam_init = 0.8 - 0.6 * math.exp(-0.3 * l)
        mod = mods[l]
        wm = w_in[l, :, :main_w].astype(bf16)
        wl = jnp.pad(w_in[l, :, lr0:g0], ((0, 0), (0, LANES - 2 * GK_RANK))).astype(bf16)
        wg = w_in[l, :, g0:].astype(bf16)
        q, k, v, bq, bk, bv, bg, lr, ga, gb = _inproj(x, mod, norm_mix_g[l][None], wm, wg, wl, cos, sin)
        ks.append(k[:T_P].reshape(BATCH, SEQ, D))
        vs.append(v[:T_P].reshape(BATCH, SEQ, D))

        lq = jnp.pad(diff_lambda[l], ((0, 4), (0, LANES - HD_A)))
        sg = diff_subln_g[l][None]
        oap = _attn_ctx(q, k, v, lq, sg, lam_init)
        oas = _attn_lat(q, k, v, ck, cv, lq, sg, lam_init, l)

        wf = jnp.pad(gla_gk_w[l, 0], ((0, LANES - GK_RANK), (0, 0)))
        wb = jnp.pad(gla_gk_w[l, 1], ((GK_RANK, LANES - 2 * GK_RANK), (0, 0)))
        bfv = gla_gk_b[l, 0][None]
        bbv = gla_gk_b[l, 1][None]
        gn = gla_norm_g[l][None]
        ogp, sf, sb = _gla(bq, bk, bv, bg, lr, wf, wb, bfv, bbv, gn, SEQ, 0, BATCH)
        ogs = _gla(bq, bk, bv, bg, lr, wf, wb, bfv, bbv, gn, DEC_SEQ, T_P // DEC_SEQ, DEC_BATCH,
                   states=(state_fwd, state_bwd), layer=l)
        sfs.append(sf)
        sbs.append(sb)

        rb = jnp.broadcast_to(router_b[l][:, None], (N_EXPERTS, LANES))
        x1, ht, idx8, gate8, rank8, cnt = _merge(
            x, oap, oas, ogp, ogs, ga, gb, mod, w_branch_a[l].astype(bf16), w_branch_b[l].astype(bf16),
            w_out[l].astype(bf16), norm_ffn_g[l][None], router_w[l].T, rb)
        moes = []
        for tok0, t_str, cnt_s in ((0, T_P, cnt[:N_EXPERTS, 0]), (T_P, T_S, cnt[N_EXPERTS:, 0])):
            sl = slice(tok0, tok0 + t_str)
            tables = _routing_tables(idx8[:TOP_K, sl], rank8[:TOP_K, sl], gate8[:TOP_K, sl], cnt_s)
            moes.append(_experts(*tables, ht, expert_w_gu, expert_b_gu, expert_w_down, expert_b_down, l,
                                 tok0, t_str))
        x = _combine(x1, moes[0], moes[1], mod, final_norm_g[None], final=(l == DEPTH - 1))
    yn = x

    y_prompt = yn[:T_P].reshape(BATCH, SEQ, D)
    y_sample = yn[T_P:].reshape(DEC_BATCH, DEC_SEQ, D)
    new_k = jnp.stack(ks, axis=1).reshape(BATCH, DEPTH, SEQ, N_HEADS_A, 2, HD_A)
    new_v = jnp.stack(vs, axis=1).reshape(BATCH, DEPTH, SEQ, N_HEADS_A, 2 * HD_A)
    new_sf = jnp.stack(sfs, axis=1)
    new_sb = jnp.stack(sbs, axis=1)
    return (y_prompt, y_sample, new_k, new_v, new_sf, new_sb)
```
